```python
import jax, jax.numpy as jnp
from jax import lax
import numpy as np

D_MODEL = 2048
BATCH = 4
SEQ = 8192
DEPTH = 1
DEC_BATCH = 8
DEC_SEQ = 32
PAST_LEN = 4096

CHUNK = 64
D_CONV = D_MODEL
CONV_W = 3
GLA_HEADS = 4
GLA_DK = D_MODEL // 2
GLA_DV = D_MODEL
GLA_HK = GLA_DK // GLA_HEADS
GLA_HV = GLA_DV // GLA_HEADS
GATE_RANK = 16
GATE_NORMALIZER = 16.0
D_FF = 5632
D_PLE = 256
EPS = 1e-6
IN_SPLITS = (D_CONV, D_CONV, D_CONV, GLA_DK, GLA_DK, GLA_DV, GLA_DV, GATE_RANK, D_MODEL, D_MODEL)
D_IN = sum(IN_SPLITS)

kernel_name = "hybrid_shortconv_gla_convffn_stream_step"


def _rmsnorm(x, g):
    xf = x.astype(jnp.float32)
    r = lax.rsqrt(jnp.mean(xf * xf, axis=-1, keepdims=True) + EPS)
    return (xf * r * g.astype(jnp.float32)).astype(x.dtype)


def _causal_dwconv(u, buf, w, bias=None):
    T = u.shape[1]
    full = jnp.concatenate([buf.astype(u.dtype), u], axis=1)
    out = w[0] * full[:, 0:T]
    for j in range(1, CONV_W):
        out = out + w[j] * full[:, j:j + T]
    if bias is not None:
        out = out + bias
    return out, full[:, -(CONV_W - 1):]


def _gla_chunked(q, k, v, log_a, s0, chunk):
    out_dtype = v.dtype
    bsz, T = q.shape[0], q.shape[1]
    n = T // chunk
    f32 = jnp.float32

    def to_chunks(t):
        return t.astype(f32).reshape(bsz, n, chunk, *t.shape[2:]).swapaxes(0, 1)

    causal = jnp.tril(jnp.ones((chunk, chunk), dtype=bool))[None, :, :, None, None]

    def step(S, inp):
        qc, kc, vc, ac = inp
        b = jnp.cumsum(ac, axis=1)
        o_inter = jnp.einsum('blhk,bhkv->blhv', qc * jnp.exp(b), S)
        diff = b[:, :, None] - b[:, None, :]
        decay = jnp.exp(jnp.where(causal, diff, -jnp.inf))
        scores = jnp.einsum('bthk,btshk,bshk->bhts', qc, decay, kc)
        o_intra = jnp.einsum('bhts,bshv->bthv', scores, vc)
        b_last = b[:, -1]
        S_new = S * jnp.exp(b_last)[..., None] + jnp.einsum(
            'bshk,bshv->bhkv', kc * jnp.exp(b_last[:, None] - b), vc)
        return S_new, o_inter + o_intra

    S_fin, o = lax.scan(step, s0.astype(f32),
                        (to_chunks(q), to_chunks(k), to_chunks(v), to_chunks(log_a)))
    o = o.swapaxes(0, 1).reshape(bsz, T, q.shape[2], v.shape[-1])
    return o, S_fin.astype(out_dtype)


def _layer(x, p_i, conv_buf, gla_S, ffn_buf, norm_mix, w_in, conv_a_w, w_a_out,
           w_gate2, b_gate, gla_norm, w_b_out, w_o, norm_ffn, w_up, ffn_conv_w,
           ffn_conv_b, w_down, norm_ple, w_ple_gate, w_ple):
    bsz, T, _ = x.shape
    xn = _rmsnorm(x, norm_mix)
    proj = xn @ w_in
    offs = [0]
    for s in IN_SPLITS:
        offs.append(offs[-1] + s)
    b_a, c_a, v_a, q, k, v, g_o, a_lr, m_a, m_b = [
        proj[..., offs[j]:offs[j + 1]] for j in range(len(IN_SPLITS))]
    conv_out, new_conv_buf = _causal_dwconv(c_a * v_a, conv_buf, conv_a_w)
    y_a = (b_a * conv_out) @ w_a_out
    log_a = jax.nn.log_sigmoid((a_lr @ w_gate2 + b_gate).astype(jnp.float32)) / GATE_NORMALIZER
    qh = (q * (GLA_HK ** -0.5)).reshape(bsz, T, GLA_HEADS, GLA_HK)
    kh = k.reshape(bsz, T, GLA_HEADS, GLA_HK)
    vh = v.reshape(bsz, T, GLA_HEADS, GLA_HV)
    ah = log_a.reshape(bsz, T, GLA_HEADS, GLA_HK)
    o, new_S = _gla_chunked(qh, kh, vh, ah, gla_S, min(CHUNK, T))
    o = _rmsnorm(o, gla_norm.reshape(GLA_HEADS, GLA_HV))
    o = o * jax.nn.silu(g_o.reshape(bsz, T, GLA_HEADS, GLA_HV))
    y_b = o.reshape(bsz, T, GLA_DV) @ w_b_out
    x = x + (jax.nn.sigmoid(m_a) * y_a + jax.nn.sigmoid(m_b) * y_b) @ w_o
    hn = _rmsnorm(x, norm_ffn)
    u = hn @ w_up
    uc, new_ffn_buf = _causal_dwconv(u, ffn_buf, ffn_conv_w, ffn_conv_b)
    val, gate = uc[..., :D_FF], uc[..., D_FF:]
    x = x + (jax.nn.silu(gate) * val) @ w_down
    pn = _rmsnorm(x, norm_ple)
    x = x + jax.nn.sigmoid(pn @ w_ple_gate) * (p_i @ w_ple)
    return x, new_conv_buf, new_S, new_ffn_buf


def setup_inputs(seed: int = 0) -> dict:
    key = jax.random.key(seed)
    ks = jax.random.split(key, 32)
    f32 = jnp.float32

    def nrm(k, shape, scale=1.0):
        return jax.random.normal(k, shape, f32) * scale

    return {
        "x_prompt": nrm(ks[0], (BATCH, SEQ, D_MODEL)),
        "x_sample": nrm(ks[1], (DEC_BATCH, DEC_SEQ, D_MODEL)),
        "p_prompt": nrm(ks[2], (DEPTH, BATCH, SEQ, D_PLE)),
        "p_sample": nrm(ks[3], (DEPTH, DEC_BATCH, DEC_SEQ, D_PLE)),
        "state_conv_a": nrm(ks[4], (DEPTH, DEC_BATCH, CONV_W - 1, D_CONV)),
        "state_gla": nrm(ks[5], (DEPTH, DEC_BATCH, GLA_HEADS, GLA_HK, GLA_HV)),
        "state_ffn_conv": nrm(ks[6], (DEPTH, DEC_BATCH, CONV_W - 1, 2 * D_FF)),
        "norm_mix": 1.0 + nrm(ks[7], (DEPTH, D_MODEL), 0.02),
        "w_in": nrm(ks[8], (DEPTH, D_MODEL, D_IN), D_MODEL ** -0.5),
        "conv_a_w": nrm(ks[9], (DEPTH, CONV_W, D_CONV), CONV_W ** -0.5),
        "w_a_out": nrm(ks[10], (DEPTH, D_CONV, D_MODEL), D_CONV ** -0.5),
        "w_gate2": nrm(ks[11], (DEPTH, GATE_RANK, GLA_DK), GATE_RANK ** -0.5),
        "b_gate": nrm(ks[12], (DEPTH, GLA_DK), 0.02),
        "gla_norm": 1.0 + nrm(ks[13], (DEPTH, GLA_DV), 0.02),
        "w_b_out": nrm(ks[14], (DEPTH, GLA_DV, D_MODEL), GLA_DV ** -0.5),
        "w_o": nrm(ks[15], (DEPTH, D_MODEL, D_MODEL), D_MODEL ** -0.5),
        "norm_ffn": 1.0 + nrm(ks[16], (DEPTH, D_MODEL), 0.02),
        "w_up": nrm(ks[17], (DEPTH, D_MODEL, 2 * D_FF), D_MODEL ** -0.5),
        "ffn_conv_w": nrm(ks[18], (DEPTH, CONV_W, 2 * D_FF), CONV_W ** -0.5),
        "ffn_conv_b": nrm(ks[19], (DEPTH, 2 * D_FF), 0.02),
        "w_down": nrm(ks[20], (DEPTH, D_FF, D_MODEL), D_FF ** -0.5),
        "norm_ple": 1.0 + nrm(ks[21], (DEPTH, D_MODEL), 0.02),
        "w_ple_gate": nrm(ks[22], (DEPTH, D_MODEL, D_MODEL), D_MODEL ** -0.5),
        "w_ple": nrm(ks[23], (DEPTH, D_PLE, D_MODEL), D_PLE ** -0.5),
        "norm_final": 1.0 + nrm(ks[24], (D_MODEL,), 0.02),
    }


def reference(x_prompt, x_sample, p_prompt, p_sample, state_conv_a, state_gla,
              state_ffn_conv, norm_mix, w_in, conv_a_w, w_a_out, w_gate2, b_gate,
              gla_norm, w_b_out, w_o, norm_ffn, w_up, ffn_conv_w, ffn_conv_b, w_down,
              norm_ple, w_ple_gate, w_ple, norm_final):
    bp = x_prompt.shape[0]
    dt = x_prompt.dtype
    hp, hs = x_prompt, x_sample
    cap, gp, fp, cas, gs, fs = [], [], [], [], [], []
    for i in range(DEPTH):
        params = (norm_mix[i], w_in[i], conv_a_w[i], w_a_out[i], w_gate2[i], b_gate[i],
                  gla_norm[i], w_b_out[i], w_o[i], norm_ffn[i], w_up[i], ffn_conv_w[i],
                  ffn_conv_b[i], w_down[i], norm_ple[i], w_ple_gate[i], w_ple[i])
        hp, c1, s1, f1 = _layer(
            hp, p_prompt[i],
            jnp.zeros((bp, CONV_W - 1, D_CONV), dt),
            jnp.zeros((bp, GLA_HEADS, GLA_HK, GLA_HV), dt),
            jnp.zeros((bp, CONV_W - 1, 2 * D_FF), dt),
            *params)
        hs, c2, s2, f2 = _layer(hs, p_sample[i], state_conv_a[i], state_gla[i],
                                state_ffn_conv[i], *params)
        cap.append(c1); gp.append(s1); fp.append(f1)
        cas.append(c2); gs.append(s2); fs.append(f2)
    y_prompt = _rmsnorm(hp, norm_final)
    y_sample = _rmsnorm(hs, norm_final)
    return (y_prompt, y_sample, jnp.stack(cap), jnp.stack(gp), jnp.stack(fp),
            jnp.stack(cas), jnp.stack(gs), jnp.stack(fs))
```

```python
import functools

import jax
import jax.numpy as jnp
from jax import lax
from jax.experimental import pallas as pl
from jax.experimental.pallas import tpu as pltpu

F32 = jnp.float32
BF16 = jnp.bfloat16

D_MODEL = 2048
CHUNK = 64
CONV_W = 3
GLA_HEADS = 4
GLA_DK = D_MODEL // 2
GLA_DV = D_MODEL
GLA_HK = GLA_DK // GLA_HEADS
GLA_HV = GLA_DV // GLA_HEADS
GATE_RANK = 16
GATE_NORMALIZER = 16.0
D_FF = 5632
D_PLE = 256
EPS = 1e-6

D_MAIN = 3 * D_MODEL + 2 * GLA_DK + 2 * GLA_DV + 2 * D_MODEL
OFF_BA, OFF_CA, OFF_VA = 0, D_MODEL, 2 * D_MODEL
OFF_Q = 3 * D_MODEL
OFF_K = OFF_Q + GLA_DK
OFF_V = OFF_K + GLA_DK
OFF_GO = OFF_V + GLA_DV
OFF_MA = OFF_GO + GLA_DV
OFF_MB = OFF_MA + D_MODEL
ALR_OFF = OFF_MA
LANE = 128

VMEM_LIMIT = 56 * 1024 * 1024


def _rms_scale(x):
    return lax.rsqrt(jnp.mean(x * x, axis=-1, keepdims=True) + EPS)


def _sigmoid(x):
    return 1.0 / (1.0 + jnp.exp(-x))


def _log_sigmoid(z):
    return jnp.minimum(z, 0.0) - jnp.log(1.0 + jnp.exp(-jnp.abs(z)))


def _dot(a, b):
    return jnp.dot(a, b, preferred_element_type=F32)


def _conv3(u, h0, h1, w):
    t = u.shape[0]
    rows = lax.broadcasted_iota(jnp.int32, (t, 1), 0)
    sh1 = jnp.where(rows == 0, h1, pltpu.roll(u, 1, axis=0))
    sh2 = jnp.where(rows == 0, h0, jnp.where(rows == 1, h1, pltpu.roll(u, 2, axis=0)))
    return w[0:1] * sh2 + w[1:2] * sh1 + w[2:3] * u


def _inproj_kernel(x_ref, g_ref, w_ref, walr_ref, wg2_ref, bg_ref, proj_ref, loga_ref, xn_ref,
                   *, nb, tm):
    j = pl.program_id(2)
    rows = nb * tm

    @pl.when(j == 0)
    def _():
        x = x_ref[...].reshape(rows, D_MODEL)
        xn = (x * _rms_scale(x) * g_ref[...]).astype(BF16)
        xn_ref[...] = xn
        a_lr = _dot(xn, walr_ref[...])
        z = _dot(a_lr.astype(BF16), wg2_ref[...]) + bg_ref[...]
        loga_ref[...] = (_log_sigmoid(z) * (1.0 / GATE_NORMALIZER)).reshape(nb, tm, GLA_DK)

    tn = proj_ref.shape[-1]
    proj_ref[...] = _dot(xn_ref[...], w_ref[...]).astype(BF16).reshape(nb, tm, tn)


def _inproj(x, g, w_main, w_alr, w_g2, b_gate, *, nb, tm, tn):
    bsz, t, _ = x.shape
    grid = (bsz // nb, t // tm, D_MAIN // tn)
    return pl.pallas_call(
        functools.partial(_inproj_kernel, nb=nb, tm=tm),
        grid=grid,
        in_specs=[
            pl.BlockSpec((nb, tm, D_MODEL), lambda b, i, j: (b, i, 0)),
            pl.BlockSpec((1, D_MODEL), lambda b, i, j: (0, 0)),
            pl.BlockSpec((D_MODEL, tn), lambda b, i, j: (0, j)),
            pl.BlockSpec((D_MODEL, LANE), lambda b, i, j: (0, 0)),
            pl.BlockSpec((LANE, GLA_DK), lambda b, i, j: (0, 0)),
            pl.BlockSpec((1, GLA_DK), lambda b, i, j: (0, 0)),
        ],
        out_specs=[
            pl.BlockSpec((nb, tm, tn), lambda b, i, j: (b, i, j)),
            pl.BlockSpec((nb, tm, GLA_DK), lambda b, i, j: (b, i, 0)),
        ],
        out_shape=[
            jax.ShapeDtypeStruct((bsz, t, D_MAIN), BF16),
            jax.ShapeDtypeStruct((bsz, t, GLA_DK), F32),
        ],
        scratch_shapes=[pltpu.VMEM((nb * tm, D_MODEL), BF16)],
        compiler_params=pltpu.CompilerParams(
            dimension_semantics=("parallel", "parallel", "arbitrary"),
            vmem_limit_bytes=VMEM_LIMIT),
        name="inproj",
    )(x, g, w_main, w_alr, w_g2, b_gate)


def _gla_kernel(*refs, chunk, n_chunks, has_state):
    if has_state:
        q_ref, k_ref, v_ref, go_ref, la_ref, gn_ref, s0_ref, o_ref, sout_ref, s_ref = refs
    else:
        q_ref, k_ref, v_ref, go_ref, la_ref, gn_ref, o_ref, sout_ref, s_ref = refs
    i = pl.program_id(2)

    @pl.when(i == 0)
    def _():
        if has_state:
            s_ref[...] = s0_ref[0, 0]
        else:
            s_ref[...] = jnp.zeros_like(s_ref)

    r_i = lax.broadcasted_iota(jnp.int32, (chunk, chunk), 0)
    c_i = lax.broadcasted_iota(jnp.int32, (chunk, chunk), 1)
    causal = r_i >= c_i
    tril = causal.astype(BF16)
    ones_col = jnp.ones((chunk, LANE), BF16)
    tn_dims = (((0,), (0,)), ((), ()))
    nt_dims = (((1,), (1,)), ((), ()))

    for c in range(n_chunks):
        sl = pl.ds(c * chunk, chunk)
        la = la_ref[0, sl, :]
        la_hi = la.astype(BF16)
        la_lo = (la - la_hi.astype(F32)).astype(BF16)
        b = _dot(tril, la_hi) + _dot(tril, la_lo)
        q = q_ref[0, sl, :].astype(F32) * (GLA_HK ** -0.5)
        k = k_ref[0, sl, :].astype(F32)
        v = v_ref[0, sl, :]
        b_last = b[chunk - 1:chunk, :]
        qt = (q * jnp.exp(b)).astype(BF16)
        kh = (k * jnp.exp(-b)).astype(BF16)
        kt = (k * jnp.exp(b_last - b)).astype(BF16)
        scores = lax.dot_general(qt, kh, nt_dims, preferred_element_type=F32)
        p = jnp.where(causal, scores, 0.0).astype(BF16)
        s_old = s_ref[...]
        o = _dot(qt, s_old.astype(BF16)) + _dot(p, v)
        dcol = (lax.dot_general(la_hi, ones_col, tn_dims, preferred_element_type=F32)
                + lax.dot_general(la_lo, ones_col, tn_dims, preferred_element_type=F32))
        decay = jnp.tile(jnp.exp(dcol), (1, GLA_HV // LANE))
        s_ref[...] = s_old * decay + lax.dot_general(kt, v, tn_dims, preferred_element_type=F32)
        g = go_ref[0, sl, :].astype(F32)
        o_ref[0, sl, :] = (o * _rms_scale(o) * gn_ref[...] * (g * _sigmoid(g))).astype(BF16)

    @pl.when(i == pl.num_programs(2) - 1)
    def _():
        sout_ref[0, 0] = s_ref[...]


def _gla(proj, loga, gla_norm, state, *, tq, chunk):
    bsz, t, _ = proj.shape
    has_state = state is not None
    grid = (bsz, GLA_HEADS, t // tq)
    in_specs = [
        pl.BlockSpec((1, tq, GLA_HK), lambda b, h, i: (b, i, OFF_Q // GLA_HK + h)),
        pl.BlockSpec((1, tq, GLA_HK), lambda b, h, i: (b, i, OFF_K // GLA_HK + h)),
        pl.BlockSpec((1, tq, GLA_HV), lambda b, h, i: (b, i, OFF_V // GLA_HV + h)),
        pl.BlockSpec((1, tq, GLA_HV), lambda b, h, i: (b, i, OFF_GO // GLA_HV + h)),
        pl.BlockSpec((1, tq, GLA_HK), lambda b, h, i: (b, i, h)),
        pl.BlockSpec((1, GLA_HV), lambda b, h, i: (0, h)),
    ]
    args = [proj, proj, proj, proj, loga, gla_norm]
    if has_state:
        in_specs.append(pl.BlockSpec((1, 1, GLA_HK, GLA_HV), lambda b, h, i: (b, h, 0, 0)))
        args.append(state)
    return pl.pallas_call(
        functools.partial(_gla_kernel, chunk=chunk, n_chunks=tq // chunk, has_state=has_state),
        grid=grid,
        in_specs=in_specs,
        out_specs=[
            pl.BlockSpec((1, tq, GLA_HV), lambda b, h, i: (b, i, h)),
            pl.BlockSpec((1, 1, GLA_HK, GLA_HV), lambda b, h, i: (b, h, 0, 0)),
        ],
        out_shape=[
            jax.ShapeDtypeStruct((bsz, t, GLA_DV), BF16),
            jax.ShapeDtypeStruct((bsz, GLA_HEADS, GLA_HK, GLA_HV), F32),
        ],
        scratch_shapes=[pltpu.VMEM((GLA_HK, GLA_HV), F32)],
        compiler_params=pltpu.CompilerParams(
            dimension_semantics=("parallel", "parallel", "arbitrary"),
            vmem_limit_bytes=VMEM_LIMIT),
        name="gla",
    )(*args)


def _mix_kernel(*refs, nb, tm, has_state):
    if has_state:
        (ba_ref, ca_ref, va_ref, ma_ref, mb_ref, og_ref, x_ref, cw_ref, st_ref,
         wa_ref, wb_ref, wo_ref, x1_ref, newc_ref, carry_ref) = refs
    else:
        (ba_ref, ca_ref, va_ref, ma_ref, mb_ref, og_ref, x_ref, cw_ref,
         wa_ref, wb_ref, wo_ref, x1_ref, newc_ref, carry_ref) = refs
    i = pl.program_id(1)
    rows = nb * tm
    first = i == 0
    cw = cw_ref[...]

    conv_parts = []
    for s in range(nb):
        cv = ca_ref[s].astype(F32) * va_ref[s].astype(F32)
        init = st_ref[s] if has_state else jnp.zeros((CONV_W - 1, D_MODEL), F32)
        halo = jnp.where(first, init, carry_ref[s, 0:2, :])
        conv_parts.append(_conv3(cv, halo[0:1], halo[1:2], cw))
        tail = cv[tm - 2:tm, :]
        carry_ref[s, 0:2, :] = tail
        newc_ref[s] = tail
    conv = conv_parts[0] if nb == 1 else jnp.concatenate(conv_parts, axis=0)

    b_a = ba_ref[...].reshape(rows, D_MODEL).astype(F32)
    y_a = _dot((b_a * conv).astype(BF16), wa_ref[...])
    y_b = _dot(og_ref[...].reshape(rows, D_MODEL), wb_ref[...])
    m_a = ma_ref[...].reshape(rows, D_MODEL).astype(F32)
    m_b = mb_ref[...].reshape(rows, D_MODEL).astype(F32)
    z = _sigmoid(m_a) * y_a + _sigmoid(m_b) * y_b
    x1 = x_ref[...].reshape(rows, D_MODEL) + _dot(z.astype(BF16), wo_ref[...])
    x1_ref[...] = x1.reshape(nb, tm, D_MODEL)


def _mix(proj, og, x, conv_w, state, w_a, w_b, w_o, *, nb, tm):
    bsz, t, _ = x.shape
    has_state = state is not None
    grid = (bsz // nb, t // tm)

    def col(off):
        return pl.BlockSpec((nb, tm, D_MODEL), lambda b, i: (b, i, off // D_MODEL))

    row_spec = pl.BlockSpec((nb, tm, D_MODEL), lambda b, i: (b, i, 0))
    w_spec = pl.BlockSpec((D_MODEL, D_MODEL), lambda b, i: (0, 0), pipeline_mode=pl.Buffered(1))
    in_specs = [col(OFF_BA), col(OFF_CA), col(OFF_VA), col(OFF_MA), col(OFF_MB), row_spec, row_spec,
                pl.BlockSpec((CONV_W, D_MODEL), lambda b, i: (0, 0))]
    args = [proj, proj, proj, proj, proj, og, x, conv_w]
    if has_state:
        in_specs.append(pl.BlockSpec((nb, CONV_W - 1, D_MODEL), lambda b, i: (b, 0, 0)))
        args.append(state)
    in_specs += [w_spec, w_spec, w_spec]
    args += [w_a, w_b, w_o]
    return pl.pallas_call(
        functools.partial(_mix_kernel, nb=nb, tm=tm, has_state=has_state),
        grid=grid,
        in_specs=in_specs,
        out_specs=[
            row_spec,
            pl.BlockSpec((nb, CONV_W - 1, D_MODEL), lambda b, i: (b, 0, 0)),
        ],
        out_shape=[
            jax.ShapeDtypeStruct((bsz, t, D_MODEL), F32),
            jax.ShapeDtypeStruct((bsz, CONV_W - 1, D_MODEL), F32),
        ],
        scratch_shapes=[pltpu.VMEM((nb, 8, D_MODEL), F32)],
        compiler_params=pltpu.CompilerParams(
            dimension_semantics=("parallel", "arbitrary"),
            vmem_limit_bytes=VMEM_LIMIT),
        name="mix",
    )(*args)


def _ffn_up_kernel(*refs, nb, tm, tn, has_state):
    if has_state:
        (x1_ref, g_ref, wv_ref, wg_ref, cwv_ref, cwg_ref, bv_ref, bg_ref, stv_ref, stg_ref,
         h_ref, nfv_ref, nfg_ref, hn_ref, cv_ref, cg_ref) = refs
    else:
        (x1_ref, g_ref, wv_ref, wg_ref, cwv_ref, cwg_ref, bv_ref, bg_ref,
         h_ref, nfv_ref, nfg_ref, hn_ref, cv_ref, cg_ref) = refs
    i = pl.program_id(1)
    j = pl.program_id(2)
    rows = nb * tm
    first = i == 0

    @pl.when(j == 0)
    def _():
        x1 = x1_ref[...].reshape(rows, D_MODEL)
        hn_ref[...] = (x1 * _rms_scale(x1) * g_ref[...]).astype(BF16)

    hn = hn_ref[...]
    uv = _dot(hn, wv_ref[...])
    ug = _dot(hn, wg_ref[...])
    cwv = cwv_ref[...]
    cwg = cwg_ref[...]
    for s in range(nb):
        uvs = uv[s * tm:(s + 1) * tm]
        ugs = ug[s * tm:(s + 1) * tm]
        zero = jnp.zeros((CONV_W - 1, tn), F32)
        halo_v = jnp.where(first, stv_ref[s] if has_state else zero, cv_ref[j, s, 0:2, :])
        halo_g = jnp.where(first, stg_ref[s] if has_state else zero, cg_ref[j, s, 0:2, :])
        val = _conv3(uvs, halo_v[0:1], halo_v[1:2], cwv) + bv_ref[...]
        gate = _conv3(ugs, halo_g[0:1], halo_g[1:2], cwg) + bg_ref[...]
        tail_v = uvs[tm - 2:tm, :]
        tail_g = ugs[tm - 2:tm, :]
        cv_ref[j, s, 0:2, :] = tail_v
        cg_ref[j, s, 0:2, :] = tail_g
        nfv_ref[s] = tail_v
        nfg_ref[s] = tail_g
        h_ref[s] = (gate * _sigmoid(gate) * val).astype(BF16)


def _ffn_up(x1, g, w_up, conv_w, conv_b, state, *, nb, tm, tn):
    bsz, t, _ = x1.shape
    has_state = state is not None
    n_ct = D_FF // tn
    grid = (bsz // nb, t // tm, n_ct)
    in_specs = [
        pl.BlockSpec((nb, tm, D_MODEL), lambda b, i, j: (b, i, 0)),
        pl.BlockSpec((1, D_MODEL), lambda b, i, j: (0, 0)),
        pl.BlockSpec((D_MODEL, tn), lambda b, i, j: (0, j)),
        pl.BlockSpec((D_MODEL, tn), lambda b, i, j: (0, n_ct + j)),
        pl.BlockSpec((CONV_W, tn), lambda b, i, j: (0, j)),
        pl.BlockSpec((CONV_W, tn), lambda b, i, j: (0, n_ct + j)),
        pl.BlockSpec((1, tn), lambda b, i, j: (0, j)),
        pl.BlockSpec((1, tn), lambda b, i, j: (0, n_ct + j)),
    ]
    args = [x1, g, w_up, w_up, conv_w, conv_w, conv_b, conv_b]
    if has_state:
        in_specs += [
            pl.BlockSpec((nb, CONV_W - 1, tn), lambda b, i, j: (b, 0, j)),
            pl.BlockSpec((nb, CONV_W - 1, tn), lambda b, i, j: (b, 0, n_ct + j)),
        ]
        args += [state, state]
    tail_spec = pl.BlockSpec((nb, CONV_W - 1, tn), lambda b, i, j: (b, 0, j))
    return pl.pallas_call(
        functools.partial(_ffn_up_kernel, nb=nb, tm=tm, tn=tn, has_state=has_state),
        grid=grid,
        in_specs=in_specs,
        out_specs=[
            pl.BlockSpec((nb, tm, tn), lambda b, i, j: (b, i, j)),
            tail_spec,
            tail_spec,
        ],
        out_shape=[
            jax.ShapeDtypeStruct((bsz, t, D_FF), BF16),
            jax.ShapeDtypeStruct((bsz, CONV_W - 1, D_FF), F32),
            jax.ShapeDtypeStruct((bsz, CONV_W - 1, D_FF), F32),
        ],
        scratch_shapes=[
            pltpu.VMEM((nb * tm, D_MODEL), BF16),
            pltpu.VMEM((n_ct, nb, 8, tn), F32),
            pltpu.VMEM((n_ct, nb, 8, tn), F32),
        ],
        compiler_params=pltpu.CompilerParams(
            dimension_semantics=("parallel", "arbitrary", "arbitrary"),
            vmem_limit_bytes=VMEM_LIMIT),
        name="ffn_up",
    )(*args)


def _ffn_down_kernel(h_ref, x1_ref, p_ref, wd_ref, gp_ref, wpg_ref, wple_ref, gf_ref, y_ref):
    x2 = x1_ref[...] + _dot(h_ref[...], wd_ref[...])
    pn = (x2 * _rms_scale(x2) * gp_ref[...]).astype(BF16)
    gate = _sigmoid(_dot(pn, wpg_ref[...]))
    pe = _dot(p_ref[...].astype(BF16), wple_ref[...])
    x3 = x2 + gate * pe
    y_ref[...] = x3 * _rms_scale(x3) * gf_ref[...]


def _ffn_down(h, x1, p, w_down, g_ple, w_pg, w_ple, g_final, *, tm):
    m = x1.shape[0]
    const = dict(pipeline_mode=pl.Buffered(1))
    return pl.pallas_call(
        _ffn_down_kernel,
        grid=(m // tm,),
        in_specs=[
            pl.BlockSpec((tm, D_FF), lambda i: (i, 0)),
            pl.BlockSpec((tm, D_MODEL), lambda i: (i, 0)),
            pl.BlockSpec((tm, D_PLE), lambda i: (i, 0)),
            pl.BlockSpec((D_FF, D_MODEL), lambda i: (0, 0), **const),
            pl.BlockSpec((1, D_MODEL), lambda i: (0, 0)),
            pl.BlockSpec((D_MODEL, D_MODEL), lambda i: (0, 0), **const),
            pl.BlockSpec((D_PLE, D_MODEL), lambda i: (0, 0), **const),
            pl.BlockSpec((1, D_MODEL), lambda i: (0, 0)),
        ],
        out_specs=pl.BlockSpec((tm, D_MODEL), lambda i: (i, 0)),
        out_shape=jax.ShapeDtypeStruct((m, D_MODEL), F32),
        compiler_params=pltpu.CompilerParams(
            dimension_semantics=("parallel",),
            vmem_limit_bytes=VMEM_LIMIT),
        name="ffn_down",
    )(h, x1, p, w_down, g_ple, w_pg, w_ple, g_final)


def _tiles(bsz, t):
    if t >= 1024:
        return dict(nb=1, tm_in=1024, tm_mix=256, tm_up=512, tq=256)
    return dict(nb=bsz, tm_in=t, tm_mix=t, tm_up=t, tq=t)


def _layer(x, p, conv_state, gla_state, ffn_state, w, g_final):
    bsz, t, _ = x.shape
    cfg = _tiles(bsz, t)
    nb = cfg["nb"]
    chunk = min(CHUNK, t)
    proj, loga = _inproj(x, w["norm_mix"], w["w_main"], w["w_alr"], w["w_g2"], w["b_gate"],
                         nb=nb, tm=cfg["tm_in"], tn=1024)
    og, new_s = _gla(proj, loga, w["gla_norm"], gla_state, tq=cfg["tq"], chunk=chunk)
    x1, new_conv = _mix(proj, og, x, w["conv_a_w"], conv_state, w["w_a_out"], w["w_b_out"], w["w_o"],
                        nb=nb, tm=cfg["tm_mix"])
    h, nf_v, nf_g = _ffn_up(x1, w["norm_ffn"], w["w_up"], w["ffn_conv_w"], w["ffn_conv_b"], ffn_state,
                            nb=nb, tm=cfg["tm_up"], tn=512)
    m = bsz * t
    y = _ffn_down(h.reshape(m, D_FF), x1.reshape(m, D_MODEL), p.reshape(m, D_PLE),
                  w["w_down"], w["norm_ple"], w["w_ple_gate"], w["w_ple"], g_final, tm=256)
    return y.reshape(bsz, t, D_MODEL), new_conv, new_s, jnp.concatenate([nf_v, nf_g], axis=-1)


def kernel(x_prompt, x_sample, p_prompt, p_sample, state_conv_a, state_gla, state_ffn_conv, norm_mix, w_in, conv_a_w, w_a_out, w_gate2, b_gate, gla_norm, w_b_out, w_o, norm_ffn, w_up, ffn_conv_w, ffn_conv_b, w_down, norm_ple, w_ple_gate, w_ple, norm_final):
    depth = w_in.shape[0]
    assert depth == 1, "the final norm is fused into the last layer; one layer supported"
    i = 0
    w_in_i = w_in[i]
    w_alr = jnp.pad(w_in_i[:, ALR_OFF:ALR_OFF + GATE_RANK], ((0, 0), (0, LANE - GATE_RANK)))
    w = dict(
        norm_mix=norm_mix[i][None],
        w_main=jnp.concatenate([w_in_i[:, :ALR_OFF], w_in_i[:, ALR_OFF + GATE_RANK:]], axis=1).astype(BF16),
        w_alr=w_alr.astype(BF16),
        w_g2=jnp.pad(w_gate2[i], ((0, LANE - GATE_RANK), (0, 0))).astype(BF16),
        b_gate=b_gate[i][None],
        gla_norm=gla_norm[i][None],
        conv_a_w=conv_a_w[i],
        w_a_out=w_a_out[i].astype(BF16),
        w_b_out=w_b_out[i].astype(BF16),
        w_o=w_o[i].astype(BF16),
        norm_ffn=norm_ffn[i][None],
        w_up=w_up[i].astype(BF16),
        ffn_conv_w=ffn_conv_w[i],
        ffn_conv_b=ffn_conv_b[i][None],
        w_down=w_down[i].astype(BF16),
        norm_ple=norm_ple[i][None],
        w_ple_gate=w_ple_gate[i].astype(BF16),
        w_ple=w_ple[i].astype(BF16),
    )
    g_final = norm_final[None]
    yp, c1, s1, f1 = _layer(x_prompt, p_prompt[i], None, None, None, w, g_final)
    ys, c2, s2, f2 = _layer(x_sample, p_sample[i], state_conv_a[i], state_gla[i], state_ffn_conv[i],
                            w, g_final)
    return (yp, ys, c1[None], s1[None], f1[None], c2[None], s2[None], f2[None])
```

```python
import functools

import jax
import jax.numpy as jnp
from jax import lax
from jax.experimental import pallas as pl
from jax.experimental.pallas import tpu as pltpu

F32 = jnp.float32
BF16 = jnp.bfloat16

D_MODEL = 2048
CONV_W = 3
GLA_HEADS = 4
GLA_DK = D_MODEL // 2
GLA_DV = D_MODEL
GLA_HK = GLA_DK // GLA_HEADS
GLA_HV = GLA_DV // GLA_HEADS
GATE_RANK = 16
GATE_NORMALIZER = 16.0
D_FF = 5632
D_PLE = 256
EPS = 1e-6

D_MAIN = 3 * D_MODEL + 2 * GLA_DK + 2 * GLA_DV + 2 * D_MODEL
OFF_BA, OFF_CA, OFF_VA = 0, D_MODEL, 2 * D_MODEL
OFF_Q = 3 * D_MODEL
OFF_K = OFF_Q + GLA_DK
OFF_V = OFF_K + GLA_DK
OFF_GO = OFF_V + GLA_DV
OFF_MA = OFF_GO + GLA_DV
OFF_MB = OFF_MA + D_MODEL
ALR_OFF = OFF_MA
LANE = 128
BF16_ROWS = 16
HALO = 8
STRIP = 16
MXU_N = 256
M_SPLIT = 4

VMEM_LIMIT = 56 * 1024 * 1024

NT_DIMS = (((1,), (1,)), ((), ()))
TN_DIMS = (((0,), (0,)), ((), ()))


def _rms_scale(x):
    return lax.rsqrt(jnp.mean(x * x, axis=-1, keepdims=True) + EPS)


def _sigmoid(x):
    return 1.0 / (1.0 + jnp.exp(-x))


def _log_sigmoid(z):
    return jnp.minimum(z, 0.0) - jnp.log(1.0 + jnp.exp(-jnp.abs(z)))


def _dot(a, b):
    return jnp.dot(a, b, preferred_element_type=F32)


def _split_bf16(x):
    hi = x.astype(BF16)
    return hi, (x - hi.astype(F32)).astype(BF16)


def _conv3(u, h0, h1, w):
    t = u.shape[0]
    rows = lax.broadcasted_iota(jnp.int32, (t, 1), 0)
    sh1 = jnp.where(rows == 0, h1, pltpu.roll(u, 1, axis=0))
    sh2 = jnp.where(rows == 0, h0, jnp.where(rows == 1, h1, pltpu.roll(u, 2, axis=0)))
    return w[0:1] * sh2 + w[1:2] * sh1 + w[2:3] * u


def _inproj_kernel(x_ref, g_ref, w_ref, walr_ref, wg2_ref, bg_ref, proj_ref, loga_ref, xn_ref,
                   *, nb, tm):
    j = pl.program_id(2)
    rows = nb * tm

    @pl.when(j == 0)
    def _():
        x = x_ref[...].reshape(rows, D_MODEL)
        xn = (x * _rms_scale(x) * g_ref[...]).astype(BF16)
        xn_ref[...] = xn
        a_lr = _dot(xn, walr_ref[...])
        z = _dot(a_lr.astype(BF16), wg2_ref[...]) + bg_ref[...]
        loga_ref[...] = (_log_sigmoid(z) * (1.0 / GATE_NORMALIZER)).reshape(nb, tm, GLA_DK)

    tn = proj_ref.shape[-1]
    proj_ref[...] = _dot(xn_ref[...], w_ref[...]).astype(BF16).reshape(nb, tm, tn)


def _inproj(x, g, w_main, w_alr, w_g2, b_gate, *, nb, tm, tn):
    bsz, t, _ = x.shape
    grid = (bsz // nb, t // tm, D_MAIN // tn)
    return pl.pallas_call(
        functools.partial(_inproj_kernel, nb=nb, tm=tm),
        grid=grid,
        in_specs=[
            pl.BlockSpec((nb, tm, D_MODEL), lambda b, i, j: (b, i, 0)),
            pl.BlockSpec((1, D_MODEL), lambda b, i, j: (0, 0)),
            pl.BlockSpec((D_MODEL, tn), lambda b, i, j: (0, j)),
            pl.BlockSpec((D_MODEL, LANE), lambda b, i, j: (0, 0)),
            pl.BlockSpec((LANE, GLA_DK), lambda b, i, j: (0, 0)),
            pl.BlockSpec((1, GLA_DK), lambda b, i, j: (0, 0)),
        ],
        out_specs=[
            pl.BlockSpec((nb, tm, tn), lambda b, i, j: (b, i, j)),
            pl.BlockSpec((nb, tm, GLA_DK), lambda b, i, j: (b, i, 0)),
        ],
        out_shape=[
            jax.ShapeDtypeStruct((bsz, t, D_MAIN), BF16),
            jax.ShapeDtypeStruct((bsz, t, GLA_DK), F32),
        ],
        scratch_shapes=[pltpu.VMEM((nb * tm, D_MODEL), BF16)],
        compiler_params=pltpu.CompilerParams(
            dimension_semantics=("parallel", "parallel", "arbitrary"),
            vmem_limit_bytes=VMEM_LIMIT),
        name="inproj",
    )(x, g, w_main, w_alr, w_g2, b_gate)


def _gla_head(q, k, v, la, s_old, *, diag):
    rows = q.shape[0]
    n_d = rows // diag
    r_i = lax.broadcasted_iota(jnp.int32, (rows, rows), 0)
    c_i = lax.broadcasted_iota(jnp.int32, (rows, rows), 1)
    tril = (r_i >= c_i).astype(BF16)
    la_hi, la_lo = _split_bf16(la)
    b = _dot(tril, la_hi) + _dot(tril, la_lo)
    b_end = b[rows - 1:rows, :]

    causal_d = (lax.broadcasted_iota(jnp.int32, (diag, diag), 0)
                >= lax.broadcasted_iota(jnp.int32, (diag, diag), 1))
    a_c, k_c, cen = [], [], []
    for d in range(n_d):
        lo = d * diag
        mid = lo + diag // 2
        c = b[mid - 1:mid, :]
        bd = b[lo:lo + diag, :]
        cen.append(c)
        a_c.append(q[lo:lo + diag, :] * jnp.exp(bd - c))
        k_c.append(k[lo:lo + diag, :] * jnp.exp(c - bd))

    def block(lo_d, hi_d):
        if hi_d - lo_d == 1:
            s = lax.dot_general(a_c[lo_d].astype(BF16), k_c[lo_d].astype(BF16), NT_DIMS,
                                preferred_element_type=F32)
            return jnp.where(causal_d, s, 0.0)
        mid_d = (lo_d + hi_d) // 2
        ref = b[mid_d * diag - 1:mid_d * diag, :]
        lhs = [a_c[d] * jnp.exp(cen[d] - ref) for d in range(mid_d, hi_d)]
        rhs = [k_c[d] * jnp.exp(ref - cen[d]) for d in range(lo_d, mid_d)]
        lhs = (lhs[0] if len(lhs) == 1 else jnp.concatenate(lhs, axis=0)).astype(BF16)
        rhs = (rhs[0] if len(rhs) == 1 else jnp.concatenate(rhs, axis=0)).astype(BF16)
        off = lax.dot_general(lhs, rhs, NT_DIMS, preferred_element_type=F32)
        top = jnp.concatenate([block(lo_d, mid_d), jnp.zeros_like(off)], axis=1)
        bot = jnp.concatenate([off, block(mid_d, hi_d)], axis=1)
        return jnp.concatenate([top, bot], axis=0)

    p = block(0, n_d).astype(BF16)
    qt = [a_c[d] * jnp.exp(cen[d]) for d in range(n_d)]
    kt = [k_c[d] * jnp.exp(b_end - cen[d]) for d in range(n_d)]
    qt = (qt[0] if n_d == 1 else jnp.concatenate(qt, axis=0)).astype(BF16)
    kt = (kt[0] if n_d == 1 else jnp.concatenate(kt, axis=0)).astype(BF16)
    o = _dot(qt, s_old.astype(BF16)) + _dot(p, v)
    e_rep = jnp.broadcast_to(b_end * (1.0 / BF16_ROWS), (BF16_ROWS, b_end.shape[1]))
    e_hi, e_lo = _split_bf16(e_rep)
    ones_col = jnp.ones((BF16_ROWS, LANE), BF16)
    dcol = (lax.dot_general(e_hi, ones_col, TN_DIMS, preferred_element_type=F32)
            + lax.dot_general(e_lo, ones_col, TN_DIMS, preferred_element_type=F32))
    decay = jnp.tile(jnp.exp(dcol), (1, v.shape[1] // LANE))
    s_new = s_old * decay + lax.dot_general(kt, v, TN_DIMS, preferred_element_type=F32)
    return o, s_new


def _gla_kernel(*refs, diag, has_state):
    if has_state:
        q_ref, k_ref, v_ref, go_ref, la_ref, gn_ref, s0_ref, o_ref, sout_ref, s_ref = refs
    else:
        q_ref, k_ref, v_ref, go_ref, la_ref, gn_ref, o_ref, sout_ref, s_ref = refs
    i = pl.program_id(1)

    @pl.when(i == 0)
    def _():
        if has_state:
            s_ref[...] = s0_ref[0]
        else:
            s_ref[...] = jnp.zeros_like(s_ref)

    for h in range(GLA_HEADS):
        ck = pl.ds(h * GLA_HK, GLA_HK)
        cv = pl.ds(h * GLA_HV, GLA_HV)
        q = q_ref[0, :, ck].astype(F32) * (GLA_HK ** -0.5)
        k = k_ref[0, :, ck].astype(F32)
        o, s_new = _gla_head(q, k, v_ref[0, :, cv], la_ref[0, :, ck], s_ref[h], diag=diag)
        s_ref[h] = s_new
        g = go_ref[0, :, cv].astype(F32)
        o_ref[0, :, cv] = (o * _rms_scale(o) * gn_ref[:, cv] * (g * _sigmoid(g))).astype(BF16)

    @pl.when(i == pl.num_programs(1) - 1)
    def _():
        sout_ref[0] = s_ref[...]


def _gla(proj, loga, gla_norm, state, *, tq, diag):
    bsz, t, _ = proj.shape
    has_state = state is not None
    grid = (bsz, t // tq)
    state_spec = pl.BlockSpec((1, GLA_HEADS, GLA_HK, GLA_HV), lambda b, i: (b, 0, 0, 0))
    in_specs = [
        pl.BlockSpec((1, tq, GLA_DK), lambda b, i: (b, i, OFF_Q // GLA_DK)),
        pl.BlockSpec((1, tq, GLA_DK), lambda b, i: (b, i, OFF_K // GLA_DK)),
        pl.BlockSpec((1, tq, GLA_DV), lambda b, i: (b, i, OFF_V // GLA_DV)),
        pl.BlockSpec((1, tq, GLA_DV), lambda b, i: (b, i, OFF_GO // GLA_DV)),
        pl.BlockSpec((1, tq, GLA_DK), lambda b, i: (b, i, 0)),
        pl.BlockSpec((1, GLA_DV), lambda b, i: (0, 0)),
    ]
    args = [proj, proj, proj, proj, loga, gla_norm]
    if has_state:
        in_specs.append(state_spec)
        args.append(state)
    return pl.pallas_call(
        functools.partial(_gla_kernel, diag=diag, has_state=has_state),
        grid=grid,
        in_specs=in_specs,
        out_specs=[
            pl.BlockSpec((1, tq, GLA_DV), lambda b, i: (b, i, 0)),
            state_spec,
        ],
        out_shape=[
            jax.ShapeDtypeStruct((bsz, t, GLA_DV), BF16),
            jax.ShapeDtypeStruct((bsz, GLA_HEADS, GLA_HK, GLA_HV), F32),
        ],
        scratch_shapes=[pltpu.VMEM((GLA_HEADS, GLA_HK, GLA_HV), F32)],
        compiler_params=pltpu.CompilerParams(
            dimension_semantics=("parallel", "arbitrary"),
            vmem_limit_bytes=VMEM_LIMIT),
        name="gla",
    )(*args)


def _mix_kernel(*refs, nb, tm, has_state):
    if has_state:
        (ba_ref, ca_ref, va_ref, ma_ref, mb_ref, og_ref, x_ref, cw_ref, st_ref,
         wa_ref, wb_ref, wo_ref, x1_ref, newc_ref, carry_ref) = refs
    else:
        (ba_ref, ca_ref, va_ref, ma_ref, mb_ref, og_ref, x_ref, cw_ref,
         wa_ref, wb_ref, wo_ref, x1_ref, newc_ref, carry_ref) = refs
    i = pl.program_id(1)
    rows = nb * tm
    first = i == 0
    cw = cw_ref[...]

    conv_parts = []
    for s in range(nb):
        cv = ca_ref[s].astype(F32) * va_ref[s].astype(F32)
        init = st_ref[s] if has_state else jnp.zeros((CONV_W - 1, D_MODEL), F32)
        halo = jnp.where(first, init, carry_ref[s, 0:2, :])
        conv_parts.append(_conv3(cv, halo[0:1], halo[1:2], cw))
        tail = cv[tm - 2:tm, :]
        carry_ref[s, 0:2, :] = tail
        newc_ref[s] = tail
    conv = conv_parts[0] if nb == 1 else jnp.concatenate(conv_parts, axis=0)

    b_a = ba_ref[...].reshape(rows, D_MODEL).astype(F32)
    y_a = _dot((b_a * conv).astype(BF16), wa_ref[...])
    y_b = _dot(og_ref[...].reshape(rows, D_MODEL), wb_ref[...])
    m_a = ma_ref[...].reshape(rows, D_MODEL).astype(F32)
    m_b = mb_ref[...].reshape(rows, D_MODEL).astype(F32)
    z = _sigmoid(m_a) * y_a + _sigmoid(m_b) * y_b
    x1 = x_ref[...].reshape(rows, D_MODEL) + _dot(z.astype(BF16), wo_ref[...])
    x1_ref[...] = x1.reshape(nb, tm, D_MODEL)


def _mix(proj, og, x, conv_w, state, w_a, w_b, w_o, *, nb, tm):
    bsz, t, _ = x.shape
    has_state = state is not None
    grid = (bsz // nb, t // tm)

    def col(off):
        return pl.BlockSpec((nb, tm, D_MODEL), lambda b, i: (b, i, off // D_MODEL))

    row_spec = pl.BlockSpec((nb, tm, D_MODEL), lambda b, i: (b, i, 0))
    w_spec = pl.BlockSpec((D_MODEL, D_MODEL), lambda b, i: (0, 0), pipeline_mode=pl.Buffered(1))
    in_specs = [col(OFF_BA), col(OFF_CA), col(OFF_VA), col(OFF_MA), col(OFF_MB), row_spec, row_spec,
                pl.BlockSpec((CONV_W, D_MODEL), lambda b, i: (0, 0))]
    args = [proj, proj, proj, proj, proj, og, x, conv_w]
    if has_state:
        in_specs.append(pl.BlockSpec((nb, CONV_W - 1, D_MODEL), lambda b, i: (b, 0, 0)))
        args.append(state)
    in_specs += [w_spec, w_spec, w_spec]
    args += [w_a, w_b, w_o]
    return pl.pallas_call(
        functools.partial(_mix_kernel, nb=nb, tm=tm, has_state=has_state),
        grid=grid,
        in_specs=in_specs,
        out_specs=[
            row_spec,
            pl.BlockSpec((nb, CONV_W - 1, D_MODEL), lambda b, i: (b, 0, 0)),
        ],
        out_shape=[
            jax.ShapeDtypeStruct((bsz, t, D_MODEL), F32),
            jax.ShapeDtypeStruct((bsz, CONV_W - 1, D_MODEL), F32),
        ],
        scratch_shapes=[pltpu.VMEM((nb, 8, D_MODEL), F32)],
        compiler_params=pltpu.CompilerParams(
            dimension_semantics=("parallel", "arbitrary"),
            vmem_limit_bytes=VMEM_LIMIT),
        name="mix",
    )(*args)


def _ffn_up_kernel(*refs, nb, tm, tn, n_i, n_j, n_tiles, has_state):
    if has_state:
        (x1_ref, g_ref, wv_ref, wg_ref, cwv_ref, cwg_ref, bv_ref, bg_ref, stv_ref, stg_ref,
         h_ref, nfv_ref, nfg_ref, hn_ref, ua_ref, ub_ref, c_ref) = refs
    else:
        (x1_ref, g_ref, wv_ref, wg_ref, cwv_ref, cwg_ref, bv_ref, bg_ref,
         h_ref, nfv_ref, nfg_ref, hn_ref, ua_ref, ub_ref, c_ref) = refs
    s = pl.program_id(0)
    rows = nb * tm
    j = lax.rem(jnp.minimum(s, n_tiles - 1), n_j)
    sp = jnp.maximum(s - 1, 0)
    jp = lax.rem(sp, n_j)
    first = lax.rem(lax.div(sp, n_j), n_i) == 0

    @pl.when(s == 0)
    def _():
        ub_ref[...] = jnp.zeros_like(ub_ref)
        c_ref[...] = jnp.zeros_like(c_ref)

    @pl.when(jnp.logical_and(j == 0, s < n_tiles))
    def _():
        x1 = x1_ref[...].reshape(rows, D_MODEL)
        hn_ref[...] = (x1 * _rms_scale(x1) * g_ref[...]).astype(BF16)

    def step(u_prev, u_next):
        cw = jnp.concatenate([cwv_ref[...], cwg_ref[...]], axis=1)
        bias = jnp.concatenate([bv_ref[...], bg_ref[...]], axis=1)
        for seg in range(nb):
            base = seg * (tm + HALO) + HALO
            if has_state:
                init = jnp.concatenate([stv_ref[seg], stg_ref[seg]], axis=1)
            else:
                init = jnp.zeros((CONV_W - 1, 2 * tn), F32)
            u_prev[base - 2:base, :] = jnp.where(first, init, c_ref[jp, seg, 0:2, :])

        def strip(seg, r0):
            lo = seg * (tm + HALO) + HALO + r0
            cur = u_prev[lo:lo + STRIP, :]
            sh1 = u_prev[lo - 1:lo - 1 + STRIP, :]
            sh2 = u_prev[lo - 2:lo - 2 + STRIP, :]
            c = cw[0:1] * sh2 + cw[1:2] * sh1 + cw[2:3] * cur + bias
            gate = c[:, tn:]
            h_ref[seg, r0:r0 + STRIP, :] = (gate * _sigmoid(gate) * c[:, :tn]).astype(BF16)

        strips = [(seg, r0) for seg in range(nb) for r0 in range(0, tm, STRIP)]
        seg_groups = [list(range(nb))] if nb > 1 else None
        row_parts = [(0, tm)] if nb > 1 else [(r, tm // M_SPLIT) for r in range(0, tm, tm // M_SPLIT)]
        pieces = [(w_ref, c0, half * tn + c0, r0, rn)
                  for half, w_ref in enumerate((wv_ref, wg_ref))
                  for c0 in range(0, tn, MXU_N) for r0, rn in row_parts]
        per_piece = -(-len(strips) // len(pieces))
        for n, (w_ref, c0, dst, r0, rn) in enumerate(pieces):
            if nb > 1:
                u = _dot(hn_ref[...], w_ref[:, c0:c0 + MXU_N])
                for seg in range(nb):
                    base = seg * (tm + HALO) + HALO
                    u_next[base:base + tm, dst:dst + MXU_N] = u[seg * tm:(seg + 1) * tm]
            else:
                u_next[HALO + r0:HALO + r0 + rn, dst:dst + MXU_N] = _dot(
                    hn_ref[r0:r0 + rn, :], w_ref[:, c0:c0 + MXU_N])
            for seg, s0 in strips[n * per_piece:(n + 1) * per_piece]:
                strip(seg, s0)

        for seg in range(nb):
            base = seg * (tm + HALO) + HALO
            tail = u_prev[base + tm - 2:base + tm, :]
            c_ref[jp, seg, 0:2, :] = tail
            nfv_ref[seg] = tail[:, :tn]
            nfg_ref[seg] = tail[:, tn:]

    parity = lax.rem(s, 2)

    @pl.when(parity == 0)
    def _():
        step(ub_ref, ua_ref)

    @pl.when(parity == 1)
    def _():
        step(ua_ref, ub_ref)


def _ffn_up(x1, g, w_up, conv_w, conv_b, state, *, nb, tm, tn):
    bsz, t, _ = x1.shape
    has_state = state is not None
    n_b, n_i, n_j = bsz // nb, t // tm, D_FF // tn
    n_tiles = n_b * n_i * n_j

    def cur(s):
        sc = jnp.minimum(s, n_tiles - 1)
        return sc // (n_i * n_j), (sc // n_j) % n_i, sc % n_j

    def prev(s):
        return cur(jnp.maximum(s - 1, 0))

    in_specs = [
        pl.BlockSpec((nb, tm, D_MODEL), lambda s: cur(s)[:2] + (0,)),
        pl.BlockSpec((1, D_MODEL), lambda s: (0, 0)),
        pl.BlockSpec((D_MODEL, tn), lambda s: (0, cur(s)[2])),
        pl.BlockSpec((D_MODEL, tn), lambda s: (0, n_j + cur(s)[2])),
        pl.BlockSpec((CONV_W, tn), lambda s: (0, prev(s)[2])),
        pl.BlockSpec((CONV_W, tn), lambda s: (0, n_j + prev(s)[2])),
        pl.BlockSpec((1, tn), lambda s: (0, prev(s)[2])),
        pl.BlockSpec((1, tn), lambda s: (0, n_j + prev(s)[2])),
    ]
    args = [x1, g, w_up, w_up, conv_w, conv_w, conv_b, conv_b]
    if has_state:
        in_specs += [
            pl.BlockSpec((nb, CONV_W - 1, tn), lambda s: (prev(s)[0], 0, prev(s)[2])),
            pl.BlockSpec((nb, CONV_W - 1, tn), lambda s: (prev(s)[0], 0, n_j + prev(s)[2])),
        ]
        args += [state, state]
    tail_spec = pl.BlockSpec((nb, CONV_W - 1, tn), lambda s: (prev(s)[0], 0, prev(s)[2]))
    return pl.pallas_call(
        functools.partial(_ffn_up_kernel, nb=nb, tm=tm, tn=tn, n_i=n_i, n_j=n_j, n_tiles=n_tiles,
                          has_state=has_state),
        grid=(n_tiles + 1,),
        in_specs=in_specs,
        out_specs=[
            pl.BlockSpec((nb, tm, tn), lambda s: prev(s)),
            tail_spec,
            tail_spec,
        ],
        out_shape=[
            jax.ShapeDtypeStruct((bsz, t, D_FF), BF16),
            jax.ShapeDtypeStruct((bsz, CONV_W - 1, D_FF), F32),
            jax.ShapeDtypeStruct((bsz, CONV_W - 1, D_FF), F32),
        ],
        scratch_shapes=[
            pltpu.VMEM((nb * tm, D_MODEL), BF16),
            pltpu.VMEM((nb * (tm + HALO), 2 * tn), F32),
            pltpu.VMEM((nb * (tm + HALO), 2 * tn), F32),
            pltpu.VMEM((n_j, nb, HALO, 2 * tn), F32),
        ],
        compiler_params=pltpu.CompilerParams(
            dimension_semantics=("arbitrary",),
            vmem_limit_bytes=VMEM_LIMIT),
        name="ffn_up",
    )(*args)


def _ffn_down_kernel(h_ref, x1_ref, p_ref, wd_ref, gp_ref, wpg_ref, wple_ref, gf_ref, y_ref):
    x2 = x1_ref[...] + _dot(h_ref[...], wd_ref[...])
    pn = (x2 * _rms_scale(x2) * gp_ref[...]).astype(BF16)
    gate = _sigmoid(_dot(pn, wpg_ref[...]))
    pe = _dot(p_ref[...].astype(BF16), wple_ref[...])
    x3 = x2 + gate * pe
    y_ref[...] = x3 * _rms_scale(x3) * gf_ref[...]


def _ffn_down(h, x1, p, w_down, g_ple, w_pg, w_ple, g_final, *, tm):
    m = x1.shape[0]
    const = dict(pipeline_mode=pl.Buffered(1))
    return pl.pallas_call(
        _ffn_down_kernel,
        grid=(m // tm,),
        in_specs=[
            pl.BlockSpec((tm, D_FF), lambda i: (i, 0)),
            pl.BlockSpec((tm, D_MODEL), lambda i: (i, 0)),
            pl.BlockSpec((tm, D_PLE), lambda i: (i, 0)),
            pl.BlockSpec((D_FF, D_MODEL), lambda i: (0, 0), **const),
            pl.BlockSpec((1, D_MODEL), lambda i: (0, 0)),
            pl.BlockSpec((D_MODEL, D_MODEL), lambda i: (0, 0), **const),
            pl.BlockSpec((D_PLE, D_MODEL), lambda i: (0, 0), **const),
            pl.BlockSpec((1, D_MODEL), lambda i: (0, 0)),
        ],
        out_specs=pl.BlockSpec((tm, D_MODEL), lambda i: (i, 0)),
        out_shape=jax.ShapeDtypeStruct((m, D_MODEL), F32),
        compiler_params=pltpu.CompilerParams(
            dimension_semantics=("parallel",),
            vmem_limit_bytes=VMEM_LIMIT),
        name="ffn_down",
    )(h, x1, p, w_down, g_ple, w_pg, w_ple, g_final)


def _tiles(bsz, t):
    if t >= 1024:
        return dict(nb=1, tm_in=1024, tm_mix=256, tm_up=512, tq=256, diag=128)
    return dict(nb=bsz, tm_in=t, tm_mix=t, tm_up=t, tq=t, diag=t)


def _layer(x, p, conv_state, gla_state, ffn_state, w, g_final):
    bsz, t, _ = x.shape
    cfg = _tiles(bsz, t)
    nb = cfg["nb"]
    proj, loga = _inproj(x, w["norm_mix"], w["w_main"], w["w_alr"], w["w_g2"], w["b_gate"],
                         nb=nb, tm=cfg["tm_in"], tn=1024)
    og, new_s = _gla(proj, loga, w["gla_norm"], gla_state, tq=cfg["tq"], diag=cfg["diag"])
    x1, new_conv = _mix(proj, og, x, w["conv_a_w"], conv_state, w["w_a_out"], w["w_b_out"], w["w_o"],
                        nb=nb, tm=cfg["tm_mix"])
    h, nf_v, nf_g = _ffn_up(x1, w["norm_ffn"], w["w_up"], w["ffn_conv_w"], w["ffn_conv_b"], ffn_state,
                            nb=nb, tm=cfg["tm_up"], tn=512)
    m = bsz * t
    y = _ffn_down(h.reshape(m, D_FF), x1.reshape(m, D_MODEL), p.reshape(m, D_PLE),
                  w["w_down"], w["norm_ple"], w["w_ple_gate"], w["w_ple"], g_final, tm=256)
    return y.reshape(bsz, t, D_MODEL), new_conv, new_s, jnp.concatenate([nf_v, nf_g], axis=-1)


def kernel(x_prompt, x_sample, p_prompt, p_sample, state_conv_a, state_gla, state_ffn_conv, norm_mix, w_in, conv_a_w, w_a_out, w_gate2, b_gate, gla_norm, w_b_out, w_o, norm_ffn, w_up, ffn_conv_w, ffn_conv_b, w_down, norm_ple, w_ple_gate, w_ple, norm_final):
    depth = w_in.shape[0]
    assert depth == 1, "the final norm is fused into the last layer; one layer supported"
    i = 0
    w_in_i = w_in[i]
    w_alr = jnp.pad(w_in_i[:, ALR_OFF:ALR_OFF + GATE_RANK], ((0, 0), (0, LANE - GATE_RANK)))
    w = dict(
        norm_mix=norm_mix[i][None],
        w_main=jnp.concatenate([w_in_i[:, :ALR_OFF], w_in_i[:, ALR_OFF + GATE_RANK:]], axis=1).astype(BF16),
        w_alr=w_alr.astype(BF16),
        w_g2=jnp.pad(w_gate2[i], ((0, LANE - GATE_RANK), (0, 0))).astype(BF16),
        b_gate=b_gate[i][None],
        gla_norm=gla_norm[i][None],
        conv_a_w=conv_a_w[i],
        w_a_out=w_a_out[i].astype(BF16),
        w_b_out=w_b_out[i].astype(BF16),
        w_o=w_o[i].astype(BF16),
        norm_ffn=norm_ffn[i][None],
        w_up=w_up[i].astype(BF16),
        ffn_conv_w=ffn_conv_w[i],
        ffn_conv_b=ffn_conv_b[i][None],
        w_down=w_down[i].astype(BF16),
        norm_ple=norm_ple[i][None],
        w_ple_gate=w_ple_gate[i].astype(BF16),
        w_ple=w_ple[i].astype(BF16),
    )
    g_final = norm_final[None]
    yp, c1, s1, f1 = _layer(x_prompt, p_prompt[i], None, None, None, w, g_final)
    ys, c2, s2, f2 = _layer(x_sample, p_sample[i], state_conv_a[i], state_gla[i], state_ffn_conv[i],
                            w, g_final)
    return (yp, ys, c1[None], s1[None], f1[None], c2[None], s2[None], f2[None])
```

```python
import functools

import jax
import jax.numpy as jnp
from jax import lax
from jax.experimental import pallas as pl
from jax.experimental.pallas import tpu as pltpu

F32 = jnp.float32
BF16 = jnp.bfloat16

D_MODEL = 2048
CONV_W = 3
GLA_HEADS = 4
GLA_DK = D_MODEL // 2
GLA_DV = D_MODEL
GLA_HK = GLA_DK // GLA_HEADS
GLA_HV = GLA_DV // GLA_HEADS
GATE_RANK = 16
GATE_NORMALIZER = 16.0
D_FF = 5632
D_PLE = 256
EPS = 1e-6

D_MAIN = 3 * D_MODEL + 2 * GLA_DK + 2 * GLA_DV + 2 * D_MODEL
OFF_BA, OFF_CA, OFF_VA = 0, D_MODEL, 2 * D_MODEL
OFF_Q = 3 * D_MODEL
OFF_K = OFF_Q + GLA_DK
OFF_V = OFF_K + GLA_DK
OFF_GO = OFF_V + GLA_DV
OFF_MA = OFF_GO + GLA_DV
OFF_MB = OFF_MA + D_MODEL
ALR_OFF = OFF_MA
LANE = 128
BF16_ROWS = 16
HALO = 8
STRIP = 16
MXU_N = 256
M_SPLIT = 4
VMEM_LIMIT = 56 * 1024 * 1024

NT_DIMS = (((1,), (1,)), ((), ()))
TN_DIMS = (((0,), (0,)), ((), ()))


def _rms_scale(x):
    return lax.rsqrt(jnp.mean(x * x, axis=-1, keepdims=True) + EPS)


def _sigmoid(x):
    return 1.0 / (1.0 + jnp.exp(-x))


def _log_sigmoid(z):
    return jnp.minimum(z, 0.0) - jnp.log(1.0 + jnp.exp(-jnp.abs(z)))


def _dot(a, b):
    return jnp.dot(a, b, preferred_element_type=F32)


def _split_bf16(x):
    hi = x.astype(BF16)
    return hi, (x - hi.astype(F32)).astype(BF16)


def _conv3(u, h0, h1, w):
    t = u.shape[0]
    rows = lax.broadcasted_iota(jnp.int32, (t, 1), 0)
    sh1 = jnp.where(rows == 0, h1, pltpu.roll(u, 1, axis=0))
    sh2 = jnp.where(rows == 0, h0, jnp.where(rows == 1, h1, pltpu.roll(u, 2, axis=0)))
    return w[0:1] * sh2 + w[1:2] * sh1 + w[2:3] * u


def _inproj_kernel(x_ref, g_ref, w_ref, walr_ref, wg2_ref, bg_ref, proj_ref, loga_ref, xn_ref,
                   *, nb, tm):
    j = pl.program_id(2)
    rows = nb * tm

    @pl.when(j == 0)
    def _():
        x = x_ref[...].reshape(rows, D_MODEL)
        xn = (x * _rms_scale(x) * g_ref[...]).astype(BF16)
        xn_ref[...] = xn
        a_lr = _dot(xn, walr_ref[...])
        z = _dot(a_lr.astype(BF16), wg2_ref[...]) + bg_ref[...]
        loga_ref[...] = (_log_sigmoid(z) * (1.0 / GATE_NORMALIZER)).reshape(nb, tm, GLA_DK)

    tn = proj_ref.shape[-1]
    proj_ref[...] = _dot(xn_ref[...], w_ref[...]).astype(BF16).reshape(nb, tm, tn)


def _inproj(x, g, w_main, w_alr, w_g2, b_gate, *, nb, tm, tn):
    bsz, t, _ = x.shape
    grid = (bsz // nb, t // tm, D_MAIN // tn)
    return pl.pallas_call(
        functools.partial(_inproj_kernel, nb=nb, tm=tm),
        grid=grid,
        in_specs=[
            pl.BlockSpec((nb, tm, D_MODEL), lambda b, i, j: (b, i, 0)),
            pl.BlockSpec((1, D_MODEL), lambda b, i, j: (0, 0)),
            pl.BlockSpec((D_MODEL, tn), lambda b, i, j: (0, j)),
            pl.BlockSpec((D_MODEL, LANE), lambda b, i, j: (0, 0)),
            pl.BlockSpec((LANE, GLA_DK), lambda b, i, j: (0, 0)),
            pl.BlockSpec((1, GLA_DK), lambda b, i, j: (0, 0)),
        ],
        out_specs=[
            pl.BlockSpec((nb, tm, tn), lambda b, i, j: (b, i, j)),
            pl.BlockSpec((nb, tm, GLA_DK), lambda b, i, j: (b, i, 0)),
        ],
        out_shape=[
            jax.ShapeDtypeStruct((bsz, t, D_MAIN), BF16),
            jax.ShapeDtypeStruct((bsz, t, GLA_DK), F32),
        ],
        scratch_shapes=[pltpu.VMEM((nb * tm, D_MODEL), BF16)],
        compiler_params=pltpu.CompilerParams(
            dimension_semantics=("parallel", "parallel", "arbitrary"),
            vmem_limit_bytes=VMEM_LIMIT),
        name="inproj",
    )(x, g, w_main, w_alr, w_g2, b_gate)


def _gla_head(q, k, v, la, s_old, *, diag):
    rows = q.shape[0]
    n_d = rows // diag
    r_i = lax.broadcasted_iota(jnp.int32, (rows, rows), 0)
    c_i = lax.broadcasted_iota(jnp.int32, (rows, rows), 1)
    tril = (r_i >= c_i).astype(BF16)
    la_hi, la_lo = _split_bf16(la)
    b = _dot(tril, la_hi) + _dot(tril, la_lo)
    b_end = b[rows - 1:rows, :]

    causal_d = (lax.broadcasted_iota(jnp.int32, (diag, diag), 0)
                >= lax.broadcasted_iota(jnp.int32, (diag, diag), 1))
    a_c, k_c, cen = [], [], []
    for d in range(n_d):
        lo = d * diag
        mid = lo + diag // 2
        c = b[mid - 1:mid, :]
        bd = b[lo:lo + diag, :]
        cen.append(c)
        a_c.append(q[lo:lo + diag, :] * jnp.exp(bd - c))
        k_c.append(k[lo:lo + diag, :] * jnp.exp(c - bd))

    def block(lo_d, hi_d):
        if hi_d - lo_d == 1:
            s = lax.dot_general(a_c[lo_d].astype(BF16), k_c[lo_d].astype(BF16), NT_DIMS,
                                preferred_element_type=F32)
            return jnp.where(causal_d, s, 0.0)
        mid_d = (lo_d + hi_d) // 2
        ref = b[mid_d * diag - 1:mid_d * diag, :]
        lhs = [a_c[d] * jnp.exp(cen[d] - ref) for d in range(mid_d, hi_d)]
        rhs = [k_c[d] * jnp.exp(ref - cen[d]) for d in range(lo_d, mid_d)]
        lhs = (lhs[0] if len(lhs) == 1 else jnp.concatenate(lhs, axis=0)).astype(BF16)
        rhs = (rhs[0] if len(rhs) == 1 else jnp.concatenate(rhs, axis=0)).astype(BF16)
        off = lax.dot_general(lhs, rhs, NT_DIMS, preferred_element_type=F32)
        top = jnp.concatenate([block(lo_d, mid_d), jnp.zeros_like(off)], axis=1)
        bot = jnp.concatenate([off, block(mid_d, hi_d)], axis=1)
        return jnp.concatenate([top, bot], axis=0)

    p = block(0, n_d).astype(BF16)
    qt = [a_c[d] * jnp.exp(cen[d]) for d in range(n_d)]
    kt = [k_c[d] * jnp.exp(b_end - cen[d]) for d in range(n_d)]
    qt = (qt[0] if n_d == 1 else jnp.concatenate(qt, axis=0)).astype(BF16)
    kt = (kt[0] if n_d == 1 else jnp.concatenate(kt, axis=0)).astype(BF16)
    o = _dot(qt, s_old.astype(BF16)) + _dot(p, v)
    e_rep = jnp.broadcast_to(b_end * (1.0 / BF16_ROWS), (BF16_ROWS, b_end.shape[1]))
    e_hi, e_lo = _split_bf16(e_rep)
    ones_col = jnp.ones((BF16_ROWS, LANE), BF16)
    dcol = (lax.dot_general(e_hi, ones_col, TN_DIMS, preferred_element_type=F32)
            + lax.dot_general(e_lo, ones_col, TN_DIMS, preferred_element_type=F32))
    decay = jnp.tile(jnp.exp(dcol), (1, v.shape[1] // LANE))
    s_new = s_old * decay + lax.dot_general(kt, v, TN_DIMS, preferred_element_type=F32)
    return o, s_new


def _gla_kernel(*refs, diag, has_state):
    if has_state:
        q_ref, k_ref, v_ref, go_ref, la_ref, gn_ref, s0_ref, o_ref, sout_ref, s_ref = refs
    else:
        q_ref, k_ref, v_ref, go_ref, la_ref, gn_ref, o_ref, sout_ref, s_ref = refs
    i = pl.program_id(1)

    @pl.when(i == 0)
    def _():
        if has_state:
            s_ref[...] = s0_ref[0]
        else:
            s_ref[...] = jnp.zeros_like(s_ref)

    for h in range(GLA_HEADS):
        ck = pl.ds(h * GLA_HK, GLA_HK)
        cv = pl.ds(h * GLA_HV, GLA_HV)
        q = q_ref[0, :, ck].astype(F32) * (GLA_HK ** -0.5)
        k = k_ref[0, :, ck].astype(F32)
        o, s_new = _gla_head(q, k, v_ref[0, :, cv], la_ref[0, :, ck], s_ref[h], diag=diag)
        s_ref[h] = s_new
        g = go_ref[0, :, cv].astype(F32)
        o_ref[0, :, cv] = (o * _rms_scale(o) * gn_ref[:, cv] * (g * _sigmoid(g))).astype(BF16)

    @pl.when(i == pl.num_programs(1) - 1)
    def _():
        sout_ref[0] = s_ref[...]


def _gla(proj, loga, gla_norm, state, *, tq, diag):
    bsz, t, _ = proj.shape
    has_state = state is not None
    grid = (bsz, t // tq)
    state_spec = pl.BlockSpec((1, GLA_HEADS, GLA_HK, GLA_HV), lambda b, i: (b, 0, 0, 0))
    in_specs = [
        pl.BlockSpec((1, tq, GLA_DK), lambda b, i: (b, i, OFF_Q // GLA_DK)),
        pl.BlockSpec((1, tq, GLA_DK), lambda b, i: (b, i, OFF_K // GLA_DK)),
        pl.BlockSpec((1, tq, GLA_DV), lambda b, i: (b, i, OFF_V // GLA_DV)),
        pl.BlockSpec((1, tq, GLA_DV), lambda b, i: (b, i, OFF_GO // GLA_DV)),
        pl.BlockSpec((1, tq, GLA_DK), lambda b, i: (b, i, 0)),
        pl.BlockSpec((1, GLA_DV), lambda b, i: (0, 0)),
    ]
    args = [proj, proj, proj, proj, loga, gla_norm]
    if has_state:
        in_specs.append(state_spec)
        args.append(state)
    return pl.pallas_call(
        functools.partial(_gla_kernel, diag=diag, has_state=has_state),
        grid=grid,
        in_specs=in_specs,
        out_specs=[
            pl.BlockSpec((1, tq, GLA_DV), lambda b, i: (b, i, 0)),
            state_spec,
        ],
        out_shape=[
            jax.ShapeDtypeStruct((bsz, t, GLA_DV), BF16),
            jax.ShapeDtypeStruct((bsz, GLA_HEADS, GLA_HK, GLA_HV), F32),
        ],
        scratch_shapes=[pltpu.VMEM((GLA_HEADS, GLA_HK, GLA_HV), F32)],
        compiler_params=pltpu.CompilerParams(
            dimension_semantics=("parallel", "arbitrary"),
            vmem_limit_bytes=VMEM_LIMIT),
        name="gla",
    )(*args)


def _mix_kernel(*refs, nb, tm, has_state):
    if has_state:
        (ba_ref, ca_ref, va_ref, ma_ref, mb_ref, og_ref, x_ref, cw_ref, st_ref,
         wa_ref, wb_ref, wo_ref, x1_ref, newc_ref, carry_ref) = refs
    else:
        (ba_ref, ca_ref, va_ref, ma_ref, mb_ref, og_ref, x_ref, cw_ref,
         wa_ref, wb_ref, wo_ref, x1_ref, newc_ref, carry_ref) = refs
    i = pl.program_id(1)
    rows = nb * tm
    first = i == 0
    cw = cw_ref[...]

    y_b = _dot(og_ref[...].reshape(rows, D_MODEL), wb_ref[...])

    conv_parts = []
    for s in range(nb):
        cv = ca_ref[s].astype(F32) * va_ref[s].astype(F32)
        init = st_ref[s] if has_state else jnp.zeros((CONV_W - 1, D_MODEL), F32)
        halo = jnp.where(first, init, carry_ref[s, 0:2, :])
        conv_parts.append(_conv3(cv, halo[0:1], halo[1:2], cw))
        tail = cv[tm - 2:tm, :]
        carry_ref[s, 0:2, :] = tail
        newc_ref[s] = tail
    conv = conv_parts[0] if nb == 1 else jnp.concatenate(conv_parts, axis=0)

    b_a = ba_ref[...].reshape(rows, D_MODEL).astype(F32)
    y_a = _dot((b_a * conv).astype(BF16), wa_ref[...])
    m_a = ma_ref[...].reshape(rows, D_MODEL).astype(F32)
    m_b = mb_ref[...].reshape(rows, D_MODEL).astype(F32)
    z = _sigmoid(m_a) * y_a + _sigmoid(m_b) * y_b
    x1 = x_ref[...].reshape(rows, D_MODEL) + _dot(z.astype(BF16), wo_ref[...])
    x1_ref[...] = x1.reshape(nb, tm, D_MODEL)


def _mix(proj, og, x, conv_w, state, w_a, w_b, w_o, *, nb, tm):
    bsz, t, _ = x.shape
    has_state = state is not None
    grid = (bsz // nb, t // tm)

    def col(off):
        return pl.BlockSpec((nb, tm, D_MODEL), lambda b, i: (b, i, off // D_MODEL))

    row_spec = pl.BlockSpec((nb, tm, D_MODEL), lambda b, i: (b, i, 0))
    w_spec = pl.BlockSpec((D_MODEL, D_MODEL), lambda b, i: (0, 0), pipeline_mode=pl.Buffered(1))
    in_specs = [col(OFF_BA), col(OFF_CA), col(OFF_VA), col(OFF_MA), col(OFF_MB), row_spec, row_spec,
                pl.BlockSpec((CONV_W, D_MODEL), lambda b, i: (0, 0))]
    args = [proj, proj, proj, proj, proj, og, x, conv_w]
    if has_state:
        in_specs.append(pl.BlockSpec((nb, CONV_W - 1, D_MODEL), lambda b, i: (b, 0, 0)))
        args.append(state)
    in_specs += [w_spec, w_spec, w_spec]
    args += [w_a, w_b, w_o]
    return pl.pallas_call(
        functools.partial(_mix_kernel, nb=nb, tm=tm, has_state=has_state),
        grid=grid,
        in_specs=in_specs,
        out_specs=[
            row_spec,
            pl.BlockSpec((nb, CONV_W - 1, D_MODEL), lambda b, i: (b, 0, 0)),
        ],
        out_shape=[
            jax.ShapeDtypeStruct((bsz, t, D_MODEL), F32),
            jax.ShapeDtypeStruct((bsz, CONV_W - 1, D_MODEL), F32),
        ],
        scratch_shapes=[pltpu.VMEM((nb, 8, D_MODEL), F32)],
        compiler_params=pltpu.CompilerParams(
            dimension_semantics=("parallel", "arbitrary"),
            vmem_limit_bytes=VMEM_LIMIT),
        name="mix",
    )(*args)


def _ffn_up_kernel(*refs, nb, tm, tn, n_i, n_j, n_tiles, has_state):
    if has_state:
        (x1_ref, g_ref, wv_ref, wg_ref, cwv_ref, cwg_ref, bv_ref, bg_ref, stv_ref, stg_ref,
         h_ref, nfv_ref, nfg_ref, hn_ref, ua_ref, ub_ref, c_ref) = refs
    else:
        (x1_ref, g_ref, wv_ref, wg_ref, cwv_ref, cwg_ref, bv_ref, bg_ref,
         h_ref, nfv_ref, nfg_ref, hn_ref, ua_ref, ub_ref, c_ref) = refs
    s = pl.program_id(0)
    rows = nb * tm
    j = lax.rem(jnp.minimum(s, n_tiles - 1), n_j)
    sp = jnp.maximum(s - 1, 0)
    jp = lax.rem(sp, n_j)
    first = lax.rem(lax.div(sp, n_j), n_i) == 0

    @pl.when(s == 0)
    def _():
        ub_ref[...] = jnp.zeros_like(ub_ref)
        c_ref[...] = jnp.zeros_like(c_ref)

    @pl.when(jnp.logical_and(j == 0, s < n_tiles))
    def _():
        x1 = x1_ref[...].reshape(rows, D_MODEL)
        hn_ref[...] = (x1 * _rms_scale(x1) * g_ref[...]).astype(BF16)

    def step(u_prev, u_next):
        cw = jnp.concatenate([cwv_ref[...], cwg_ref[...]], axis=1)
        bias = jnp.concatenate([bv_ref[...], bg_ref[...]], axis=1)
        for seg in range(nb):
            base = seg * (tm + HALO) + HALO
            if has_state:
                init = jnp.concatenate([stv_ref[seg], stg_ref[seg]], axis=1)
            else:
                init = jnp.zeros((CONV_W - 1, 2 * tn), F32)
            u_prev[base - 2:base, :] = jnp.where(first, init, c_ref[jp, seg, 0:2, :])

        def strip(seg, r0):
            lo = seg * (tm + HALO) + HALO + r0
            win = u_prev[lo - HALO:lo + STRIP, :]
            cur = win[HALO:]
            sh1 = pltpu.roll(win, 1, axis=0)[HALO:]
            sh2 = pltpu.roll(win, 2, axis=0)[HALO:]
            c = cw[0:1] * sh2 + cw[1:2] * sh1 + cw[2:3] * cur + bias
            gate = c[:, tn:]
            h_ref[seg, r0:r0 + STRIP, :] = (gate * _sigmoid(gate) * c[:, :tn]).astype(BF16)

        strips = [(seg, r0) for seg in range(nb) for r0 in range(0, tm, STRIP)]
        seg_groups = [list(range(nb))] if nb > 1 else None
        row_parts = [(0, tm)] if nb > 1 else [(r, tm // M_SPLIT) for r in range(0, tm, tm // M_SPLIT)]
        pieces = [(w_ref, c0, half * tn + c0, r0, rn)
                  for half, w_ref in enumerate((wv_ref, wg_ref))
                  for c0 in range(0, tn, MXU_N) for r0, rn in row_parts]
        per_piece = -(-len(strips) // len(pieces))
        for n, (w_ref, c0, dst, r0, rn) in enumerate(pieces):
            if nb > 1:
                u = _dot(hn_ref[...], w_ref[:, c0:c0 + MXU_N])
                for seg in range(nb):
                    base = seg * (tm + HALO) + HALO
                    u_next[base:base + tm, dst:dst + MXU_N] = u[seg * tm:(seg + 1) * tm]
            else:
                u_next[HALO + r0:HALO + r0 + rn, dst:dst + MXU_N] = _dot(
                    hn_ref[r0:r0 + rn, :], w_ref[:, c0:c0 + MXU_N])
            for seg, s0 in strips[n * per_piece:(n + 1) * per_piece]:
                strip(seg, s0)

        for seg in range(nb):
            base = seg * (tm + HALO) + HALO
            tail = u_prev[base + tm - 2:base + tm, :]
            c_ref[jp, seg, 0:2, :] = tail
            nfv_ref[seg] = tail[:, :tn]
            nfg_ref[seg] = tail[:, tn:]

    parity = lax.rem(s, 2)

    @pl.when(parity == 0)
    def _():
        step(ub_ref, ua_ref)

    @pl.when(parity == 1)
    def _():
        step(ua_ref, ub_ref)


def _ffn_up(x1, g, w_up, conv_w, conv_b, state, *, nb, tm, tn):
    bsz, t, _ = x1.shape
    has_state = state is not None
    n_b, n_i, n_j = bsz // nb, t // tm, D_FF // tn
    n_tiles = n_b * n_i * n_j

    def cur(s):
        sc = jnp.minimum(s, n_tiles - 1)
        return sc // (n_i * n_j), (sc // n_j) % n_i, sc % n_j

    def prev(s):
        return cur(jnp.maximum(s - 1, 0))

    in_specs = [
        pl.BlockSpec((nb, tm, D_MODEL), lambda s: cur(s)[:2] + (0,)),
        pl.BlockSpec((1, D_MODEL), lambda s: (0, 0)),
        pl.BlockSpec((D_MODEL, tn), lambda s: (0, cur(s)[2])),
        pl.BlockSpec((D_MODEL, tn), lambda s: (0, n_j + cur(s)[2])),
        pl.BlockSpec((CONV_W, tn), lambda s: (0, prev(s)[2])),
        pl.BlockSpec((CONV_W, tn), lambda s: (0, n_j + prev(s)[2])),
        pl.BlockSpec((1, tn), lambda s: (0, prev(s)[2])),
        pl.BlockSpec((1, tn), lambda s: (0, n_j + prev(s)[2])),
    ]
    args = [x1, g, w_up, w_up, conv_w, conv_w, conv_b, conv_b]
    if has_state:
        in_specs += [
            pl.BlockSpec((nb, CONV_W - 1, tn), lambda s: (prev(s)[0], 0, prev(s)[2])),
            pl.BlockSpec((nb, CONV_W - 1, tn), lambda s: (prev(s)[0], 0, n_j + prev(s)[2])),
        ]
        args += [state, state]
    tail_spec = pl.BlockSpec((nb, CONV_W - 1, tn), lambda s: (prev(s)[0], 0, prev(s)[2]))
    return pl.pallas_call(
        functools.partial(_ffn_up_kernel, nb=nb, tm=tm, tn=tn, n_i=n_i, n_j=n_j, n_tiles=n_tiles,
                          has_state=has_state),
        grid=(n_tiles + 1,),
        in_specs=in_specs,
        out_specs=[
            pl.BlockSpec((nb, tm, tn), lambda s: prev(s)),
            tail_spec,
            tail_spec,
        ],
        out_shape=[
            jax.ShapeDtypeStruct((bsz, t, D_FF), BF16),
            jax.ShapeDtypeStruct((bsz, CONV_W - 1, D_FF), F32),
            jax.ShapeDtypeStruct((bsz, CONV_W - 1, D_FF), F32),
        ],
        scratch_shapes=[
            pltpu.VMEM((nb * tm, D_MODEL), BF16),
            pltpu.VMEM((nb * (tm + HALO), 2 * tn), F32),
            pltpu.VMEM((nb * (tm + HALO), 2 * tn), F32),
            pltpu.VMEM((n_j, nb, HALO, 2 * tn), F32),
        ],
        compiler_params=pltpu.CompilerParams(
            dimension_semantics=("arbitrary",),
            vmem_limit_bytes=VMEM_LIMIT),
        name="ffn_up",
    )(*args)


def _ffn_down_kernel(h_ref, x1_ref, p_ref, wd_ref, gp_ref, wpg_ref, wple_ref, gf_ref, y_ref):
    x2 = x1_ref[...] + _dot(h_ref[...], wd_ref[...])
    pn = (x2 * _rms_scale(x2) * gp_ref[...]).astype(BF16)
    gate = _sigmoid(_dot(pn, wpg_ref[...]))
    pe = _dot(p_ref[...].astype(BF16), wple_ref[...])
    x3 = x2 + gate * pe
    y_ref[...] = x3 * _rms_scale(x3) * gf_ref[...]


def _ffn_down(h, x1, p, w_down, g_ple, w_pg, w_ple, g_final, *, tm):
    m = x1.shape[0]
    const = dict(pipeline_mode=pl.Buffered(1))
    return pl.pallas_call(
        _ffn_down_kernel,
        grid=(m // tm,),
        in_specs=[
            pl.BlockSpec((tm, D_FF), lambda i: (i, 0)),
            pl.BlockSpec((tm, D_MODEL), lambda i: (i, 0)),
            pl.BlockSpec((tm, D_PLE), lambda i: (i, 0)),
            pl.BlockSpec((D_FF, D_MODEL), lambda i: (0, 0), **const),
            pl.BlockSpec((1, D_MODEL), lambda i: (0, 0)),
            pl.BlockSpec((D_MODEL, D_MODEL), lambda i: (0, 0), **const),
            pl.BlockSpec((D_PLE, D_MODEL), lambda i: (0, 0), **const),
            pl.BlockSpec((1, D_MODEL), lambda i: (0, 0)),
        ],
        out_specs=pl.BlockSpec((tm, D_MODEL), lambda i: (i, 0)),
        out_shape=jax.ShapeDtypeStruct((m, D_MODEL), F32),
        compiler_params=pltpu.CompilerParams(
            dimension_semantics=("parallel",),
            vmem_limit_bytes=VMEM_LIMIT),
        name="ffn_down",
    )(h, x1, p, w_down, g_ple, w_pg, w_ple, g_final)


def _tiles(bsz, t):
    if t >= 1024:
        return dict(nb=1, tm_in=1024, tm_mix=256, tm_up=512, tq=256, diag=128)
    return dict(nb=bsz, tm_in=t, tm_mix=t, tm_up=t, tq=t, diag=t)


def _layer(x, p, conv_state, gla_state, ffn_state, w, g_final):
    bsz, t, _ = x.shape
    cfg = _tiles(bsz, t)
    nb = cfg["nb"]
    proj, loga = _inproj(x, w["norm_mix"], w["w_main"], w["w_alr"], w["w_g2"], w["b_gate"],
                         nb=nb, tm=cfg["tm_in"], tn=1024)
    og, new_s = _gla(proj, loga, w["gla_norm"], gla_state, tq=cfg["tq"], diag=cfg["diag"])
    x1, new_conv = _mix(proj, og, x, w["conv_a_w"], conv_state, w["w_a_out"], w["w_b_out"], w["w_o"],
                        nb=nb, tm=cfg["tm_mix"])
    h, nf_v, nf_g = _ffn_up(x1, w["norm_ffn"], w["w_up"], w["ffn_conv_w"], w["ffn_conv_b"], ffn_state,
                            nb=nb, tm=cfg["tm_up"], tn=512)
    m = bsz * t
    y = _ffn_down(h.reshape(m, D_FF), x1.reshape(m, D_MODEL), p.reshape(m, D_PLE),
                  w["w_down"], w["norm_ple"], w["w_ple_gate"], w["w_ple"], g_final, tm=256)
    return y.reshape(bsz, t, D_MODEL), new_conv, new_s, jnp.concatenate([nf_v, nf_g], axis=-1)


def kernel(x_prompt, x_sample, p_prompt, p_sample, state_conv_a, state_gla, state_ffn_conv, norm_mix, w_in, conv_a_w, w_a_out, w_gate2, b_gate, gla_norm, w_b_out, w_o, norm_ffn, w_up, ffn_conv_w, ffn_conv_b, w_down, norm_ple, w_ple_gate, w_ple, norm_final):
    depth = w_in.shape[0]
    assert depth == 1, "the final norm is fused into the last layer; one layer supported"
    i = 0
    w_in_i = w_in[i]
    w_alr = jnp.pad(w_in_i[:, ALR_OFF:ALR_OFF + GATE_RANK], ((0, 0), (0, LANE - GATE_RANK)))
    w = dict(
        norm_mix=norm_mix[i][None],
        w_main=jnp.concatenate([w_in_i[:, :ALR_OFF].astype(BF16),
                                w_in_i[:, ALR_OFF + GATE_RANK:].astype(BF16)], axis=1),
        w_alr=w_alr.astype(BF16),
        w_g2=jnp.pad(w_gate2[i], ((0, LANE - GATE_RANK), (0, 0))).astype(BF16),
        b_gate=b_gate[i][None],
        gla_norm=gla_norm[i][None],
        conv_a_w=conv_a_w[i],
        w_a_out=w_a_out[i].astype(BF16),
        w_b_out=w_b_out[i].astype(BF16),
        w_o=w_o[i].astype(BF16),
        norm_ffn=norm_ffn[i][None],
        w_up=w_up[i].astype(BF16),
        ffn_conv_w=ffn_conv_w[i],
        ffn_conv_b=ffn_conv_b[i][None],
        w_down=w_down[i].astype(BF16),
        norm_ple=norm_ple[i][None],
        w_ple_gate=w_ple_gate[i].astype(BF16),
        w_ple=w_ple[i].astype(BF16),
    )
    g_final = norm_final[None]
    yp, c1, s1, f1 = _layer(x_prompt, p_prompt[i], None, None, None, w, g_final)
    ys, c2, s2, f2 = _layer(x_sample, p_sample[i], state_conv_a[i], state_gla[i], state_ffn_conv[i],
                            w, g_final)
    return (yp, ys, c1[None], s1[None], f1[None], c2[None], s2[None], f2[None])
```

```python
import functools

import jax
import jax.numpy as jnp
from jax import lax
from jax.experimental import pallas as pl
from jax.experimental.pallas import tpu as pltpu

F32 = jnp.float32
BF16 = jnp.bfloat16

D_MODEL = 2048
CONV_W = 3
GLA_HEADS = 4
GLA_DK = D_MODEL // 2
GLA_DV = D_MODEL
GLA_HK = GLA_DK // GLA_HEADS
GLA_HV = GLA_DV // GLA_HEADS
GATE_RANK = 16
GATE_NORMALIZER = 16.0
D_FF = 5632
D_PLE = 256
EPS = 1e-6

D_MAIN = 3 * D_MODEL + 2 * GLA_DK + 2 * GLA_DV + 2 * D_MODEL
OFF_BA, OFF_CA, OFF_VA = 0, D_MODEL, 2 * D_MODEL
OFF_Q = 3 * D_MODEL
OFF_K = OFF_Q + GLA_DK
OFF_V = OFF_K + GLA_DK
OFF_GO = OFF_V + GLA_DV
OFF_MA = OFF_GO + GLA_DV
OFF_MB = OFF_MA + D_MODEL
ALR_OFF = OFF_MA
LANE = 128
BF16_ROWS = 16
HALO = 8
STRIP = 16
MXU_N = 256
M_SPLIT = 2
VMEM_LIMIT = 56 * 1024 * 1024

NT_DIMS = (((1,), (1,)), ((), ()))
TN_DIMS = (((0,), (0,)), ((), ()))


def _rms_scale(x):
    return lax.rsqrt(jnp.mean(x * x, axis=-1, keepdims=True) + EPS)


def _sigmoid(x):
    return 1.0 / (1.0 + jnp.exp(-x))


def _log_sigmoid(z):
    return jnp.minimum(z, 0.0) - jnp.log(1.0 + jnp.exp(-jnp.abs(z)))


def _dot(a, b):
    return jnp.dot(a, b, preferred_element_type=F32)


def _split_bf16(x):
    hi = x.astype(BF16)
    return hi, (x - hi.astype(F32)).astype(BF16)


def _conv3(u, h0, h1, w):
    t = u.shape[0]
    rows = lax.broadcasted_iota(jnp.int32, (t, 1), 0)
    sh1 = jnp.where(rows == 0, h1, pltpu.roll(u, 1, axis=0))
    sh2 = jnp.where(rows == 0, h0, jnp.where(rows == 1, h1, pltpu.roll(u, 2, axis=0)))
    return w[0:1] * sh2 + w[1:2] * sh1 + w[2:3] * u


def _inproj_kernel(x_ref, g_ref, w_ref, walr_ref, wg2_ref, bg_ref, proj_ref, loga_ref, xn_ref,
                   *, nb, tm):
    j = pl.program_id(2)
    rows = nb * tm

    @pl.when(j == 0)
    def _():
        x = x_ref[...].reshape(rows, D_MODEL)
        xn = (x * _rms_scale(x) * g_ref[...]).astype(BF16)
        xn_ref[...] = xn
        a_lr = _dot(xn, walr_ref[...])
        z = _dot(a_lr.astype(BF16), wg2_ref[...]) + bg_ref[...]
        loga_ref[...] = (_log_sigmoid(z) * (1.0 / GATE_NORMALIZER)).reshape(nb, tm, GLA_DK)

    tn = proj_ref.shape[-1]
    proj_ref[...] = _dot(xn_ref[...], w_ref[...]).astype(BF16).reshape(nb, tm, tn)


def _inproj(x, g, w_main, w_alr, w_g2, b_gate, *, nb, tm, tn):
    bsz, t, _ = x.shape
    grid = (bsz // nb, t // tm, D_MAIN // tn)
    return pl.pallas_call(
        functools.partial(_inproj_kernel, nb=nb, tm=tm),
        grid=grid,
        in_specs=[
            pl.BlockSpec((nb, tm, D_MODEL), lambda b, i, j: (b, i, 0)),
            pl.BlockSpec((1, D_MODEL), lambda b, i, j: (0, 0)),
            pl.BlockSpec((D_MODEL, tn), lambda b, i, j: (0, j)),
            pl.BlockSpec((D_MODEL, LANE), lambda b, i, j: (0, 0)),
            pl.BlockSpec((LANE, GLA_DK), lambda b, i, j: (0, 0)),
            pl.BlockSpec((1, GLA_DK), lambda b, i, j: (0, 0)),
        ],
        out_specs=[
            pl.BlockSpec((nb, tm, tn), lambda b, i, j: (b, i, j)),
            pl.BlockSpec((nb, tm, GLA_DK), lambda b, i, j: (b, i, 0)),
        ],
        out_shape=[
            jax.ShapeDtypeStruct((bsz, t, D_MAIN), BF16),
            jax.ShapeDtypeStruct((bsz, t, GLA_DK), F32),
        ],
        scratch_shapes=[pltpu.VMEM((nb * tm, D_MODEL), BF16)],
        compiler_params=pltpu.CompilerParams(
            dimension_semantics=("parallel", "parallel", "arbitrary"),
            vmem_limit_bytes=VMEM_LIMIT),
        name="inproj",
    )(x, g, w_main, w_alr, w_g2, b_gate)


def _gla_head(q, k, v, la, s_old, *, diag):
    rows = q.shape[0]
    n_d = rows // diag
    r_i = lax.broadcasted_iota(jnp.int32, (rows, rows), 0)
    c_i = lax.broadcasted_iota(jnp.int32, (rows, rows), 1)
    tril = (r_i >= c_i).astype(BF16)
    la_hi, la_lo = _split_bf16(la)
    b = _dot(tril, la_hi) + _dot(tril, la_lo)
    b_end = b[rows - 1:rows, :]

    causal_d = (lax.broadcasted_iota(jnp.int32, (diag, diag), 0)
                >= lax.broadcasted_iota(jnp.int32, (diag, diag), 1))
    a_c, k_c, cen = [], [], []
    for d in range(n_d):
        lo = d * diag
        mid = lo + diag // 2
        c = b[mid - 1:mid, :]
        bd = b[lo:lo + diag, :]
        cen.append(c)
        a_c.append(q[lo:lo + diag, :] * jnp.exp(bd - c))
        k_c.append(k[lo:lo + diag, :] * jnp.exp(c - bd))

    def block(lo_d, hi_d):
        if hi_d - lo_d == 1:
            s = lax.dot_general(a_c[lo_d].astype(BF16), k_c[lo_d].astype(BF16), NT_DIMS,
                                preferred_element_type=F32)
            return jnp.where(causal_d, s, 0.0)
        mid_d = (lo_d + hi_d) // 2
        ref = b[mid_d * diag - 1:mid_d * diag, :]
        lhs = [a_c[d] * jnp.exp(cen[d] - ref) for d in range(mid_d, hi_d)]
        rhs = [k_c[d] * jnp.exp(ref - cen[d]) for d in range(lo_d, mid_d)]
        lhs = (lhs[0] if len(lhs) == 1 else jnp.concatenate(lhs, axis=0)).astype(BF16)
        rhs = (rhs[0] if len(rhs) == 1 else jnp.concatenate(rhs, axis=0)).astype(BF16)
        off = lax.dot_general(lhs, rhs, NT_DIMS, preferred_element_type=F32)
        top = jnp.concatenate([block(lo_d, mid_d), jnp.zeros_like(off)], axis=1)
        bot = jnp.concatenate([off, block(mid_d, hi_d)], axis=1)
        return jnp.concatenate([top, bot], axis=0)

    p = block(0, n_d).astype(BF16)
    qt = [a_c[d] * jnp.exp(cen[d]) for d in range(n_d)]
    kt = [k_c[d] * jnp.exp(b_end - cen[d]) for d in range(n_d)]
    qt = (qt[0] if n_d == 1 else jnp.concatenate(qt, axis=0)).astype(BF16)
    kt = (kt[0] if n_d == 1 else jnp.concatenate(kt, axis=0)).astype(BF16)
    o = _dot(qt, s_old.astype(BF16)) + _dot(p, v)
    e_rep = jnp.broadcast_to(b_end * (1.0 / BF16_ROWS), (BF16_ROWS, b_end.shape[1]))
    e_hi, e_lo = _split_bf16(e_rep)
    ones_col = jnp.ones((BF16_ROWS, LANE), BF16)
    dcol = (lax.dot_general(e_hi, ones_col, TN_DIMS, preferred_element_type=F32)
            + lax.dot_general(e_lo, ones_col, TN_DIMS, preferred_element_type=F32))
    decay = jnp.tile(jnp.exp(dcol), (1, v.shape[1] // LANE))
    s_new = s_old * decay + lax.dot_general(kt, v, TN_DIMS, preferred_element_type=F32)
    return o, s_new


def _gla_kernel(*refs, diag, has_state):
    if has_state:
        q_ref, k_ref, v_ref, go_ref, la_ref, gn_ref, s0_ref, o_ref, sout_ref, s_ref = refs
    else:
        q_ref, k_ref, v_ref, go_ref, la_ref, gn_ref, o_ref, sout_ref, s_ref = refs
    i = pl.program_id(1)

    @pl.when(i == 0)
    def _():
        if has_state:
            s_ref[...] = s0_ref[0]
        else:
            s_ref[...] = jnp.zeros_like(s_ref)

    for h in range(GLA_HEADS):
        ck = pl.ds(h * GLA_HK, GLA_HK)
        cv = pl.ds(h * GLA_HV, GLA_HV)
        q = q_ref[0, :, ck].astype(F32) * (GLA_HK ** -0.5)
        k = k_ref[0, :, ck].astype(F32)
        o, s_new = _gla_head(q, k, v_ref[0, :, cv], la_ref[0, :, ck], s_ref[h], diag=diag)
        s_ref[h] = s_new
        g = go_ref[0, :, cv].astype(F32)
        o_ref[0, :, cv] = (o * _rms_scale(o) * gn_ref[:, cv] * (g * _sigmoid(g))).astype(BF16)

    @pl.when(i == pl.num_programs(1) - 1)
    def _():
        sout_ref[0] = s_ref[...]


def _gla(proj, loga, gla_norm, state, *, tq, diag):
    bsz, t, _ = proj.shape
    has_state = state is not None
    grid = (bsz, t // tq)
    state_spec = pl.BlockSpec((1, GLA_HEADS, GLA_HK, GLA_HV), lambda b, i: (b, 0, 0, 0))
    in_specs = [
        pl.BlockSpec((1, tq, GLA_DK), lambda b, i: (b, i, OFF_Q // GLA_DK)),
        pl.BlockSpec((1, tq, GLA_DK), lambda b, i: (b, i, OFF_K // GLA_DK)),
        pl.BlockSpec((1, tq, GLA_DV), lambda b, i: (b, i, OFF_V // GLA_DV)),
        pl.BlockSpec((1, tq, GLA_DV), lambda b, i: (b, i, OFF_GO // GLA_DV)),
        pl.BlockSpec((1, tq, GLA_DK), lambda b, i: (b, i, 0)),
        pl.BlockSpec((1, GLA_DV), lambda b, i: (0, 0)),
    ]
    args = [proj, proj, proj, proj, loga, gla_norm]
    if has_state:
        in_specs.append(state_spec)
        args.append(state)
    return pl.pallas_call(
        functools.partial(_gla_kernel, diag=diag, has_state=has_state),
        grid=grid,
        in_specs=in_specs,
        out_specs=[
            pl.BlockSpec((1, tq, GLA_DV), lambda b, i: (b, i, 0)),
            state_spec,
        ],
        out_shape=[
            jax.ShapeDtypeStruct((bsz, t, GLA_DV), BF16),
            jax.ShapeDtypeStruct((bsz, GLA_HEADS, GLA_HK, GLA_HV), F32),
        ],
        scratch_shapes=[pltpu.VMEM((GLA_HEADS, GLA_HK, GLA_HV), F32)],
        compiler_params=pltpu.CompilerParams(
            dimension_semantics=("parallel", "arbitrary"),
            vmem_limit_bytes=VMEM_LIMIT),
        name="gla",
    )(*args)


def _mix_kernel(*refs, nb, tm, has_state):
    if has_state:
        (ba_ref, ca_ref, va_ref, ma_ref, mb_ref, og_ref, x_ref, cw_ref, st_ref,
         wa_ref, wb_ref, wo_ref, x1_ref, newc_ref, carry_ref) = refs
    else:
        (ba_ref, ca_ref, va_ref, ma_ref, mb_ref, og_ref, x_ref, cw_ref,
         wa_ref, wb_ref, wo_ref, x1_ref, newc_ref, carry_ref) = refs
    i = pl.program_id(1)
    rows = nb * tm
    first = i == 0
    cw = cw_ref[...]

    y_b = _dot(og_ref[...].reshape(rows, D_MODEL), wb_ref[...])

    conv_parts = []
    for s in range(nb):
        cv = ca_ref[s].astype(F32) * va_ref[s].astype(F32)
        init = st_ref[s] if has_state else jnp.zeros((CONV_W - 1, D_MODEL), F32)
        halo = jnp.where(first, init, carry_ref[s, 0:2, :])
        conv_parts.append(_conv3(cv, halo[0:1], halo[1:2], cw))
        tail = cv[tm - 2:tm, :]
        carry_ref[s, 0:2, :] = tail
        newc_ref[s] = tail
    conv = conv_parts[0] if nb == 1 else jnp.concatenate(conv_parts, axis=0)

    b_a = ba_ref[...].reshape(rows, D_MODEL).astype(F32)
    y_a = _dot((b_a * conv).astype(BF16), wa_ref[...])
    m_a = ma_ref[...].reshape(rows, D_MODEL).astype(F32)
    m_b = mb_ref[...].reshape(rows, D_MODEL).astype(F32)
    z = _sigmoid(m_a) * y_a + _sigmoid(m_b) * y_b
    x1 = x_ref[...].reshape(rows, D_MODEL) + _dot(z.astype(BF16), wo_ref[...])
    x1_ref[...] = x1.reshape(nb, tm, D_MODEL)


def _mix(proj, og, x, conv_w, state, w_a, w_b, w_o, *, nb, tm):
    bsz, t, _ = x.shape
    has_state = state is not None
    grid = (bsz // nb, t // tm)

    def col(off):
        return pl.BlockSpec((nb, tm, D_MODEL), lambda b, i: (b, i, off // D_MODEL))

    row_spec = pl.BlockSpec((nb, tm, D_MODEL), lambda b, i: (b, i, 0))
    w_spec = pl.BlockSpec((D_MODEL, D_MODEL), lambda b, i: (0, 0), pipeline_mode=pl.Buffered(1))
    in_specs = [col(OFF_BA), col(OFF_CA), col(OFF_VA), col(OFF_MA), col(OFF_MB), row_spec, row_spec,
                pl.BlockSpec((CONV_W, D_MODEL), lambda b, i: (0, 0))]
    args = [proj, proj, proj, proj, proj, og, x, conv_w]
    if has_state:
        in_specs.append(pl.BlockSpec((nb, CONV_W - 1, D_MODEL), lambda b, i: (b, 0, 0)))
        args.append(state)
    in_specs += [w_spec, w_spec, w_spec]
    args += [w_a, w_b, w_o]
    return pl.pallas_call(
        functools.partial(_mix_kernel, nb=nb, tm=tm, has_state=has_state),
        grid=grid,
        in_specs=in_specs,
        out_specs=[
            row_spec,
            pl.BlockSpec((nb, CONV_W - 1, D_MODEL), lambda b, i: (b, 0, 0)),
        ],
        out_shape=[
            jax.ShapeDtypeStruct((bsz, t, D_MODEL), F32),
            jax.ShapeDtypeStruct((bsz, CONV_W - 1, D_MODEL), F32),
        ],
        scratch_shapes=[pltpu.VMEM((nb, 8, D_MODEL), F32)],
        compiler_params=pltpu.CompilerParams(
            dimension_semantics=("parallel", "arbitrary"),
            vmem_limit_bytes=VMEM_LIMIT),
        name="mix",
    )(*args)


def _ffn_up_kernel(*refs, nb, tm, tn, n_i, n_j, n_tiles, has_state):
    if has_state:
        (x1_ref, g_ref, wv_ref, wg_ref, cwv_ref, cwg_ref, bv_ref, bg_ref, stv_ref, stg_ref,
         h_ref, nfv_ref, nfg_ref, hn_ref, ua_ref, ub_ref, c_ref) = refs
    else:
        (x1_ref, g_ref, wv_ref, wg_ref, cwv_ref, cwg_ref, bv_ref, bg_ref,
         h_ref, nfv_ref, nfg_ref, hn_ref, ua_ref, ub_ref, c_ref) = refs
    s = pl.program_id(0)
    rows = nb * tm
    j = lax.rem(jnp.minimum(s, n_tiles - 1), n_j)
    sp = jnp.maximum(s - 1, 0)
    jp = lax.rem(sp, n_j)
    first = lax.rem(lax.div(sp, n_j), n_i) == 0

    @pl.when(s == 0)
    def _():
        ub_ref[...] = jnp.zeros_like(ub_ref)
        c_ref[...] = jnp.zeros_like(c_ref)

    @pl.when(jnp.logical_and(j == 0, s < n_tiles))
    def _():
        x1 = x1_ref[...].reshape(rows, D_MODEL)
        hn_ref[...] = (x1 * _rms_scale(x1) * g_ref[...]).astype(BF16)

    def step(u_prev, u_next):
        cw = jnp.concatenate([cwv_ref[...], cwg_ref[...]], axis=1)
        bias = jnp.concatenate([bv_ref[...], bg_ref[...]], axis=1)
        for seg in range(nb):
            base = seg * (tm + HALO) + HALO
            if has_state:
                init = jnp.concatenate([stv_ref[seg], stg_ref[seg]], axis=1)
            else:
                init = jnp.zeros((CONV_W - 1, 2 * tn), F32)
            u_prev[base - 2:base, :] = jnp.where(first, init, c_ref[jp, seg, 0:2, :])

        def strip(seg, r0):
            lo = seg * (tm + HALO) + HALO + r0
            win = u_prev[lo - HALO:lo + STRIP, :]
            cur = win[HALO:]
            sh1 = pltpu.roll(win, 1, axis=0)[HALO:]
            sh2 = pltpu.roll(win, 2, axis=0)[HALO:]
            c = cw[0:1] * sh2 + cw[1:2] * sh1 + cw[2:3] * cur + bias
            gate = c[:, tn:]
            h_ref[seg, r0:r0 + STRIP, :] = (gate * _sigmoid(gate) * c[:, :tn]).astype(BF16)

        strips = [(seg, r0) for seg in range(nb) for r0 in range(0, tm, STRIP)]
        seg_groups = [list(range(nb))] if nb > 1 else None
        row_parts = [(0, tm)] if nb > 1 else [(r, tm // M_SPLIT) for r in range(0, tm, tm // M_SPLIT)]
        pieces = [(w_ref, c0, half * tn + c0, r0, rn)
                  for half, w_ref in enumerate((wv_ref, wg_ref))
                  for c0 in range(0, tn, MXU_N) for r0, rn in row_parts]
        per_piece = -(-len(strips) // len(pieces))
        for n, (w_ref, c0, dst, r0, rn) in enumerate(pieces):
            if nb > 1:
                u = _dot(hn_ref[...], w_ref[:, c0:c0 + MXU_N])
                for seg in range(nb):
                    base = seg * (tm + HALO) + HALO
                    u_next[base:base + tm, dst:dst + MXU_N] = u[seg * tm:(seg + 1) * tm]
            else:
                u_next[HALO + r0:HALO + r0 + rn, dst:dst + MXU_N] = _dot(
                    hn_ref[r0:r0 + rn, :], w_ref[:, c0:c0 + MXU_N])
            for seg, s0 in strips[n * per_piece:(n + 1) * per_piece]:
                strip(seg, s0)

        for seg in range(nb):
            base = seg * (tm + HALO) + HALO
            tail = u_prev[base + tm - 2:base + tm, :]
            c_ref[jp, seg, 0:2, :] = tail
            nfv_ref[seg, 0] = tail[:, :tn]
            nfg_ref[seg, 0] = tail[:, tn:]

    parity = lax.rem(s, 2)

    @pl.when(parity == 0)
    def _():
        step(ub_ref, ua_ref)

    @pl.when(parity == 1)
    def _():
        step(ua_ref, ub_ref)


def _ffn_up(x1, g, w_up, conv_w, conv_b, state, *, nb, tm, tn):
    bsz, t, _ = x1.shape
    has_state = state is not None
    n_b, n_i, n_j = bsz // nb, t // tm, D_FF // tn
    n_tiles = n_b * n_i * n_j

    def cur(s):
        sc = jnp.minimum(s, n_tiles - 1)
        return sc // (n_i * n_j), (sc // n_j) % n_i, sc % n_j

    def prev(s):
        return cur(jnp.maximum(s - 1, 0))

    in_specs = [
        pl.BlockSpec((nb, tm, D_MODEL), lambda s: cur(s)[:2] + (0,)),
        pl.BlockSpec((1, D_MODEL), lambda s: (0, 0)),
        pl.BlockSpec((D_MODEL, tn), lambda s: (0, cur(s)[2])),
        pl.BlockSpec((D_MODEL, tn), lambda s: (0, n_j + cur(s)[2])),
        pl.BlockSpec((CONV_W, tn), lambda s: (0, prev(s)[2])),
        pl.BlockSpec((CONV_W, tn), lambda s: (0, n_j + prev(s)[2])),
        pl.BlockSpec((1, tn), lambda s: (0, prev(s)[2])),
        pl.BlockSpec((1, tn), lambda s: (0, n_j + prev(s)[2])),
    ]
    args = [x1, g, w_up, w_up, conv_w, conv_w, conv_b, conv_b]
    if has_state:
        in_specs += [
            pl.BlockSpec((nb, CONV_W - 1, tn), lambda s: (prev(s)[0], 0, prev(s)[2])),
            pl.BlockSpec((nb, CONV_W - 1, tn), lambda s: (prev(s)[0], 0, n_j + prev(s)[2])),
        ]
        args += [state, state]
    tail_spec = pl.BlockSpec((nb, 1, CONV_W - 1, tn),
                             lambda s: (prev(s)[0], prev(s)[1], 0, prev(s)[2]))
    return pl.pallas_call(
        functools.partial(_ffn_up_kernel, nb=nb, tm=tm, tn=tn, n_i=n_i, n_j=n_j, n_tiles=n_tiles,
                          has_state=has_state),
        grid=(n_tiles + 1,),
        in_specs=in_specs,
        out_specs=[
            pl.BlockSpec((nb, tm, tn), lambda s: prev(s)),
            tail_spec,
            tail_spec,
        ],
        out_shape=[
            jax.ShapeDtypeStruct((bsz, t, D_FF), BF16),
            jax.ShapeDtypeStruct((bsz, n_i, CONV_W - 1, D_FF), F32),
            jax.ShapeDtypeStruct((bsz, n_i, CONV_W - 1, D_FF), F32),
        ],
        scratch_shapes=[
            pltpu.VMEM((nb * tm, D_MODEL), BF16),
            pltpu.VMEM((nb * (tm + HALO), 2 * tn), F32),
            pltpu.VMEM((nb * (tm + HALO), 2 * tn), F32),
            pltpu.VMEM((n_j, nb, HALO, 2 * tn), F32),
        ],
        compiler_params=pltpu.CompilerParams(
            dimension_semantics=("arbitrary",),
            vmem_limit_bytes=VMEM_LIMIT),
        name="ffn_up",
    )(*args)


def _ffn_down_kernel(h_ref, x1_ref, p_ref, wd_ref, gp_ref, wpg_ref, wple_ref, gf_ref, y_ref):
    x2 = x1_ref[...] + _dot(h_ref[...], wd_ref[...])
    pn = (x2 * _rms_scale(x2) * gp_ref[...]).astype(BF16)
    gate = _sigmoid(_dot(pn, wpg_ref[...]))
    pe = _dot(p_ref[...].astype(BF16), wple_ref[...])
    x3 = x2 + gate * pe
    y_ref[...] = x3 * _rms_scale(x3) * gf_ref[...]


def _ffn_down(h, x1, p, w_down, g_ple, w_pg, w_ple, g_final, *, tm):
    m = x1.shape[0]
    const = dict(pipeline_mode=pl.Buffered(1))
    return pl.pallas_call(
        _ffn_down_kernel,
        grid=(m // tm,),
        in_specs=[
            pl.BlockSpec((tm, D_FF), lambda i: (i, 0)),
            pl.BlockSpec((tm, D_MODEL), lambda i: (i, 0)),
            pl.BlockSpec((tm, D_PLE), lambda i: (i, 0)),
            pl.BlockSpec((D_FF, D_MODEL), lambda i: (0, 0), **const),
            pl.BlockSpec((1, D_MODEL), lambda i: (0, 0)),
            pl.BlockSpec((D_MODEL, D_MODEL), lambda i: (0, 0), **const),
            pl.BlockSpec((D_PLE, D_MODEL), lambda i: (0, 0), **const),
            pl.BlockSpec((1, D_MODEL), lambda i: (0, 0)),
        ],
        out_specs=pl.BlockSpec((tm, D_MODEL), lambda i: (i, 0)),
        out_shape=jax.ShapeDtypeStruct((m, D_MODEL), F32),
        compiler_params=pltpu.CompilerParams(
            dimension_semantics=("parallel",),
            vmem_limit_bytes=VMEM_LIMIT),
        name="ffn_down",
    )(h, x1, p, w_down, g_ple, w_pg, w_ple, g_final)


def _tiles(bsz, t):
    if t >= 1024:
        return dict(nb=1, tm_in=1024, tm_mix=256, tm_up=512, tq=256, diag=128)
    return dict(nb=bsz, tm_in=t, tm_mix=t, tm_up=t, tq=t, diag=t)


def _layer(x, p, conv_state, gla_state, ffn_state, w, g_final):
    bsz, t, _ = x.shape
    cfg = _tiles(bsz, t)
    nb = cfg["nb"]
    proj, loga = _inproj(x, w["norm_mix"], w["w_main"], w["w_alr"], w["w_g2"], w["b_gate"],
                         nb=nb, tm=cfg["tm_in"], tn=1024)
    og, new_s = _gla(proj, loga, w["gla_norm"], gla_state, tq=cfg["tq"], diag=cfg["diag"])
    x1, new_conv = _mix(proj, og, x, w["conv_a_w"], conv_state, w["w_a_out"], w["w_b_out"], w["w_o"],
                        nb=nb, tm=cfg["tm_mix"])
    h, nf_v, nf_g = _ffn_up(x1, w["norm_ffn"], w["w_up"], w["ffn_conv_w"], w["ffn_conv_b"], ffn_state,
                            nb=nb, tm=cfg["tm_up"], tn=512)
    m = bsz * t
    y = _ffn_down(h.reshape(m, D_FF), x1.reshape(m, D_MODEL), p.reshape(m, D_PLE),
                  w["w_down"], w["norm_ple"], w["w_ple_gate"], w["w_ple"], g_final, tm=256)
    new_ffn = jnp.concatenate([nf_v[:, -1], nf_g[:, -1]], axis=-1)
    return y.reshape(bsz, t, D_MODEL), new_conv, new_s, new_ffn


def kernel(x_prompt, x_sample, p_prompt, p_sample, state_conv_a, state_gla, state_ffn_conv, norm_mix, w_in, conv_a_w, w_a_out, w_gate2, b_gate, gla_norm, w_b_out, w_o, norm_ffn, w_up, ffn_conv_w, ffn_conv_b, w_down, norm_ple, w_ple_gate, w_ple, norm_final):
    depth = w_in.shape[0]
    assert depth == 1, "the final norm is fused into the last layer; one layer supported"
    i = 0
    w_in_i = w_in[i]
    w_alr = jnp.pad(w_in_i[:, ALR_OFF:ALR_OFF + GATE_RANK], ((0, 0), (0, LANE - GATE_RANK)))
    w = dict(
        norm_mix=norm_mix[i][None],
        w_main=jnp.concatenate([w_in_i[:, :ALR_OFF].astype(BF16),
                                w_in_i[:, ALR_OFF + GATE_RANK:].astype(BF16)], axis=1),
        w_alr=w_alr.astype(BF16),
        w_g2=jnp.pad(w_gate2[i], ((0, LANE - GATE_RANK), (0, 0))).astype(BF16),
        b_gate=b_gate[i][None],
        gla_norm=gla_norm[i][None],
        conv_a_w=conv_a_w[i],
        w_a_out=w_a_out[i].astype(BF16),
        w_b_out=w_b_out[i].astype(BF16),
        w_o=w_o[i].astype(BF16),
        norm_ffn=norm_ffn[i][None],
        w_up=w_up[i].astype(BF16),
        ffn_conv_w=ffn_conv_w[i],
        ffn_conv_b=ffn_conv_b[i][None],
        w_down=w_down[i].astype(BF16),
        norm_ple=norm_ple[i][None],
        w_ple_gate=w_ple_gate[i].astype(BF16),
        w_ple=w_ple[i].astype(BF16),
    )
    g_final = norm_final[None]
    yp, c1, s1, f1 = _layer(x_prompt, p_prompt[i], None, None, None, w, g_final)
    ys, c2, s2, f2 = _layer(x_sample, p_sample[i], state_conv_a[i], state_gla[i], state_ffn_conv[i],
                            w, g_final)
    return (yp, ys, c1[None], s1[None], f1[None], c2[None], s2[None], f2[None])
```

```python
import functools

import jax
import jax.numpy as jnp
from jax import lax
from jax.experimental import pallas as pl
from jax.experimental.pallas import tpu as pltpu

F32 = jnp.float32
BF16 = jnp.bfloat16

D_MODEL = 2048
CONV_W = 3
GLA_HEADS = 4
GLA_DK = D_MODEL // 2
GLA_DV = D_MODEL
GLA_HK = GLA_DK // GLA_HEADS
GLA_HV = GLA_DV // GLA_HEADS
GATE_RANK = 16
GATE_NORMALIZER = 16.0
D_FF = 5632
D_PLE = 256
EPS = 1e-6

D_MAIN = 3 * D_MODEL + 2 * GLA_DK + 2 * GLA_DV + 2 * D_MODEL
OFF_BA, OFF_CA, OFF_VA = 0, D_MODEL, 2 * D_MODEL
OFF_Q = 3 * D_MODEL
OFF_K = OFF_Q + GLA_DK
OFF_V = OFF_K + GLA_DK
OFF_GO = OFF_V + GLA_DV
OFF_MA = OFF_GO + GLA_DV
OFF_MB = OFF_MA + D_MODEL
ALR_OFF = OFF_MA
LANE = 128
BF16_ROWS = 16
HALO = 8
STRIP = 16
MXU_N = 256
M_SPLIT = 4
VMEM_LIMIT = 56 * 1024 * 1024

NT_DIMS = (((1,), (1,)), ((), ()))
TN_DIMS = (((0,), (0,)), ((), ()))


def _rms_scale(x):
    return lax.rsqrt(jnp.mean(x * x, axis=-1, keepdims=True) + EPS)


def _sigmoid(x):
    return 1.0 / (1.0 + jnp.exp(-x))


def _log_sigmoid(z):
    return jnp.minimum(z, 0.0) - jnp.log(1.0 + jnp.exp(-jnp.abs(z)))


def _dot(a, b):
    return jnp.dot(a, b, preferred_element_type=F32)


def _split_bf16(x):
    hi = x.astype(BF16)
    return hi, (x - hi.astype(F32)).astype(BF16)


def _conv3(u, h0, h1, w):
    t = u.shape[0]
    rows = lax.broadcasted_iota(jnp.int32, (t, 1), 0)
    sh1 = jnp.where(rows == 0, h1, pltpu.roll(u, 1, axis=0))
    sh2 = jnp.where(rows == 0, h0, jnp.where(rows == 1, h1, pltpu.roll(u, 2, axis=0)))
    return w[0:1] * sh2 + w[1:2] * sh1 + w[2:3] * u


def _inproj_kernel(x_ref, g_ref, w_ref, walr_ref, wg2_ref, bg_ref, proj_ref, loga_ref, xn_ref,
                   *, nb, tm):
    j = pl.program_id(2)
    rows = nb * tm

    @pl.when(j == 0)
    def _():
        x = x_ref[...].reshape(rows, D_MODEL)
        xn = (x * _rms_scale(x) * g_ref[...]).astype(BF16)
        xn_ref[...] = xn
        a_lr = _dot(xn, walr_ref[...])
        z = _dot(a_lr.astype(BF16), wg2_ref[...]) + bg_ref[...]
        loga_ref[...] = (_log_sigmoid(z) * (1.0 / GATE_NORMALIZER)).reshape(nb, tm, GLA_DK)

    tn = proj_ref.shape[-1]
    proj_ref[...] = _dot(xn_ref[...], w_ref[0]).astype(BF16).reshape(nb, tm, tn)


def _inproj(x, g, w_main, w_alr, w_g2, b_gate, *, nb, tm, tn):
    bsz, t, _ = x.shape
    grid = (bsz // nb, t // tm, D_MAIN // tn)
    return pl.pallas_call(
        functools.partial(_inproj_kernel, nb=nb, tm=tm),
        grid=grid,
        in_specs=[
            pl.BlockSpec((nb, tm, D_MODEL), lambda b, i, j: (b, i, 0)),
            pl.BlockSpec((1, D_MODEL), lambda b, i, j: (0, 0)),
            pl.BlockSpec((1, D_MODEL, tn), lambda b, i, j: (j, 0, 0)),
            pl.BlockSpec((D_MODEL, LANE), lambda b, i, j: (0, 0)),
            pl.BlockSpec((LANE, GLA_DK), lambda b, i, j: (0, 0)),
            pl.BlockSpec((1, GLA_DK), lambda b, i, j: (0, 0)),
        ],
        out_specs=[
            pl.BlockSpec((nb, tm, tn), lambda b, i, j: (b, i, j)),
            pl.BlockSpec((nb, tm, GLA_DK), lambda b, i, j: (b, i, 0)),
        ],
        out_shape=[
            jax.ShapeDtypeStruct((bsz, t, D_MAIN), BF16),
            jax.ShapeDtypeStruct((bsz, t, GLA_DK), F32),
        ],
        scratch_shapes=[pltpu.VMEM((nb * tm, D_MODEL), BF16)],
        compiler_params=pltpu.CompilerParams(
            dimension_semantics=("parallel", "parallel", "arbitrary"),
            vmem_limit_bytes=VMEM_LIMIT),
        name="inproj",
    )(x, g, w_main, w_alr, w_g2, b_gate)


def _gla_head(q, k, v, la, s_old, *, diag):
    rows = q.shape[0]
    n_d = rows // diag
    r_i = lax.broadcasted_iota(jnp.int32, (rows, rows), 0)
    c_i = lax.broadcasted_iota(jnp.int32, (rows, rows), 1)
    tril = (r_i >= c_i).astype(BF16)
    la_hi, la_lo = _split_bf16(la)
    b = _dot(tril, la_hi) + _dot(tril, la_lo)
    b_end = b[rows - 1:rows, :]

    causal_d = (lax.broadcasted_iota(jnp.int32, (diag, diag), 0)
                >= lax.broadcasted_iota(jnp.int32, (diag, diag), 1))
    a_c, k_c, cen = [], [], []
    for d in range(n_d):
        lo = d * diag
        mid = lo + diag // 2
        c = b[mid - 1:mid, :]
        bd = b[lo:lo + diag, :]
        cen.append(c)
        a_c.append(q[lo:lo + diag, :] * jnp.exp(bd - c))
        k_c.append(k[lo:lo + diag, :] * jnp.exp(c - bd))

    def block(lo_d, hi_d):
        if hi_d - lo_d == 1:
            s = lax.dot_general(a_c[lo_d].astype(BF16), k_c[lo_d].astype(BF16), NT_DIMS,
                                preferred_element_type=F32)
            return jnp.where(causal_d, s, 0.0)
        mid_d = (lo_d + hi_d) // 2
        ref = b[mid_d * diag - 1:mid_d * diag, :]
        lhs = [a_c[d] * jnp.exp(cen[d] - ref) for d in range(mid_d, hi_d)]
        rhs = [k_c[d] * jnp.exp(ref - cen[d]) for d in range(lo_d, mid_d)]
        lhs = (lhs[0] if len(lhs) == 1 else jnp.concatenate(lhs, axis=0)).astype(BF16)
        rhs = (rhs[0] if len(rhs) == 1 else jnp.concatenate(rhs, axis=0)).astype(BF16)
        off = lax.dot_general(lhs, rhs, NT_DIMS, preferred_element_type=F32)
        top = jnp.concatenate([block(lo_d, mid_d), jnp.zeros_like(off)], axis=1)
        bot = jnp.concatenate([off, block(mid_d, hi_d)], axis=1)
        return jnp.concatenate([top, bot], axis=0)

    p = block(0, n_d).astype(BF16)
    qt = [a_c[d] * jnp.exp(cen[d]) for d in range(n_d)]
    kt = [k_c[d] * jnp.exp(b_end - cen[d]) for d in range(n_d)]
    qt = (qt[0] if n_d == 1 else jnp.concatenate(qt, axis=0)).astype(BF16)
    kt = (kt[0] if n_d == 1 else jnp.concatenate(kt, axis=0)).astype(BF16)
    o = _dot(qt, s_old.astype(BF16)) + _dot(p, v)
    e_rep = jnp.broadcast_to(b_end * (1.0 / BF16_ROWS), (BF16_ROWS, b_end.shape[1]))
    e_hi, e_lo = _split_bf16(e_rep)
    ones_col = jnp.ones((BF16_ROWS, LANE), BF16)
    dcol = (lax.dot_general(e_hi, ones_col, TN_DIMS, preferred_element_type=F32)
            + lax.dot_general(e_lo, ones_col, TN_DIMS, preferred_element_type=F32))
    decay = jnp.tile(jnp.exp(dcol), (1, v.shape[1] // LANE))
    s_new = s_old * decay + lax.dot_general(kt, v, TN_DIMS, preferred_element_type=F32)
    return o, s_new


def _gla_kernel(*refs, diag, has_state):
    if has_state:
        q_ref, k_ref, v_ref, go_ref, la_ref, gn_ref, s0_ref, o_ref, sout_ref, s_ref = refs
    else:
        q_ref, k_ref, v_ref, go_ref, la_ref, gn_ref, o_ref, sout_ref, s_ref = refs
    i = pl.program_id(1)

    @pl.when(i == 0)
    def _():
        if has_state:
            s_ref[...] = s0_ref[0]
        else:
            s_ref[...] = jnp.zeros_like(s_ref)

    for h in range(GLA_HEADS):
        ck = pl.ds(h * GLA_HK, GLA_HK)
        cv = pl.ds(h * GLA_HV, GLA_HV)
        q = q_ref[0, :, ck].astype(F32) * (GLA_HK ** -0.5)
        k = k_ref[0, :, ck].astype(F32)
        o, s_new = _gla_head(q, k, v_ref[0, :, cv], la_ref[0, :, ck], s_ref[h], diag=diag)
        s_ref[h] = s_new
        g = go_ref[0, :, cv].astype(F32)
        o_ref[0, :, cv] = (o * _rms_scale(o) * gn_ref[:, cv] * (g * _sigmoid(g))).astype(BF16)

    @pl.when(i == pl.num_programs(1) - 1)
    def _():
        sout_ref[0] = s_ref[...]


def _gla(proj, loga, gla_norm, state, *, tq, diag):
    bsz, t, _ = proj.shape
    has_state = state is not None
    grid = (bsz, t // tq)
    state_spec = pl.BlockSpec((1, GLA_HEADS, GLA_HK, GLA_HV), lambda b, i: (b, 0, 0, 0))
    in_specs = [
        pl.BlockSpec((1, tq, GLA_DK), lambda b, i: (b, i, OFF_Q // GLA_DK)),
        pl.BlockSpec((1, tq, GLA_DK), lambda b, i: (b, i, OFF_K // GLA_DK)),
        pl.BlockSpec((1, tq, GLA_DV), lambda b, i: (b, i, OFF_V // GLA_DV)),
        pl.BlockSpec((1, tq, GLA_DV), lambda b, i: (b, i, OFF_GO // GLA_DV)),
        pl.BlockSpec((1, tq, GLA_DK), lambda b, i: (b, i, 0)),
        pl.BlockSpec((1, GLA_DV), lambda b, i: (0, 0)),
    ]
    args = [proj, proj, proj, proj, loga, gla_norm]
    if has_state:
        in_specs.append(state_spec)
        args.append(state)
    return pl.pallas_call(
        functools.partial(_gla_kernel, diag=diag, has_state=has_state),
        grid=grid,
        in_specs=in_specs,
        out_specs=[
            pl.BlockSpec((1, tq, GLA_DV), lambda b, i: (b, i, 0)),
            state_spec,
        ],
        out_shape=[
            jax.ShapeDtypeStruct((bsz, t, GLA_DV), BF16),
            jax.ShapeDtypeStruct((bsz, GLA_HEADS, GLA_HK, GLA_HV), F32),
        ],
        scratch_shapes=[pltpu.VMEM((GLA_HEADS, GLA_HK, GLA_HV), F32)],
        compiler_params=pltpu.CompilerParams(
            dimension_semantics=("parallel", "arbitrary"),
            vmem_limit_bytes=VMEM_LIMIT),
        name="gla",
    )(*args)


def _mix_kernel(*refs, nb, tm, has_state):
    if has_state:
        (ba_ref, ca_ref, va_ref, ma_ref, mb_ref, og_ref, x_ref, cw_ref, st_ref,
         wa_ref, wb_ref, wo_ref, x1_ref, newc_ref, carry_ref) = refs
    else:
        (ba_ref, ca_ref, va_ref, ma_ref, mb_ref, og_ref, x_ref, cw_ref,
         wa_ref, wb_ref, wo_ref, x1_ref, newc_ref, carry_ref) = refs
    i = pl.program_id(1)
    rows = nb * tm
    first = i == 0
    cw = cw_ref[...]

    y_b = _dot(og_ref[...].reshape(rows, D_MODEL), wb_ref[...])

    conv_parts = []
    for s in range(nb):
        cv = ca_ref[s].astype(F32) * va_ref[s].astype(F32)
        init = st_ref[s] if has_state else jnp.zeros((CONV_W - 1, D_MODEL), F32)
        halo = jnp.where(first, init, carry_ref[s, 0:2, :])
        conv_parts.append(_conv3(cv, halo[0:1], halo[1:2], cw))
        tail = cv[tm - 2:tm, :]
        carry_ref[s, 0:2, :] = tail
        newc_ref[s] = tail
    conv = conv_parts[0] if nb == 1 else jnp.concatenate(conv_parts, axis=0)

    b_a = ba_ref[...].reshape(rows, D_MODEL).astype(F32)
    y_a = _dot((b_a * conv).astype(BF16), wa_ref[...])
    m_a = ma_ref[...].reshape(rows, D_MODEL).astype(F32)
    m_b = mb_ref[...].reshape(rows, D_MODEL).astype(F32)
    z = _sigmoid(m_a) * y_a + _sigmoid(m_b) * y_b
    x1 = x_ref[...].reshape(rows, D_MODEL) + _dot(z.astype(BF16), wo_ref[...])
    x1_ref[...] = x1.reshape(nb, tm, D_MODEL)


def _mix(proj, og, x, conv_w, state, w_a, w_b, w_o, *, nb, tm):
    bsz, t, _ = x.shape
    has_state = state is not None
    grid = (bsz // nb, t // tm)

    def col(off):
        return pl.BlockSpec((nb, tm, D_MODEL), lambda b, i: (b, i, off // D_MODEL))

    row_spec = pl.BlockSpec((nb, tm, D_MODEL), lambda b, i: (b, i, 0))
    w_spec = pl.BlockSpec((D_MODEL, D_MODEL), lambda b, i: (0, 0), pipeline_mode=pl.Buffered(1))
    in_specs = [col(OFF_BA), col(OFF_CA), col(OFF_VA), col(OFF_MA), col(OFF_MB), row_spec, row_spec,
                pl.BlockSpec((CONV_W, D_MODEL), lambda b, i: (0, 0))]
    args = [proj, proj, proj, proj, proj, og, x, conv_w]
    if has_state:
        in_specs.append(pl.BlockSpec((nb, CONV_W - 1, D_MODEL), lambda b, i: (b, 0, 0)))
        args.append(state)
    in_specs += [w_spec, w_spec, w_spec]
    args += [w_a, w_b, w_o]
    return pl.pallas_call(
        functools.partial(_mix_kernel, nb=nb, tm=tm, has_state=has_state),
        grid=grid,
        in_specs=in_specs,
        out_specs=[
            row_spec,
            pl.BlockSpec((nb, CONV_W - 1, D_MODEL), lambda b, i: (b, 0, 0)),
        ],
        out_shape=[
            jax.ShapeDtypeStruct((bsz, t, D_MODEL), F32),
            jax.ShapeDtypeStruct((bsz, CONV_W - 1, D_MODEL), F32),
        ],
        scratch_shapes=[pltpu.VMEM((nb, 8, D_MODEL), F32)],
        compiler_params=pltpu.CompilerParams(
            dimension_semantics=("parallel", "arbitrary"),
            vmem_limit_bytes=VMEM_LIMIT),
        name="mix",
    )(*args)


def _ffn_up_kernel(*refs, nb, tm, tn, n_i, n_j, n_tiles, has_state):
    if has_state:
        (x1_ref, g_ref, wv_ref, wg_ref, cwv_ref, cwg_ref, bv_ref, bg_ref, stv_ref, stg_ref,
         h_ref, nfv_ref, nfg_ref, hn_ref, ua_ref, ub_ref, c_ref) = refs
    else:
        (x1_ref, g_ref, wv_ref, wg_ref, cwv_ref, cwg_ref, bv_ref, bg_ref,
         h_ref, nfv_ref, nfg_ref, hn_ref, ua_ref, ub_ref, c_ref) = refs
    s = pl.program_id(0)
    rows = nb * tm
    j = lax.rem(jnp.minimum(s, n_tiles - 1), n_j)
    sp = jnp.maximum(s - 1, 0)
    jp = lax.rem(sp, n_j)
    first = lax.rem(lax.div(sp, n_j), n_i) == 0

    @pl.when(s == 0)
    def _():
        ub_ref[...] = jnp.zeros_like(ub_ref)
        c_ref[...] = jnp.zeros_like(c_ref)

    @pl.when(jnp.logical_and(j == 0, s < n_tiles))
    def _():
        x1 = x1_ref[...].reshape(rows, D_MODEL)
        hn_ref[...] = (x1 * _rms_scale(x1) * g_ref[...]).astype(BF16)

    def step(u_prev, u_next):
        cw = jnp.concatenate([cwv_ref[...], cwg_ref[...]], axis=1)
        bias = jnp.concatenate([bv_ref[...], bg_ref[...]], axis=1)
        for seg in range(nb):
            base = seg * (tm + HALO) + HALO
            if has_state:
                init = jnp.concatenate([stv_ref[seg], stg_ref[seg]], axis=1)
            else:
                init = jnp.zeros((CONV_W - 1, 2 * tn), F32)
            u_prev[base - 2:base, :] = jnp.where(first, init, c_ref[jp, seg, 0:2, :])

        def strip(seg, r0):
            lo = seg * (tm + HALO) + HALO + r0
            win = u_prev[lo - HALO:lo + STRIP, :]
            cur = win[HALO:]
            sh1 = pltpu.roll(win, 1, axis=0)[HALO:]
            sh2 = pltpu.roll(win, 2, axis=0)[HALO:]
            c = cw[0:1] * sh2 + cw[1:2] * sh1 + cw[2:3] * cur + bias
            gate = c[:, tn:]
            h_ref[seg, r0:r0 + STRIP, :] = (gate * _sigmoid(gate) * c[:, :tn]).astype(BF16)

        strips = [(seg, r0) for seg in range(nb) for r0 in range(0, tm, STRIP)]
        seg_groups = [list(range(nb))] if nb > 1 else None
        row_parts = [(0, tm)] if nb > 1 else [(r, tm // M_SPLIT) for r in range(0, tm, tm // M_SPLIT)]
        pieces = [(w_ref, c0, half * tn + c0, r0, rn)
                  for half, w_ref in enumerate((wv_ref, wg_ref))
                  for c0 in range(0, tn, MXU_N) for r0, rn in row_parts]
        per_piece = -(-len(strips) // len(pieces))
        for n, (w_ref, c0, dst, r0, rn) in enumerate(pieces):
            if nb > 1:
                u = _dot(hn_ref[...], w_ref[0, :, c0:c0 + MXU_N])
                for seg in range(nb):
                    base = seg * (tm + HALO) + HALO
                    u_next[base:base + tm, dst:dst + MXU_N] = u[seg * tm:(seg + 1) * tm]
            else:
                u_next[HALO + r0:HALO + r0 + rn, dst:dst + MXU_N] = _dot(
                    hn_ref[r0:r0 + rn, :], w_ref[0, :, c0:c0 + MXU_N])
            for seg, s0 in strips[n * per_piece:(n + 1) * per_piece]:
                strip(seg, s0)

        for seg in range(nb):
            base = seg * (tm + HALO) + HALO
            tail = u_prev[base + tm - 2:base + tm, :]
            c_ref[jp, seg, 0:2, :] = tail
            nfv_ref[seg, 0] = tail[:, :tn]
            nfg_ref[seg, 0] = tail[:, tn:]

    parity = lax.rem(s, 2)

    @pl.when(parity == 0)
    def _():
        step(ub_ref, ua_ref)

    @pl.when(parity == 1)
    def _():
        step(ua_ref, ub_ref)


def _ffn_up(x1, g, w_up, conv_w, conv_b, state, *, nb, tm, tn):
    bsz, t, _ = x1.shape
    has_state = state is not None
    n_b, n_i, n_j = bsz // nb, t // tm, D_FF // tn
    n_tiles = n_b * n_i * n_j

    def cur(s):
        sc = jnp.minimum(s, n_tiles - 1)
        return sc // (n_i * n_j), (sc // n_j) % n_i, sc % n_j

    def prev(s):
        return cur(jnp.maximum(s - 1, 0))

    in_specs = [
        pl.BlockSpec((nb, tm, D_MODEL), lambda s: cur(s)[:2] + (0,)),
        pl.BlockSpec((1, D_MODEL), lambda s: (0, 0)),
        pl.BlockSpec((1, D_MODEL, tn), lambda s: (cur(s)[2], 0, 0)),
        pl.BlockSpec((1, D_MODEL, tn), lambda s: (n_j + cur(s)[2], 0, 0)),
        pl.BlockSpec((CONV_W, tn), lambda s: (0, prev(s)[2])),
        pl.BlockSpec((CONV_W, tn), lambda s: (0, n_j + prev(s)[2])),
        pl.BlockSpec((1, tn), lambda s: (0, prev(s)[2])),
        pl.BlockSpec((1, tn), lambda s: (0, n_j + prev(s)[2])),
    ]
    args = [x1, g, w_up, w_up, conv_w, conv_w, conv_b, conv_b]
    if has_state:
        in_specs += [
            pl.BlockSpec((nb, CONV_W - 1, tn), lambda s: (prev(s)[0], 0, prev(s)[2])),
            pl.BlockSpec((nb, CONV_W - 1, tn), lambda s: (prev(s)[0], 0, n_j + prev(s)[2])),
        ]
        args += [state, state]
    tail_spec = pl.BlockSpec((nb, 1, CONV_W - 1, tn),
                             lambda s: (prev(s)[0], prev(s)[1], 0, prev(s)[2]))
    return pl.pallas_call(
        functools.partial(_ffn_up_kernel, nb=nb, tm=tm, tn=tn, n_i=n_i, n_j=n_j, n_tiles=n_tiles,
                          has_state=has_state),
        grid=(n_tiles + 1,),
        in_specs=in_specs,
        out_specs=[
            pl.BlockSpec((nb, tm, tn), lambda s: prev(s)),
            tail_spec,
            tail_spec,
        ],
        out_shape=[
            jax.ShapeDtypeStruct((bsz, t, D_FF), BF16),
            jax.ShapeDtypeStruct((bsz, n_i, CONV_W - 1, D_FF), F32),
            jax.ShapeDtypeStruct((bsz, n_i, CONV_W - 1, D_FF), F32),
        ],
        scratch_shapes=[
            pltpu.VMEM((nb * tm, D_MODEL), BF16),
            pltpu.VMEM((nb * (tm + HALO), 2 * tn), F32),
            pltpu.VMEM((nb * (tm + HALO), 2 * tn), F32),
            pltpu.VMEM((n_j, nb, HALO, 2 * tn), F32),
        ],
        compiler_params=pltpu.CompilerParams(
            dimension_semantics=("arbitrary",),
            vmem_limit_bytes=VMEM_LIMIT),
        name="ffn_up",
    )(*args)


def _ffn_down_kernel(h_ref, x1_ref, p_ref, wd_ref, gp_ref, wpg_ref, wple_ref, gf_ref, y_ref):
    x2 = x1_ref[...] + _dot(h_ref[...], wd_ref[...])
    pn = (x2 * _rms_scale(x2) * gp_ref[...]).astype(BF16)
    gate = _sigmoid(_dot(pn, wpg_ref[...]))
    pe = _dot(p_ref[...].astype(BF16), wple_ref[...])
    x3 = x2 + gate * pe
    y_ref[...] = x3 * _rms_scale(x3) * gf_ref[...]


def _ffn_down(h, x1, p, w_down, g_ple, w_pg, w_ple, g_final, *, tm):
    m = x1.shape[0]
    const = dict(pipeline_mode=pl.Buffered(1))
    return pl.pallas_call(
        _ffn_down_kernel,
        grid=(m // tm,),
        in_specs=[
            pl.BlockSpec((tm, D_FF), lambda i: (i, 0)),
            pl.BlockSpec((tm, D_MODEL), lambda i: (i, 0)),
            pl.BlockSpec((tm, D_PLE), lambda i: (i, 0)),
            pl.BlockSpec((D_FF, D_MODEL), lambda i: (0, 0), **const),
            pl.BlockSpec((1, D_MODEL), lambda i: (0, 0)),
            pl.BlockSpec((D_MODEL, D_MODEL), lambda i: (0, 0), **const),
            pl.BlockSpec((D_PLE, D_MODEL), lambda i: (0, 0), **const),
            pl.BlockSpec((1, D_MODEL), lambda i: (0, 0)),
        ],
        out_specs=pl.BlockSpec((tm, D_MODEL), lambda i: (i, 0)),
        out_shape=jax.ShapeDtypeStruct((m, D_MODEL), F32),
        compiler_params=pltpu.CompilerParams(
            dimension_semantics=("parallel",),
            vmem_limit_bytes=VMEM_LIMIT),
        name="ffn_down",
    )(h, x1, p, w_down, g_ple, w_pg, w_ple, g_final)


TN_IN = 1024
TN_UP = 512


def _tile_major(w, tn):
    k, n = w.shape
    return w.reshape(k, n // tn, tn).transpose(1, 0, 2).astype(BF16)


def _tiles(bsz, t):
    if t >= 1024:
        return dict(nb=1, tm_in=1024, tm_mix=256, tm_up=1024, tq=256, diag=128)
    return dict(nb=bsz, tm_in=t, tm_mix=t, tm_up=t, tq=t, diag=t)


def _layer(x, p, conv_state, gla_state, ffn_state, w, g_final):
    bsz, t, _ = x.shape
    cfg = _tiles(bsz, t)
    nb = cfg["nb"]
    proj, loga = _inproj(x, w["norm_mix"], w["w_main"], w["w_alr"], w["w_g2"], w["b_gate"],
                         nb=nb, tm=cfg["tm_in"], tn=TN_IN)
    og, new_s = _gla(proj, loga, w["gla_norm"], gla_state, tq=cfg["tq"], diag=cfg["diag"])
    x1, new_conv = _mix(proj, og, x, w["conv_a_w"], conv_state, w["w_a_out"], w["w_b_out"], w["w_o"],
                        nb=nb, tm=cfg["tm_mix"])
    h, nf_v, nf_g = _ffn_up(x1, w["norm_ffn"], w["w_up"], w["ffn_conv_w"], w["ffn_conv_b"], ffn_state,
                            nb=nb, tm=cfg["tm_up"], tn=TN_UP)
    m = bsz * t
    y = _ffn_down(h.reshape(m, D_FF), x1.reshape(m, D_MODEL), p.reshape(m, D_PLE),
                  w["w_down"], w["norm_ple"], w["w_ple_gate"], w["w_ple"], g_final, tm=256)
    new_ffn = jnp.concatenate([nf_v[:, -1], nf_g[:, -1]], axis=-1)
    return y.reshape(bsz, t, D_MODEL), new_conv, new_s, new_ffn


def kernel(x_prompt, x_sample, p_prompt, p_sample, state_conv_a, state_gla, state_ffn_conv, norm_mix, w_in, conv_a_w, w_a_out, w_gate2, b_gate, gla_norm, w_b_out, w_o, norm_ffn, w_up, ffn_conv_w, ffn_conv_b, w_down, norm_ple, w_ple_gate, w_ple, norm_final):
    depth = w_in.shape[0]
    assert depth == 1, "the final norm is fused into the last layer; one layer supported"
    i = 0
    w_in_i = w_in[i]
    w_alr = jnp.pad(w_in_i[:, ALR_OFF:ALR_OFF + GATE_RANK], ((0, 0), (0, LANE - GATE_RANK)))
    w = dict(
        norm_mix=norm_mix[i][None],
        w_main=_tile_major(jnp.concatenate([w_in_i[:, :ALR_OFF], w_in_i[:, ALR_OFF + GATE_RANK:]],
                                           axis=1), TN_IN),
        w_alr=w_alr.astype(BF16),
        w_g2=jnp.pad(w_gate2[i], ((0, LANE - GATE_RANK), (0, 0))).astype(BF16),
        b_gate=b_gate[i][None],
        gla_norm=gla_norm[i][None],
        conv_a_w=conv_a_w[i],
        w_a_out=w_a_out[i].astype(BF16),
        w_b_out=w_b_out[i].astype(BF16),
        w_o=w_o[i].astype(BF16),
        norm_ffn=norm_ffn[i][None],
        w_up=_tile_major(w_up[i], TN_UP),
        ffn_conv_w=ffn_conv_w[i],
        ffn_conv_b=ffn_conv_b[i][None],
        w_down=w_down[i].astype(BF16),
        norm_ple=norm_ple[i][None],
        w_ple_gate=w_ple_gate[i].astype(BF16),
        w_ple=w_ple[i].astype(BF16),
    )
    g_final = norm_final[None]
    yp, c1, s1, f1 = _layer(x_prompt, p_prompt[i], None, None, None, w, g_final)
    ys, c2, s2, f2 = _layer(x_sample, p_sample[i], state_conv_a[i], state_gla[i], state_ffn_conv[i],
                            w, g_final)
    return (yp, ys, c1[None], s1[None], f1[None], c2[None], s2[None], f2[None])
```

```python
import functools

import jax
import jax.numpy as jnp
from jax import lax
from jax.experimental import pallas as pl
from jax.experimental.pallas import tpu as pltpu

F32 = jnp.float32
BF16 = jnp.bfloat16

D_MODEL = 2048
CONV_W = 3
GLA_HEADS = 4
GLA_DK = D_MODEL // 2
GLA_DV = D_MODEL
GLA_HK = GLA_DK // GLA_HEADS
GLA_HV = GLA_DV // GLA_HEADS
GATE_RANK = 16
GATE_NORMALIZER = 16.0
D_FF = 5632
D_PLE = 256
EPS = 1e-6

D_MAIN = 3 * D_MODEL + 2 * GLA_DK + 2 * GLA_DV + 2 * D_MODEL
OFF_BA, OFF_CA, OFF_VA = 0, D_MODEL, 2 * D_MODEL
OFF_Q = 3 * D_MODEL
OFF_K = OFF_Q + GLA_DK
OFF_V = OFF_K + GLA_DK
OFF_GO = OFF_V + GLA_DV
OFF_MA = OFF_GO + GLA_DV
OFF_MB = OFF_MA + D_MODEL
ALR_OFF = OFF_MA
LANE = 128
BF16_ROWS = 16
VMEM_LIMIT = 56 * 1024 * 1024

NT_DIMS = (((1,), (1,)), ((), ()))
TN_DIMS = (((0,), (0,)), ((), ()))


def _rms_scale(x):
    return lax.rsqrt(jnp.mean(x * x, axis=-1, keepdims=True) + EPS)


def _sigmoid(x):
    return 1.0 / (1.0 + jnp.exp(-x))


def _log_sigmoid(z):
    return jnp.minimum(z, 0.0) - jnp.log(1.0 + jnp.exp(-jnp.abs(z)))


def _dot(a, b):
    return jnp.dot(a, b, preferred_element_type=F32)


def _split_bf16(x):
    hi = x.astype(BF16)
    return hi, (x - hi.astype(F32)).astype(BF16)


def _conv3(u, h0, h1, w):
    t = u.shape[0]
    rows = lax.broadcasted_iota(jnp.int32, (t, 1), 0)
    sh1 = jnp.where(rows == 0, h1, pltpu.roll(u, 1, axis=0))
    sh2 = jnp.where(rows == 0, h0, jnp.where(rows == 1, h1, pltpu.roll(u, 2, axis=0)))
    return w[0:1] * sh2 + w[1:2] * sh1 + w[2:3] * u


def _inproj_kernel(x_ref, g_ref, w_ref, walr_ref, wg2_ref, bg_ref, proj_ref, loga_ref, xn_ref,
                   *, nb, tm):
    j = pl.program_id(2)
    rows = nb * tm

    @pl.when(j == 0)
    def _():
        x = x_ref[...].reshape(rows, D_MODEL)
        xn = (x * _rms_scale(x) * g_ref[...]).astype(BF16)
        xn_ref[...] = xn
        a_lr = _dot(xn, walr_ref[...])
        z = _dot(a_lr.astype(BF16), wg2_ref[...]) + bg_ref[...]
        loga_ref[...] = (_log_sigmoid(z) * (1.0 / GATE_NORMALIZER)).reshape(nb, tm, GLA_DK)

    tn = proj_ref.shape[-1]
    proj_ref[...] = _dot(xn_ref[...], w_ref[...]).astype(BF16).reshape(nb, tm, tn)


def _inproj(x, g, w_main, w_alr, w_g2, b_gate, *, nb, tm, tn):
    bsz, t, _ = x.shape
    grid = (bsz // nb, t // tm, D_MAIN // tn)
    return pl.pallas_call(
        functools.partial(_inproj_kernel, nb=nb, tm=tm),
        grid=grid,
        in_specs=[
            pl.BlockSpec((nb, tm, D_MODEL), lambda b, i, j: (b, i, 0)),
            pl.BlockSpec((1, D_MODEL), lambda b, i, j: (0, 0)),
            pl.BlockSpec((D_MODEL, tn), lambda b, i, j: (0, j)),
            pl.BlockSpec((D_MODEL, LANE), lambda b, i, j: (0, 0)),
            pl.BlockSpec((LANE, GLA_DK), lambda b, i, j: (0, 0)),
            pl.BlockSpec((1, GLA_DK), lambda b, i, j: (0, 0)),
        ],
        out_specs=[
            pl.BlockSpec((nb, tm, tn), lambda b, i, j: (b, i, j)),
            pl.BlockSpec((nb, tm, GLA_DK), lambda b, i, j: (b, i, 0)),
        ],
        out_shape=[
            jax.ShapeDtypeStruct((bsz, t, D_MAIN), BF16),
            jax.ShapeDtypeStruct((bsz, t, GLA_DK), F32),
        ],
        scratch_shapes=[pltpu.VMEM((nb * tm, D_MODEL), BF16)],
        compiler_params=pltpu.CompilerParams(
            dimension_semantics=("parallel", "parallel", "arbitrary"),
            vmem_limit_bytes=VMEM_LIMIT),
        name="inproj",
    )(x, g, w_main, w_alr, w_g2, b_gate)


def _gla_head(q, k, v, la, s_old, *, diag):
    rows = q.shape[0]
    n_d = rows // diag
    r_i = lax.broadcasted_iota(jnp.int32, (rows, rows), 0)
    c_i = lax.broadcasted_iota(jnp.int32, (rows, rows), 1)
    tril = (r_i >= c_i).astype(BF16)
    la_hi, la_lo = _split_bf16(la)
    b = _dot(tril, la_hi) + _dot(tril, la_lo)
    b_end = b[rows - 1:rows, :]

    causal_d = (lax.broadcasted_iota(jnp.int32, (diag, diag), 0)
                >= lax.broadcasted_iota(jnp.int32, (diag, diag), 1))
    a_c, k_c, cen = [], [], []
    for d in range(n_d):
        lo = d * diag
        mid = lo + diag // 2
        c = b[mid - 1:mid, :]
        bd = b[lo:lo + diag, :]
        cen.append(c)
        a_c.append(q[lo:lo + diag, :] * jnp.exp(bd - c))
        k_c.append(k[lo:lo + diag, :] * jnp.exp(c - bd))

    def block(lo_d, hi_d):
        if hi_d - lo_d == 1:
            s = lax.dot_general(a_c[lo_d].astype(BF16), k_c[lo_d].astype(BF16), NT_DIMS,
                                preferred_element_type=F32)
            return jnp.where(causal_d, s, 0.0)
        mid_d = (lo_d + hi_d) // 2
        ref = b[mid_d * diag - 1:mid_d * diag, :]
        lhs = [a_c[d] * jnp.exp(cen[d] - ref) for d in range(mid_d, hi_d)]
        rhs = [k_c[d] * jnp.exp(ref - cen[d]) for d in range(lo_d, mid_d)]
        lhs = (lhs[0] if len(lhs) == 1 else jnp.concatenate(lhs, axis=0)).astype(BF16)
        rhs = (rhs[0] if len(rhs) == 1 else jnp.concatenate(rhs, axis=0)).astype(BF16)
        off = lax.dot_general(lhs, rhs, NT_DIMS, preferred_element_type=F32)
        top = jnp.concatenate([block(lo_d, mid_d), jnp.zeros_like(off)], axis=1)
        bot = jnp.concatenate([off, block(mid_d, hi_d)], axis=1)
        return jnp.concatenate([top, bot], axis=0)

    p = block(0, n_d).astype(BF16)
    qt = [a_c[d] * jnp.exp(cen[d]) for d in range(n_d)]
    kt = [k_c[d] * jnp.exp(b_end - cen[d]) for d in range(n_d)]
    qt = (qt[0] if n_d == 1 else jnp.concatenate(qt, axis=0)).astype(BF16)
    kt = (kt[0] if n_d == 1 else jnp.concatenate(kt, axis=0)).astype(BF16)
    o = _dot(qt, s_old.astype(BF16)) + _dot(p, v)
    e_rep = jnp.broadcast_to(b_end * (1.0 / BF16_ROWS), (BF16_ROWS, b_end.shape[1]))
    e_hi, e_lo = _split_bf16(e_rep)
    ones_col = jnp.ones((BF16_ROWS, LANE), BF16)
    dcol = (lax.dot_general(e_hi, ones_col, TN_DIMS, preferred_element_type=F32)
            + lax.dot_general(e_lo, ones_col, TN_DIMS, preferred_element_type=F32))
    decay = jnp.tile(jnp.exp(dcol), (1, v.shape[1] // LANE))
    s_new = s_old * decay + lax.dot_general(kt, v, TN_DIMS, preferred_element_type=F32)
    return o, s_new


def _gla_kernel(*refs, diag, has_state):
    if has_state:
        q_ref, k_ref, v_ref, go_ref, la_ref, gn_ref, s0_ref, o_ref, sout_ref, s_ref = refs
    else:
        q_ref, k_ref, v_ref, go_ref, la_ref, gn_ref, o_ref, sout_ref, s_ref = refs
    i = pl.program_id(1)

    @pl.when(i == 0)
    def _():
        if has_state:
            s_ref[...] = s0_ref[0]
        else:
            s_ref[...] = jnp.zeros_like(s_ref)

    for h in range(GLA_HEADS):
        ck = pl.ds(h * GLA_HK, GLA_HK)
        cv = pl.ds(h * GLA_HV, GLA_HV)
        q = q_ref[0, :, ck].astype(F32) * (GLA_HK ** -0.5)
        k = k_ref[0, :, ck].astype(F32)
        o, s_new = _gla_head(q, k, v_ref[0, :, cv], la_ref[0, :, ck], s_ref[h], diag=diag)
        s_ref[h] = s_new
        g = go_ref[0, :, cv].astype(F32)
        o_ref[0, :, cv] = (o * _rms_scale(o) * gn_ref[:, cv] * (g * _sigmoid(g))).astype(BF16)

    @pl.when(i == pl.num_programs(1) - 1)
    def _():
        sout_ref[0] = s_ref[...]


def _gla(proj, loga, gla_norm, state, *, tq, diag):
    bsz, t, _ = proj.shape
    has_state = state is not None
    grid = (bsz, t // tq)
    state_spec = pl.BlockSpec((1, GLA_HEADS, GLA_HK, GLA_HV), lambda b, i: (b, 0, 0, 0))
    in_specs = [
        pl.BlockSpec((1, tq, GLA_DK), lambda b, i: (b, i, OFF_Q // GLA_DK)),
        pl.BlockSpec((1, tq, GLA_DK), lambda b, i: (b, i, OFF_K // GLA_DK)),
        pl.BlockSpec((1, tq, GLA_DV), lambda b, i: (b, i, OFF_V // GLA_DV)),
        pl.BlockSpec((1, tq, GLA_DV), lambda b, i: (b, i, OFF_GO // GLA_DV)),
        pl.BlockSpec((1, tq, GLA_DK), lambda b, i: (b, i, 0)),
        pl.BlockSpec((1, GLA_DV), lambda b, i: (0, 0)),
    ]
    args = [proj, proj, proj, proj, loga, gla_norm]
    if has_state:
        in_specs.append(state_spec)
        args.append(state)
    return pl.pallas_call(
        functools.partial(_gla_kernel, diag=diag, has_state=has_state),
        grid=grid,
        in_specs=in_specs,
        out_specs=[
            pl.BlockSpec((1, tq, GLA_DV), lambda b, i: (b, i, 0)),
            state_spec,
        ],
        out_shape=[
            jax.ShapeDtypeStruct((bsz, t, GLA_DV), BF16),
            jax.ShapeDtypeStruct((bsz, GLA_HEADS, GLA_HK, GLA_HV), F32),
        ],
        scratch_shapes=[pltpu.VMEM((GLA_HEADS, GLA_HK, GLA_HV), F32)],
        compiler_params=pltpu.CompilerParams(
            dimension_semantics=("parallel", "arbitrary"),
            vmem_limit_bytes=VMEM_LIMIT),
        name="gla",
    )(*args)


def _mix_kernel(*refs, nb, tm, has_state):
    if has_state:
        (ba_ref, ca_ref, va_ref, ma_ref, mb_ref, og_ref, x_ref, cw_ref, st_ref,
         wa_ref, wb_ref, wo_ref, x1_ref, newc_ref, carry_ref) = refs
    else:
        (ba_ref, ca_ref, va_ref, ma_ref, mb_ref, og_ref, x_ref, cw_ref,
         wa_ref, wb_ref, wo_ref, x1_ref, newc_ref, carry_ref) = refs
    i = pl.program_id(1)
    rows = nb * tm
    first = i == 0
    cw = cw_ref[...]

    conv_parts = []
    for s in range(nb):
        cv = ca_ref[s].astype(F32) * va_ref[s].astype(F32)
        init = st_ref[s] if has_state else jnp.zeros((CONV_W - 1, D_MODEL), F32)
        halo = jnp.where(first, init, carry_ref[s, 0:2, :])
        conv_parts.append(_conv3(cv, halo[0:1], halo[1:2], cw))
        tail = cv[tm - 2:tm, :]
        carry_ref[s, 0:2, :] = tail
        newc_ref[s] = tail
    conv = conv_parts[0] if nb == 1 else jnp.concatenate(conv_parts, axis=0)

    b_a = ba_ref[...].reshape(rows, D_MODEL).astype(F32)
    y_a = _dot((b_a * conv).astype(BF16), wa_ref[...])
    y_b = _dot(og_ref[...].reshape(rows, D_MODEL), wb_ref[...])
    m_a = ma_ref[...].reshape(rows, D_MODEL).astype(F32)
    m_b = mb_ref[...].reshape(rows, D_MODEL).astype(F32)
    z = _sigmoid(m_a) * y_a + _sigmoid(m_b) * y_b
    x1 = x_ref[...].reshape(rows, D_MODEL) + _dot(z.astype(BF16), wo_ref[...])
    x1_ref[...] = x1.reshape(nb, tm, D_MODEL)


def _mix(proj, og, x, conv_w, state, w_a, w_b, w_o, *, nb, tm):
    bsz, t, _ = x.shape
    has_state = state is not None
    grid = (bsz // nb, t // tm)

    def col(off):
        return pl.BlockSpec((nb, tm, D_MODEL), lambda b, i: (b, i, off // D_MODEL))

    row_spec = pl.BlockSpec((nb, tm, D_MODEL), lambda b, i: (b, i, 0))
    w_spec = pl.BlockSpec((D_MODEL, D_MODEL), lambda b, i: (0, 0), pipeline_mode=pl.Buffered(1))
    in_specs = [col(OFF_BA), col(OFF_CA), col(OFF_VA), col(OFF_MA), col(OFF_MB), row_spec, row_spec,
                pl.BlockSpec((CONV_W, D_MODEL), lambda b, i: (0, 0))]
    args = [proj, proj, proj, proj, proj, og, x, conv_w]
    if has_state:
        in_specs.append(pl.BlockSpec((nb, CONV_W - 1, D_MODEL), lambda b, i: (b, 0, 0)))
        args.append(state)
    in_specs += [w_spec, w_spec, w_spec]
    args += [w_a, w_b, w_o]
    return pl.pallas_call(
        functools.partial(_mix_kernel, nb=nb, tm=tm, has_state=has_state),
        grid=grid,
        in_specs=in_specs,
        out_specs=[
            row_spec,
            pl.BlockSpec((nb, CONV_W - 1, D_MODEL), lambda b, i: (b, 0, 0)),
        ],
        out_shape=[
            jax.ShapeDtypeStruct((bsz, t, D_MODEL), F32),
            jax.ShapeDtypeStruct((bsz, CONV_W - 1, D_MODEL), F32),
        ],
        scratch_shapes=[pltpu.VMEM((nb, 8, D_MODEL), F32)],
        compiler_params=pltpu.CompilerParams(
            dimension_semantics=("parallel", "arbitrary"),
            vmem_limit_bytes=VMEM_LIMIT),
        name="mix",
    )(*args)


def _ffn_up_kernel(*refs, nb, tm, tn, has_state):
    if has_state:
        (x1_ref, g_ref, wv_ref, wg_ref, cwv_ref, cwg_ref, bv_ref, bg_ref, stv_ref, stg_ref,
         h_ref, nfv_ref, nfg_ref, hn_ref, cv_ref, cg_ref) = refs
    else:
        (x1_ref, g_ref, wv_ref, wg_ref, cwv_ref, cwg_ref, bv_ref, bg_ref,
         h_ref, nfv_ref, nfg_ref, hn_ref, cv_ref, cg_ref) = refs
    i = pl.program_id(1)
    j = pl.program_id(2)
    rows = nb * tm
    first = i == 0

    @pl.when(j == 0)
    def _():
        x1 = x1_ref[...].reshape(rows, D_MODEL)
        hn_ref[...] = (x1 * _rms_scale(x1) * g_ref[...]).astype(BF16)

    hn = hn_ref[...]
    uv = _dot(hn, wv_ref[...])
    ug = _dot(hn, wg_ref[...])
    cwv = cwv_ref[...]
    cwg = cwg_ref[...]
    for s in range(nb):
        uvs = uv[s * tm:(s + 1) * tm]
        ugs = ug[s * tm:(s + 1) * tm]
        zero = jnp.zeros((CONV_W - 1, tn), F32)
        halo_v = jnp.where(first, stv_ref[s] if has_state else zero, cv_ref[j, s, 0:2, :])
        halo_g = jnp.where(first, stg_ref[s] if has_state else zero, cg_ref[j, s, 0:2, :])
        val = _conv3(uvs, halo_v[0:1], halo_v[1:2], cwv) + bv_ref[...]
        gate = _conv3(ugs, halo_g[0:1], halo_g[1:2], cwg) + bg_ref[...]
        tail_v = uvs[tm - 2:tm, :]
        tail_g = ugs[tm - 2:tm, :]
        cv_ref[j, s, 0:2, :] = tail_v
        cg_ref[j, s, 0:2, :] = tail_g
        nfv_ref[s, 0] = tail_v
        nfg_ref[s, 0] = tail_g
        h_ref[s] = (gate * _sigmoid(gate) * val).astype(BF16)


def _ffn_up(x1, g, w_up, conv_w, conv_b, state, *, nb, tm, tn):
    bsz, t, _ = x1.shape
    has_state = state is not None
    n_ct = D_FF // tn
    grid = (bsz // nb, t // tm, n_ct)
    in_specs = [
        pl.BlockSpec((nb, tm, D_MODEL), lambda b, i, j: (b, i, 0)),
        pl.BlockSpec((1, D_MODEL), lambda b, i, j: (0, 0)),
        pl.BlockSpec((D_MODEL, tn), lambda b, i, j: (0, j)),
        pl.BlockSpec((D_MODEL, tn), lambda b, i, j: (0, n_ct + j)),
        pl.BlockSpec((CONV_W, tn), lambda b, i, j: (0, j)),
        pl.BlockSpec((CONV_W, tn), lambda b, i, j: (0, n_ct + j)),
        pl.BlockSpec((1, tn), lambda b, i, j: (0, j)),
        pl.BlockSpec((1, tn), lambda b, i, j: (0, n_ct + j)),
    ]
    args = [x1, g, w_up, w_up, conv_w, conv_w, conv_b, conv_b]
    if has_state:
        in_specs += [
            pl.BlockSpec((nb, CONV_W - 1, tn), lambda b, i, j: (b, 0, j)),
            pl.BlockSpec((nb, CONV_W - 1, tn), lambda b, i, j: (b, 0, n_ct + j)),
        ]
        args += [state, state]
    tail_spec = pl.BlockSpec((nb, 1, CONV_W - 1, tn), lambda b, i, j: (b, i, 0, j))
    return pl.pallas_call(
        functools.partial(_ffn_up_kernel, nb=nb, tm=tm, tn=tn, has_state=has_state),
        grid=grid,
        in_specs=in_specs,
        out_specs=[
            pl.BlockSpec((nb, tm, tn), lambda b, i, j: (b, i, j)),
            tail_spec,
            tail_spec,
        ],
        out_shape=[
            jax.ShapeDtypeStruct((bsz, t, D_FF), BF16),
            jax.ShapeDtypeStruct((bsz, t // tm, CONV_W - 1, D_FF), F32),
            jax.ShapeDtypeStruct((bsz, t // tm, CONV_W - 1, D_FF), F32),
        ],
        scratch_shapes=[
            pltpu.VMEM((nb * tm, D_MODEL), BF16),
            pltpu.VMEM((n_ct, nb, 8, tn), F32),
            pltpu.VMEM((n_ct, nb, 8, tn), F32),
        ],
        compiler_params=pltpu.CompilerParams(
            dimension_semantics=("parallel", "arbitrary", "arbitrary"),
            vmem_limit_bytes=VMEM_LIMIT),
        name="ffn_up",
    )(*args)


def _ffn_down_kernel(h_ref, x1_ref, p_ref, wd_ref, gp_ref, wpg_ref, wple_ref, gf_ref, y_ref):
    x2 = x1_ref[...] + _dot(h_ref[...], wd_ref[...])
    pn = (x2 * _rms_scale(x2) * gp_ref[...]).astype(BF16)
    gate = _sigmoid(_dot(pn, wpg_ref[...]))
    pe = _dot(p_ref[...].astype(BF16), wple_ref[...])
    x3 = x2 + gate * pe
    y_ref[...] = x3 * _rms_scale(x3) * gf_ref[...]


def _ffn_down(h, x1, p, w_down, g_ple, w_pg, w_ple, g_final, *, tm):
    m = x1.shape[0]
    const = dict(pipeline_mode=pl.Buffered(1))
    return pl.pallas_call(
        _ffn_down_kernel,
        grid=(m // tm,),
        in_specs=[
            pl.BlockSpec((tm, D_FF), lambda i: (i, 0)),
            pl.BlockSpec((tm, D_MODEL), lambda i: (i, 0)),
            pl.BlockSpec((tm, D_PLE), lambda i: (i, 0)),
            pl.BlockSpec((D_FF, D_MODEL), lambda i: (0, 0), **const),
            pl.BlockSpec((1, D_MODEL), lambda i: (0, 0)),
            pl.BlockSpec((D_MODEL, D_MODEL), lambda i: (0, 0), **const),
            pl.BlockSpec((D_PLE, D_MODEL), lambda i: (0, 0), **const),
            pl.BlockSpec((1, D_MODEL), lambda i: (0, 0)),
        ],
        out_specs=pl.BlockSpec((tm, D_MODEL), lambda i: (i, 0)),
        out_shape=jax.ShapeDtypeStruct((m, D_MODEL), F32),
        compiler_params=pltpu.CompilerParams(
            dimension_semantics=("parallel",),
            vmem_limit_bytes=VMEM_LIMIT),
        name="ffn_down",
    )(h, x1, p, w_down, g_ple, w_pg, w_ple, g_final)


def _tiles(bsz, t):
    if t >= 1024:
        return dict(nb=1, tm_in=1024, tm_mix=256, tm_up=1024, tq=256, diag=128)
    return dict(nb=bsz, tm_in=t, tm_mix=t, tm_up=t, tq=t, diag=t)


def _layer(x, p, conv_state, gla_state, ffn_state, w, g_final):
    bsz, t, _ = x.shape
    cfg = _tiles(bsz, t)
    nb = cfg["nb"]
    proj, loga = _inproj(x, w["norm_mix"], w["w_main"], w["w_alr"], w["w_g2"], w["b_gate"],
                         nb=nb, tm=cfg["tm_in"], tn=1024)
    og, new_s = _gla(proj, loga, w["gla_norm"], gla_state, tq=cfg["tq"], diag=cfg["diag"])
    x1, new_conv = _mix(proj, og, x, w["conv_a_w"], conv_state, w["w_a_out"], w["w_b_out"], w["w_o"],
                        nb=nb, tm=cfg["tm_mix"])
    h, nf_v, nf_g = _ffn_up(x1, w["norm_ffn"], w["w_up"], w["ffn_conv_w"], w["ffn_conv_b"], ffn_state,
                            nb=nb, tm=cfg["tm_up"], tn=512)
    m = bsz * t
    y = _ffn_down(h.reshape(m, D_FF), x1.reshape(m, D_MODEL), p.reshape(m, D_PLE),
                  w["w_down"], w["norm_ple"], w["w_ple_gate"], w["w_ple"], g_final, tm=256)
    new_ffn = jnp.concatenate([nf_v[:, -1], nf_g[:, -1]], axis=-1)
    return y.reshape(bsz, t, D_MODEL), new_conv, new_s, new_ffn


def kernel(x_prompt, x_sample, p_prompt, p_sample, state_conv_a, state_gla, state_ffn_conv, norm_mix, w_in, conv_a_w, w_a_out, w_gate2, b_gate, gla_norm, w_b_out, w_o, norm_ffn, w_up, ffn_conv_w, ffn_conv_b, w_down, norm_ple, w_ple_gate, w_ple, norm_final):
    depth = w_in.shape[0]
    assert depth == 1, "the final norm is fused into the last layer; one layer supported"
    i = 0
    w_in_i = w_in[i]
    w_alr = jnp.pad(w_in_i[:, ALR_OFF:ALR_OFF + GATE_RANK], ((0, 0), (0, LANE - GATE_RANK)))
    w = dict(
        norm_mix=norm_mix[i][None],
        w_main=jnp.concatenate([w_in_i[:, :ALR_OFF].astype(BF16),
                                w_in_i[:, ALR_OFF + GATE_RANK:].astype(BF16)], axis=1),
        w_alr=w_alr.astype(BF16),
        w_g2=jnp.pad(w_gate2[i], ((0, LANE - GATE_RANK), (0, 0))).astype(BF16),
        b_gate=b_gate[i][None],
        gla_norm=gla_norm[i][None],
        conv_a_w=conv_a_w[i],
        w_a_out=w_a_out[i].astype(BF16),
        w_b_out=w_b_out[i].astype(BF16),
        w_o=w_o[i].astype(BF16),
        norm_ffn=norm_ffn[i][None],
        w_up=w_up[i].astype(BF16),
        ffn_conv_w=ffn_conv_w[i],
        ffn_conv_b=ffn_conv_b[i][None],
        w_down=w_down[i].astype(BF16),
        norm_ple=norm_ple[i][None],
        w_ple_gate=w_ple_gate[i].astype(BF16),
        w_ple=w_ple[i].astype(BF16),
    )
    g_final = norm_final[None]
    yp, c1, s1, f1 = _layer(x_prompt, p_prompt[i], None, None, None, w, g_final)
    ys, c2, s2, f2 = _layer(x_sample, p_sample[i], state_conv_a[i], state_gla[i], state_ffn_conv[i],
                            w, g_final)
    return (yp, ys, c1[None], s1[None], f1[None], c2[None], s2[None], f2[None])
```

```python
import functools

import jax
import jax.numpy as jnp
from jax import lax
from jax.experimental import pallas as pl
from jax.experimental.pallas import tpu as pltpu

F32 = jnp.float32
BF16 = jnp.bfloat16

D_MODEL = 2048
CONV_W = 3
GLA_HEADS = 4
GLA_DK = D_MODEL // 2
GLA_DV = D_MODEL
GLA_HK = GLA_DK // GLA_HEADS
GLA_HV = GLA_DV // GLA_HEADS
GATE_RANK = 16
GATE_NORMALIZER = 16.0
D_FF = 5632
D_PLE = 256
EPS = 1e-6

D_MAIN = 3 * D_MODEL + 2 * GLA_DK + 2 * GLA_DV + 2 * D_MODEL
OFF_BA, OFF_CA, OFF_VA = 0, D_MODEL, 2 * D_MODEL
OFF_Q = 3 * D_MODEL
OFF_K = OFF_Q + GLA_DK
OFF_V = OFF_K + GLA_DK
OFF_GO = OFF_V + GLA_DV
OFF_MA = OFF_GO + GLA_DV
OFF_MB = OFF_MA + D_MODEL
ALR_OFF = OFF_MA
LANE = 128
SUBLANES = 8
BF16_ROWS = 16
VMEM_LIMIT = 56 * 1024 * 1024

NT_DIMS = (((1,), (1,)), ((), ()))
TN_DIMS = (((0,), (0,)), ((), ()))


def _rms_scale(x):
    return lax.rsqrt(jnp.mean(x * x, axis=-1, keepdims=True) + EPS)


def _sigmoid(x):
    return 1.0 / (1.0 + jnp.exp(-x))


def _log_sigmoid(z):
    return jnp.minimum(z, 0.0) - jnp.log(1.0 + jnp.exp(-jnp.abs(z)))


def _dot(a, b):
    return jnp.dot(a, b, preferred_element_type=F32)


def _split_bf16(x):
    hi = x.astype(BF16)
    return hi, (x - hi.astype(F32)).astype(BF16)


def _conv3(u, h0, h1, w):
    r1 = pltpu.roll(u, 1, axis=0)
    r2 = pltpu.roll(u, 2, axis=0)
    rows = lax.broadcasted_iota(jnp.int32, (SUBLANES, 1), 0)
    top1 = jnp.where(rows == 0, h1, r1[:SUBLANES])
    top2 = jnp.where(rows == 0, h0, jnp.where(rows == 1, h1, r2[:SUBLANES]))
    sh1 = jnp.concatenate([top1, r1[SUBLANES:]], axis=0)
    sh2 = jnp.concatenate([top2, r2[SUBLANES:]], axis=0)
    return w[0:1] * sh2 + w[1:2] * sh1 + w[2:3] * u


def _inproj_kernel(x_ref, g_ref, w_ref, walr_ref, wg2_ref, bg_ref, proj_ref, loga_ref, xn_ref,
                   *, nb, tm):
    j = pl.program_id(2)
    rows = nb * tm

    @pl.when(j == 0)
    def _():
        x = x_ref[...].reshape(rows, D_MODEL)
        xn = (x * _rms_scale(x) * g_ref[...]).astype(BF16)
        xn_ref[...] = xn
        a_lr = _dot(xn, walr_ref[...])
        z = _dot(a_lr.astype(BF16), wg2_ref[...]) + bg_ref[...]
        loga_ref[...] = (_log_sigmoid(z) * (1.0 / GATE_NORMALIZER)).reshape(nb, tm, GLA_DK)

    tn = proj_ref.shape[-1]
    proj_ref[...] = _dot(xn_ref[...], w_ref[...]).astype(BF16).reshape(nb, tm, tn)


def _inproj(x, g, w_main, w_alr, w_g2, b_gate, *, nb, tm, tn):
    bsz, t, _ = x.shape
    grid = (bsz // nb, t // tm, D_MAIN // tn)
    return pl.pallas_call(
        functools.partial(_inproj_kernel, nb=nb, tm=tm),
        grid=grid,
        in_specs=[
            pl.BlockSpec((nb, tm, D_MODEL), lambda b, i, j: (b, i, 0)),
            pl.BlockSpec((1, D_MODEL), lambda b, i, j: (0, 0)),
            pl.BlockSpec((D_MODEL, tn), lambda b, i, j: (0, j)),
            pl.BlockSpec((D_MODEL, LANE), lambda b, i, j: (0, 0)),
            pl.BlockSpec((LANE, GLA_DK), lambda b, i, j: (0, 0)),
            pl.BlockSpec((1, GLA_DK), lambda b, i, j: (0, 0)),
        ],
        out_specs=[
            pl.BlockSpec((nb, tm, tn), lambda b, i, j: (b, i, j)),
            pl.BlockSpec((nb, tm, GLA_DK), lambda b, i, j: (b, i, 0)),
        ],
        out_shape=[
            jax.ShapeDtypeStruct((bsz, t, D_MAIN), BF16),
            jax.ShapeDtypeStruct((bsz, t, GLA_DK), F32),
        ],
        scratch_shapes=[pltpu.VMEM((nb * tm, D_MODEL), BF16)],
        compiler_params=pltpu.CompilerParams(
            dimension_semantics=("parallel", "parallel", "arbitrary"),
            vmem_limit_bytes=VMEM_LIMIT),
        name="inproj",
    )(x, g, w_main, w_alr, w_g2, b_gate)


def _gla_head(q, k, v, la, s_old, *, diag):
    rows = q.shape[0]
    n_d = rows // diag
    r_i = lax.broadcasted_iota(jnp.int32, (rows, rows), 0)
    c_i = lax.broadcasted_iota(jnp.int32, (rows, rows), 1)
    tril = (r_i >= c_i).astype(BF16)
    la_hi, la_lo = _split_bf16(la)
    b = _dot(tril, la_hi) + _dot(tril, la_lo)
    b_end = b[rows - 1:rows, :]

    causal_d = (lax.broadcasted_iota(jnp.int32, (diag, diag), 0)
                >= lax.broadcasted_iota(jnp.int32, (diag, diag), 1))
    a_c, k_c, cen = [], [], []
    for d in range(n_d):
        lo = d * diag
        mid = lo + diag // 2
        c = b[mid - 1:mid, :]
        bd = b[lo:lo + diag, :]
        cen.append(c)
        a_c.append(q[lo:lo + diag, :] * jnp.exp(bd - c))
        k_c.append(k[lo:lo + diag, :] * jnp.exp(c - bd))

    def block(lo_d, hi_d):
        if hi_d - lo_d == 1:
            s = lax.dot_general(a_c[lo_d].astype(BF16), k_c[lo_d].astype(BF16), NT_DIMS,
                                preferred_element_type=F32)
            return jnp.where(causal_d, s, 0.0)
        mid_d = (lo_d + hi_d) // 2
        ref = b[mid_d * diag - 1:mid_d * diag, :]
        lhs = [a_c[d] * jnp.exp(cen[d] - ref) for d in range(mid_d, hi_d)]
        rhs = [k_c[d] * jnp.exp(ref - cen[d]) for d in range(lo_d, mid_d)]
        lhs = (lhs[0] if len(lhs) == 1 else jnp.concatenate(lhs, axis=0)).astype(BF16)
        rhs = (rhs[0] if len(rhs) == 1 else jnp.concatenate(rhs, axis=0)).astype(BF16)
        off = lax.dot_general(lhs, rhs, NT_DIMS, preferred_element_type=F32)
        top = jnp.concatenate([block(lo_d, mid_d), jnp.zeros_like(off)], axis=1)
        bot = jnp.concatenate([off, block(mid_d, hi_d)], axis=1)
        return jnp.concatenate([top, bot], axis=0)

    p = block(0, n_d).astype(BF16)
    qt = [a_c[d] * jnp.exp(cen[d]) for d in range(n_d)]
    kt = [k_c[d] * jnp.exp(b_end - cen[d]) for d in range(n_d)]
    qt = (qt[0] if n_d == 1 else jnp.concatenate(qt, axis=0)).astype(BF16)
    kt = (kt[0] if n_d == 1 else jnp.concatenate(kt, axis=0)).astype(BF16)
    o = _dot(qt, s_old.astype(BF16)) + _dot(p, v)
    e_rep = jnp.broadcast_to(b_end * (1.0 / BF16_ROWS), (BF16_ROWS, b_end.shape[1]))
    e_hi, e_lo = _split_bf16(e_rep)
    ones_col = jnp.ones((BF16_ROWS, LANE), BF16)
    dcol = (lax.dot_general(e_hi, ones_col, TN_DIMS, preferred_element_type=F32)
            + lax.dot_general(e_lo, ones_col, TN_DIMS, preferred_element_type=F32))
    decay = jnp.tile(jnp.exp(dcol), (1, v.shape[1] // LANE))
    s_new = s_old * decay + lax.dot_general(kt, v, TN_DIMS, preferred_element_type=F32)
    return o, s_new


def _gla_kernel(*refs, diag, has_state):
    if has_state:
        q_ref, k_ref, v_ref, go_ref, la_ref, gn_ref, s0_ref, o_ref, sout_ref, s_ref = refs
    else:
        q_ref, k_ref, v_ref, go_ref, la_ref, gn_ref, o_ref, sout_ref, s_ref = refs
    i = pl.program_id(1)

    @pl.when(i == 0)
    def _():
        if has_state:
            s_ref[...] = s0_ref[0]
        else:
            s_ref[...] = jnp.zeros_like(s_ref)

    for h in range(GLA_HEADS):
        ck = pl.ds(h * GLA_HK, GLA_HK)
        cv = pl.ds(h * GLA_HV, GLA_HV)
        q = q_ref[0, :, ck].astype(F32) * (GLA_HK ** -0.5)
        k = k_ref[0, :, ck].astype(F32)
        o, s_new = _gla_head(q, k, v_ref[0, :, cv], la_ref[0, :, ck], s_ref[h], diag=diag)
        s_ref[h] = s_new
        g = go_ref[0, :, cv].astype(F32)
        o_ref[0, :, cv] = (o * _rms_scale(o) * gn_ref[:, cv] * (g * _sigmoid(g))).astype(BF16)

    @pl.when(i == pl.num_programs(1) - 1)
    def _():
        sout_ref[0] = s_ref[...]


def _gla(proj, loga, gla_norm, state, *, tq, diag):
    bsz, t, _ = proj.shape
    has_state = state is not None
    grid = (bsz, t // tq)
    state_spec = pl.BlockSpec((1, GLA_HEADS, GLA_HK, GLA_HV), lambda b, i: (b, 0, 0, 0))
    in_specs = [
        pl.BlockSpec((1, tq, GLA_DK), lambda b, i: (b, i, OFF_Q // GLA_DK)),
        pl.BlockSpec((1, tq, GLA_DK), lambda b, i: (b, i, OFF_K // GLA_DK)),
        pl.BlockSpec((1, tq, GLA_DV), lambda b, i: (b, i, OFF_V // GLA_DV)),
        pl.BlockSpec((1, tq, GLA_DV), lambda b, i: (b, i, OFF_GO // GLA_DV)),
        pl.BlockSpec((1, tq, GLA_DK), lambda b, i: (b, i, 0)),
        pl.BlockSpec((1, GLA_DV), lambda b, i: (0, 0)),
    ]
    args = [proj, proj, proj, proj, loga, gla_norm]
    if has_state:
        in_specs.append(state_spec)
        args.append(state)
    return pl.pallas_call(
        functools.partial(_gla_kernel, diag=diag, has_state=has_state),
        grid=grid,
        in_specs=in_specs,
        out_specs=[
            pl.BlockSpec((1, tq, GLA_DV), lambda b, i: (b, i, 0)),
            state_spec,
        ],
        out_shape=[
            jax.ShapeDtypeStruct((bsz, t, GLA_DV), BF16),
            jax.ShapeDtypeStruct((bsz, GLA_HEADS, GLA_HK, GLA_HV), F32),
        ],
        scratch_shapes=[pltpu.VMEM((GLA_HEADS, GLA_HK, GLA_HV), F32)],
        compiler_params=pltpu.CompilerParams(
            dimension_semantics=("parallel", "arbitrary"),
            vmem_limit_bytes=VMEM_LIMIT),
        name="gla",
    )(*args)


def _mix_kernel(*refs, nb, tm, has_state):
    if has_state:
        (ba_ref, ca_ref, va_ref, ma_ref, mb_ref, og_ref, x_ref, cw_ref, st_ref,
         wa_ref, wb_ref, wo_ref, x1_ref, newc_ref, carry_ref) = refs
    else:
        (ba_ref, ca_ref, va_ref, ma_ref, mb_ref, og_ref, x_ref, cw_ref,
         wa_ref, wb_ref, wo_ref, x1_ref, newc_ref, carry_ref) = refs
    i = pl.program_id(1)
    rows = nb * tm
    first = i == 0
    cw = cw_ref[...]

    conv_parts = []
    for s in range(nb):
        cv = ca_ref[s].astype(F32) * va_ref[s].astype(F32)
        init = st_ref[s] if has_state else jnp.zeros((CONV_W - 1, D_MODEL), F32)
        halo = jnp.where(first, init, carry_ref[s, 0:2, :])
        conv_parts.append(_conv3(cv, halo[0:1], halo[1:2], cw))
        tail = cv[tm - 2:tm, :]
        carry_ref[s, 0:2, :] = tail
        newc_ref[s] = tail
    conv = conv_parts[0] if nb == 1 else jnp.concatenate(conv_parts, axis=0)

    b_a = ba_ref[...].reshape(rows, D_MODEL).astype(F32)
    y_a = _dot((b_a * conv).astype(BF16), wa_ref[...])
    y_b = _dot(og_ref[...].reshape(rows, D_MODEL), wb_ref[...])
    m_a = ma_ref[...].reshape(rows, D_MODEL).astype(F32)
    m_b = mb_ref[...].reshape(rows, D_MODEL).astype(F32)
    z = _sigmoid(m_a) * y_a + _sigmoid(m_b) * y_b
    x1 = x_ref[...].reshape(rows, D_MODEL) + _dot(z.astype(BF16), wo_ref[...])
    x1_ref[...] = x1.reshape(nb, tm, D_MODEL)


def _mix(proj, og, x, conv_w, state, w_a, w_b, w_o, *, nb, tm):
    bsz, t, _ = x.shape
    has_state = state is not None
    grid = (bsz // nb, t // tm)

    def col(off):
        return pl.BlockSpec((nb, tm, D_MODEL), lambda b, i: (b, i, off // D_MODEL))

    row_spec = pl.BlockSpec((nb, tm, D_MODEL), lambda b, i: (b, i, 0))
    w_spec = pl.BlockSpec((D_MODEL, D_MODEL), lambda b, i: (0, 0), pipeline_mode=pl.Buffered(1))
    in_specs = [col(OFF_BA), col(OFF_CA), col(OFF_VA), col(OFF_MA), col(OFF_MB), row_spec, row_spec,
                pl.BlockSpec((CONV_W, D_MODEL), lambda b, i: (0, 0))]
    args = [proj, proj, proj, proj, proj, og, x, conv_w]
    if has_state:
        in_specs.append(pl.BlockSpec((nb, CONV_W - 1, D_MODEL), lambda b, i: (b, 0, 0)))
        args.append(state)
    in_specs += [w_spec, w_spec, w_spec]
    args += [w_a, w_b, w_o]
    return pl.pallas_call(
        functools.partial(_mix_kernel, nb=nb, tm=tm, has_state=has_state),
        grid=grid,
        in_specs=in_specs,
        out_specs=[
            row_spec,
            pl.BlockSpec((nb, CONV_W - 1, D_MODEL), lambda b, i: (b, 0, 0)),
        ],
        out_shape=[
            jax.ShapeDtypeStruct((bsz, t, D_MODEL), F32),
            jax.ShapeDtypeStruct((bsz, CONV_W - 1, D_MODEL), F32),
        ],
        scratch_shapes=[pltpu.VMEM((nb, 8, D_MODEL), F32)],
        compiler_params=pltpu.CompilerParams(
            dimension_semantics=("parallel", "arbitrary"),
            vmem_limit_bytes=VMEM_LIMIT),
        name="mix",
    )(*args)


def _ffn_up_kernel(*refs, nb, tm, tn, has_state):
    if has_state:
        (x1_ref, g_ref, wv_ref, wg_ref, cwv_ref, cwg_ref, bv_ref, bg_ref, stv_ref, stg_ref,
         h_ref, nfv_ref, nfg_ref, hn_ref, cv_ref, cg_ref) = refs
    else:
        (x1_ref, g_ref, wv_ref, wg_ref, cwv_ref, cwg_ref, bv_ref, bg_ref,
         h_ref, nfv_ref, nfg_ref, hn_ref, cv_ref, cg_ref) = refs
    i = pl.program_id(1)
    j = pl.program_id(2)
    rows = nb * tm
    first = i == 0

    @pl.when(j == 0)
    def _():
        x1 = x1_ref[...].reshape(rows, D_MODEL)
        hn_ref[...] = (x1 * _rms_scale(x1) * g_ref[...]).astype(BF16)

    hn = hn_ref[...]
    ug = _dot(hn, wg_ref[...])
    uv = _dot(hn, wv_ref[...])
    cwv = cwv_ref[...]
    cwg = cwg_ref[...]
    for s in range(nb):
        uvs = uv[s * tm:(s + 1) * tm]
        ugs = ug[s * tm:(s + 1) * tm]
        zero = jnp.zeros((CONV_W - 1, tn), F32)
        halo_g = jnp.where(first, stg_ref[s] if has_state else zero, cg_ref[j, s, 0:2, :])
        gate = _conv3(ugs, halo_g[0:1], halo_g[1:2], cwg) + bg_ref[...]
        act = gate * _sigmoid(gate)
        halo_v = jnp.where(first, stv_ref[s] if has_state else zero, cv_ref[j, s, 0:2, :])
        val = _conv3(uvs, halo_v[0:1], halo_v[1:2], cwv) + bv_ref[...]
        tail_v = uvs[tm - 2:tm, :]
        tail_g = ugs[tm - 2:tm, :]
        cv_ref[j, s, 0:2, :] = tail_v
        cg_ref[j, s, 0:2, :] = tail_g
        nfv_ref[s, 0] = tail_v
        nfg_ref[s, 0] = tail_g
        h_ref[s] = (act * val).astype(BF16)


def _ffn_up(x1, g, w_up, conv_w, conv_b, state, *, nb, tm, tn):
    bsz, t, _ = x1.shape
    has_state = state is not None
    n_ct = D_FF // tn
    grid = (bsz // nb, t // tm, n_ct)
    in_specs = [
        pl.BlockSpec((nb, tm, D_MODEL), lambda b, i, j: (b, i, 0)),
        pl.BlockSpec((1, D_MODEL), lambda b, i, j: (0, 0)),
        pl.BlockSpec((D_MODEL, tn), lambda b, i, j: (0, j)),
        pl.BlockSpec((D_MODEL, tn), lambda b, i, j: (0, n_ct + j)),
        pl.BlockSpec((CONV_W, tn), lambda b, i, j: (0, j)),
        pl.BlockSpec((CONV_W, tn), lambda b, i, j: (0, n_ct + j)),
        pl.BlockSpec((1, tn), lambda b, i, j: (0, j)),
        pl.BlockSpec((1, tn), lambda b, i, j: (0, n_ct + j)),
    ]
    args = [x1, g, w_up, w_up, conv_w, conv_w, conv_b, conv_b]
    if has_state:
        in_specs += [
            pl.BlockSpec((nb, CONV_W - 1, tn), lambda b, i, j: (b, 0, j)),
            pl.BlockSpec((nb, CONV_W - 1, tn), lambda b, i, j: (b, 0, n_ct + j)),
        ]
        args += [state, state]
    tail_spec = pl.BlockSpec((nb, 1, CONV_W - 1, tn), lambda b, i, j: (b, i, 0, j))
    return pl.pallas_call(
        functools.partial(_ffn_up_kernel, nb=nb, tm=tm, tn=tn, has_state=has_state),
        grid=grid,
        in_specs=in_specs,
        out_specs=[
            pl.BlockSpec((nb, tm, tn), lambda b, i, j: (b, i, j)),
            tail_spec,
            tail_spec,
        ],
        out_shape=[
            jax.ShapeDtypeStruct((bsz, t, D_FF), BF16),
            jax.ShapeDtypeStruct((bsz, t // tm, CONV_W - 1, D_FF), F32),
            jax.ShapeDtypeStruct((bsz, t // tm, CONV_W - 1, D_FF), F32),
        ],
        scratch_shapes=[
            pltpu.VMEM((nb * tm, D_MODEL), BF16),
            pltpu.VMEM((n_ct, nb, 8, tn), F32),
            pltpu.VMEM((n_ct, nb, 8, tn), F32),
        ],
        compiler_params=pltpu.CompilerParams(
            dimension_semantics=("parallel", "arbitrary", "arbitrary"),
            vmem_limit_bytes=VMEM_LIMIT),
        name="ffn_up",
    )(*args)


def _ffn_down_kernel(h_ref, x1_ref, p_ref, wd_ref, gp_ref, wpg_ref, wple_ref, gf_ref, y_ref):
    x2 = x1_ref[...] + _dot(h_ref[...], wd_ref[...])
    pn = (x2 * _rms_scale(x2) * gp_ref[...]).astype(BF16)
    gate = _sigmoid(_dot(pn, wpg_ref[...]))
    pe = _dot(p_ref[...].astype(BF16), wple_ref[...])
    x3 = x2 + gate * pe
    y_ref[...] = x3 * _rms_scale(x3) * gf_ref[...]


def _ffn_down(h, x1, p, w_down, g_ple, w_pg, w_ple, g_final, *, tm):
    m = x1.shape[0]
    const = dict(pipeline_mode=pl.Buffered(1))
    return pl.pallas_call(
        _ffn_down_kernel,
        grid=(m // tm,),
        in_specs=[
            pl.BlockSpec((tm, D_FF), lambda i: (i, 0)),
            pl.BlockSpec((tm, D_MODEL), lambda i: (i, 0)),
            pl.BlockSpec((tm, D_PLE), lambda i: (i, 0)),
            pl.BlockSpec((D_FF, D_MODEL), lambda i: (0, 0), **const),
            pl.BlockSpec((1, D_MODEL), lambda i: (0, 0)),
            pl.BlockSpec((D_MODEL, D_MODEL), lambda i: (0, 0), **const),
            pl.BlockSpec((D_PLE, D_MODEL), lambda i: (0, 0), **const),
            pl.BlockSpec((1, D_MODEL), lambda i: (0, 0)),
        ],
        out_specs=pl.BlockSpec((tm, D_MODEL), lambda i: (i, 0)),
        out_shape=jax.ShapeDtypeStruct((m, D_MODEL), F32),
        compiler_params=pltpu.CompilerParams(
            dimension_semantics=("parallel",),
            vmem_limit_bytes=VMEM_LIMIT),
        name="ffn_down",
    )(h, x1, p, w_down, g_ple, w_pg, w_ple, g_final)


def _tiles(bsz, t):
    if t >= 1024:
        return dict(nb=1, tm_in=1024, tm_mix=256, tm_up=1024, tq=256, diag=128)
    return dict(nb=bsz, tm_in=t, tm_mix=t, tm_up=t, tq=t, diag=t)


def _layer(x, p, conv_state, gla_state, ffn_state, w, g_final):
    bsz, t, _ = x.shape
    cfg = _tiles(bsz, t)
    nb = cfg["nb"]
    proj, loga = _inproj(x, w["norm_mix"], w["w_main"], w["w_alr"], w["w_g2"], w["b_gate"],
                         nb=nb, tm=cfg["tm_in"], tn=1024)
    og, new_s = _gla(proj, loga, w["gla_norm"], gla_state, tq=cfg["tq"], diag=cfg["diag"])
    x1, new_conv = _mix(proj, og, x, w["conv_a_w"], conv_state, w["w_a_out"], w["w_b_out"], w["w_o"],
                        nb=nb, tm=cfg["tm_mix"])
    h, nf_v, nf_g = _ffn_up(x1, w["norm_ffn"], w["w_up"], w["ffn_conv_w"], w["ffn_conv_b"], ffn_state,
                            nb=nb, tm=cfg["tm_up"], tn=512)
    m = bsz * t
    y = _ffn_down(h.reshape(m, D_FF), x1.reshape(m, D_MODEL), p.reshape(m, D_PLE),
                  w["w_down"], w["norm_ple"], w["w_ple_gate"], w["w_ple"], g_final, tm=256)
    new_ffn = jnp.concatenate([nf_v[:, -1], nf_g[:, -1]], axis=-1)
    return y.reshape(bsz, t, D_MODEL), new_conv, new_s, new_ffn


def kernel(x_prompt, x_sample, p_prompt, p_sample, state_conv_a, state_gla, state_ffn_conv, norm_mix, w_in, conv_a_w, w_a_out, w_gate2, b_gate, gla_norm, w_b_out, w_o, norm_ffn, w_up, ffn_conv_w, ffn_conv_b, w_down, norm_ple, w_ple_gate, w_ple, norm_final):
    depth = w_in.shape[0]
    assert depth == 1, "the final norm is fused into the last layer; one layer supported"
    i = 0
    w_in_i = w_in[i]
    w_alr = jnp.pad(w_in_i[:, ALR_OFF:ALR_OFF + GATE_RANK], ((0, 0), (0, LANE - GATE_RANK)))
    w = dict(
        norm_mix=norm_mix[i][None],
        w_main=jnp.concatenate([w_in_i[:, :ALR_OFF].astype(BF16),
                                w_in_i[:, ALR_OFF + GATE_RANK:].astype(BF16)], axis=1),
        w_alr=w_alr.astype(BF16),
        w_g2=jnp.pad(w_gate2[i], ((0, LANE - GATE_RANK), (0, 0))).astype(BF16),
        b_gate=b_gate[i][None],
        gla_norm=gla_norm[i][None],
        conv_a_w=conv_a_w[i],
        w_a_out=w_a_out[i].astype(BF16),
        w_b_out=w_b_out[i].astype(BF16),
        w_o=w_o[i].astype(BF16),
        norm_ffn=norm_ffn[i][None],
        w_up=w_up[i].astype(BF16),
        ffn_conv_w=ffn_conv_w[i],
        ffn_conv_b=ffn_conv_b[i][None],
        w_down=w_down[i].astype(BF16),
        norm_ple=norm_ple[i][None],
        w_ple_gate=w_ple_gate[i].astype(BF16),
        w_ple=w_ple[i].astype(BF16),
    )
    g_final = norm_final[None]
    yp, c1, s1, f1 = _layer(x_prompt, p_prompt[i], None, None, None, w, g_final)
    ys, c2, s2, f2 = _layer(x_sample, p_sample[i], state_conv_a[i], state_gla[i], state_ffn_conv[i],
                            w, g_final)
    return (yp, ys, c1[None], s1[None], f1[None], c2[None], s2[None], f2[None])
```

```python
import functools
import math

import jax
import jax.numpy as jnp
from jax import lax
from jax.experimental import pallas as pl
from jax.experimental.pallas import tpu as pltpu

F32 = jnp.float32
BF16 = jnp.bfloat16

D_MODEL = 2048
CONV_W = 3
GLA_HEADS = 4
GLA_DK = D_MODEL // 2
GLA_DV = D_MODEL
GLA_HK = GLA_DK // GLA_HEADS
GLA_HV = GLA_DV // GLA_HEADS
GATE_RANK = 16
GATE_NORMALIZER = 16.0
D_FF = 5632
D_PLE = 256
EPS = 1e-6

D_MAIN = 3 * D_MODEL + 2 * GLA_DK + 2 * GLA_DV + 2 * D_MODEL
OFF_BA, OFF_CA, OFF_VA = 0, D_MODEL, 2 * D_MODEL
OFF_Q = 3 * D_MODEL
OFF_K = OFF_Q + GLA_DK
OFF_V = OFF_K + GLA_DK
OFF_GO = OFF_V + GLA_DV
OFF_MA = OFF_GO + GLA_DV
OFF_MB = OFF_MA + D_MODEL
ALR_OFF = OFF_MA
LANE = 128
SUBLANES = 8
ROW_PARTS = 4
BF16_ROWS = 16
VMEM_LIMIT = 56 * 1024 * 1024

LOG2_E = math.log2(math.e)
Q_LOG2_SCALE = -0.5 * math.log2(GLA_HK)

NT_DIMS = (((1,), (1,)), ((), ()))
TN_DIMS = (((0,), (0,)), ((), ()))


def _rms_scale(x):
    return lax.rsqrt(jnp.mean(x * x, axis=-1, keepdims=True) + EPS)


def _sigmoid(x):
    return 1.0 / (1.0 + jnp.exp(-x))


def _log_sigmoid(z):
    return jnp.minimum(z, 0.0) - jnp.log(1.0 + jnp.exp(-jnp.abs(z)))


def _dot(a, b):
    return jnp.dot(a, b, preferred_element_type=F32)


def _split_bf16(x):
    hi = x.astype(BF16)
    return hi, (x - hi.astype(F32)).astype(BF16)


def _conv3(u, h0, h1, w):
    r1 = pltpu.roll(u, 1, axis=0)
    r2 = pltpu.roll(u, 2, axis=0)
    rows = lax.broadcasted_iota(jnp.int32, (SUBLANES, 1), 0)
    top1 = jnp.where(rows == 0, h1, r1[:SUBLANES])
    top2 = jnp.where(rows == 0, h0, jnp.where(rows == 1, h1, r2[:SUBLANES]))
    sh1 = jnp.concatenate([top1, r1[SUBLANES:]], axis=0)
    sh2 = jnp.concatenate([top2, r2[SUBLANES:]], axis=0)
    return w[0:1] * sh2 + w[1:2] * sh1 + w[2:3] * u


def _inproj_kernel(x_ref, g_ref, whead_ref, wtail_ref, walr_ref, wg2_ref, bg_ref, proj_ref, loga_ref,
                   xn_ref, *, nb, tm, n_head):
    j = pl.program_id(2)
    rows = nb * tm

    @pl.when(j == 0)
    def _():
        x = x_ref[...].reshape(rows, D_MODEL)
        xn = (x * _rms_scale(x) * g_ref[...]).astype(BF16)
        xn_ref[...] = xn
        a_lr = _dot(xn, walr_ref[...])
        z = _dot(a_lr.astype(BF16), wg2_ref[...]) + bg_ref[...]
        loga_ref[...] = (_log_sigmoid(z) * (1.0 / GATE_NORMALIZER)).reshape(nb, tm, GLA_DK)

    tn = proj_ref.shape[-1]

    @pl.when(j < n_head)
    def _():
        proj_ref[...] = _dot(xn_ref[...], whead_ref[...]).astype(BF16).reshape(nb, tm, tn)

    @pl.when(j >= n_head)
    def _():
        proj_ref[...] = _dot(xn_ref[...], wtail_ref[...]).astype(BF16).reshape(nb, tm, tn)


def _inproj(x, g, w_all, w_tail, w_alr, w_g2, b_gate, *, nb, tm, tn):
    bsz, t, _ = x.shape
    n_head = ALR_OFF // tn
    grid = (bsz // nb, t // tm, D_MAIN // tn)
    return pl.pallas_call(
        functools.partial(_inproj_kernel, nb=nb, tm=tm, n_head=n_head),
        grid=grid,
        in_specs=[
            pl.BlockSpec((nb, tm, D_MODEL), lambda b, i, j: (b, i, 0)),
            pl.BlockSpec((1, D_MODEL), lambda b, i, j: (0, 0)),
            pl.BlockSpec((D_MODEL, tn), lambda b, i, j: (0, jnp.minimum(j, n_head - 1))),
            pl.BlockSpec((D_MODEL, tn), lambda b, i, j: (0, jnp.maximum(j - n_head, 0))),
            pl.BlockSpec((D_MODEL, LANE), lambda b, i, j: (0, 0)),
            pl.BlockSpec((LANE, GLA_DK), lambda b, i, j: (0, 0)),
            pl.BlockSpec((1, GLA_DK), lambda b, i, j: (0, 0)),
        ],
        out_specs=[
            pl.BlockSpec((nb, tm, tn), lambda b, i, j: (b, i, j)),
            pl.BlockSpec((nb, tm, GLA_DK), lambda b, i, j: (b, i, 0)),
        ],
        out_shape=[
            jax.ShapeDtypeStruct((bsz, t, D_MAIN), BF16),
            jax.ShapeDtypeStruct((bsz, t, GLA_DK), F32),
        ],
        scratch_shapes=[pltpu.VMEM((nb * tm, D_MODEL), BF16)],
        compiler_params=pltpu.CompilerParams(
            dimension_semantics=("parallel", "parallel", "arbitrary"),
            vmem_limit_bytes=VMEM_LIMIT),
        name="inproj",
    )(x, g, w_all, w_tail, w_alr, w_g2, b_gate)


def _gla_head(q, k, v, la, s_old, *, diag):
    rows = q.shape[0]
    n_d = rows // diag
    r_i = lax.broadcasted_iota(jnp.int32, (rows, rows), 0)
    c_i = lax.broadcasted_iota(jnp.int32, (rows, rows), 1)
    tril = (r_i >= c_i).astype(BF16)
    la_hi, la_lo = _split_bf16(la * LOG2_E)
    b = _dot(tril, la_hi) + _dot(tril, la_lo)
    b_end = b[rows - 1:rows, :]

    causal_d = (lax.broadcasted_iota(jnp.int32, (diag, diag), 0)
                >= lax.broadcasted_iota(jnp.int32, (diag, diag), 1))
    a_c, k_c, cen = [], [], []
    for d in range(n_d):
        lo = d * diag
        mid = lo + diag // 2
        c = b[mid - 1:mid, :]
        bd = b[lo:lo + diag, :]
        cen.append(c)
        a_c.append(q[lo:lo + diag, :] * jnp.exp2(bd - (c - Q_LOG2_SCALE)))
        k_c.append(k[lo:lo + diag, :] * jnp.exp2(c - bd))

    def block(lo_d, hi_d):
        if hi_d - lo_d == 1:
            s = lax.dot_general(a_c[lo_d].astype(BF16), k_c[lo_d].astype(BF16), NT_DIMS,
                                preferred_element_type=F32)
            return jnp.where(causal_d, s, 0.0)
        mid_d = (lo_d + hi_d) // 2
        ref = b[mid_d * diag - 1:mid_d * diag, :]
        lhs = [a_c[d] * jnp.exp2(cen[d] - ref) for d in range(mid_d, hi_d)]
        rhs = [k_c[d] * jnp.exp2(ref - cen[d]) for d in range(lo_d, mid_d)]
        lhs = (lhs[0] if len(lhs) == 1 else jnp.concatenate(lhs, axis=0)).astype(BF16)
        rhs = (rhs[0] if len(rhs) == 1 else jnp.concatenate(rhs, axis=0)).astype(BF16)
        off = lax.dot_general(lhs, rhs, NT_DIMS, preferred_element_type=F32)
        top = jnp.concatenate([block(lo_d, mid_d), jnp.zeros_like(off)], axis=1)
        bot = jnp.concatenate([off, block(mid_d, hi_d)], axis=1)
        return jnp.concatenate([top, bot], axis=0)

    p = block(0, n_d).astype(BF16)
    qt = [a_c[d] * jnp.exp2(cen[d]) for d in range(n_d)]
    kt = [k_c[d] * jnp.exp2(b_end - cen[d]) for d in range(n_d)]
    qt = (qt[0] if n_d == 1 else jnp.concatenate(qt, axis=0)).astype(BF16)
    kt = (kt[0] if n_d == 1 else jnp.concatenate(kt, axis=0)).astype(BF16)
    o = _dot(qt, s_old.astype(BF16)) + _dot(p, v)
    e_rep = jnp.broadcast_to(b_end * (1.0 / BF16_ROWS), (BF16_ROWS, b_end.shape[1]))
    e_hi, e_lo = _split_bf16(e_rep)
    ones_col = jnp.ones((BF16_ROWS, LANE), BF16)
    dcol = (lax.dot_general(e_hi, ones_col, TN_DIMS, preferred_element_type=F32)
            + lax.dot_general(e_lo, ones_col, TN_DIMS, preferred_element_type=F32))
    decay = jnp.tile(jnp.exp2(dcol), (1, v.shape[1] // LANE))
    s_new = s_old * decay + lax.dot_general(kt, v, TN_DIMS, preferred_element_type=F32)
    return o, s_new


def _gla_kernel(*refs, diag, has_state):
    if has_state:
        q_ref, k_ref, v_ref, go_ref, la_ref, gn_ref, s0_ref, o_ref, sout_ref, s_ref = refs
    else:
        q_ref, k_ref, v_ref, go_ref, la_ref, gn_ref, o_ref, sout_ref, s_ref = refs
    i = pl.program_id(1)

    @pl.when(i == 0)
    def _():
        if has_state:
            s_ref[...] = s0_ref[0]
        else:
            s_ref[...] = jnp.zeros_like(s_ref)

    for h in range(GLA_HEADS):
        ck = pl.ds(h * GLA_HK, GLA_HK)
        cv = pl.ds(h * GLA_HV, GLA_HV)
        q = q_ref[0, :, ck].astype(F32)
        k = k_ref[0, :, ck].astype(F32)
        o, s_new = _gla_head(q, k, v_ref[0, :, cv], la_ref[0, :, ck], s_ref[h], diag=diag)
        s_ref[h] = s_new
        g = go_ref[0, :, cv].astype(F32)
        o_ref[0, :, cv] = (o * _rms_scale(o) * gn_ref[:, cv] * (g * _sigmoid(g))).astype(BF16)

    @pl.when(i == pl.num_programs(1) - 1)
    def _():
        sout_ref[0] = s_ref[...]


def _gla(proj, loga, gla_norm, state, *, tq, diag):
    bsz, t, _ = proj.shape
    has_state = state is not None
    grid = (bsz, t // tq)
    state_spec = pl.BlockSpec((1, GLA_HEADS, GLA_HK, GLA_HV), lambda b, i: (b, 0, 0, 0))
    in_specs = [
        pl.BlockSpec((1, tq, GLA_DK), lambda b, i: (b, i, OFF_Q // GLA_DK)),
        pl.BlockSpec((1, tq, GLA_DK), lambda b, i: (b, i, OFF_K // GLA_DK)),
        pl.BlockSpec((1, tq, GLA_DV), lambda b, i: (b, i, OFF_V // GLA_DV)),
        pl.BlockSpec((1, tq, GLA_DV), lambda b, i: (b, i, OFF_GO // GLA_DV)),
        pl.BlockSpec((1, tq, GLA_DK), lambda b, i: (b, i, 0)),
        pl.BlockSpec((1, GLA_DV), lambda b, i: (0, 0)),
    ]
    args = [proj, proj, proj, proj, loga, gla_norm]
    if has_state:
        in_specs.append(state_spec)
        args.append(state)
    return pl.pallas_call(
        functools.partial(_gla_kernel, diag=diag, has_state=has_state),
        grid=grid,
        in_specs=in_specs,
        out_specs=[
            pl.BlockSpec((1, tq, GLA_DV), lambda b, i: (b, i, 0)),
            state_spec,
        ],
        out_shape=[
            jax.ShapeDtypeStruct((bsz, t, GLA_DV), BF16),
            jax.ShapeDtypeStruct((bsz, GLA_HEADS, GLA_HK, GLA_HV), F32),
        ],
        scratch_shapes=[pltpu.VMEM((GLA_HEADS, GLA_HK, GLA_HV), F32)],
        compiler_params=pltpu.CompilerParams(
            dimension_semantics=("parallel", "arbitrary"),
            vmem_limit_bytes=VMEM_LIMIT),
        name="gla",
    )(*args)


def _mix_kernel(*refs, nb, tm, has_state):
    if has_state:
        (ba_ref, ca_ref, va_ref, ma_ref, mb_ref, og_ref, x_ref, cw_ref, st_ref,
         wa_ref, wb_ref, wo_ref, x1_ref, newc_ref, carry_ref) = refs
    else:
        (ba_ref, ca_ref, va_ref, ma_ref, mb_ref, og_ref, x_ref, cw_ref,
         wa_ref, wb_ref, wo_ref, x1_ref, newc_ref, carry_ref) = refs
    i = pl.program_id(1)
    rows = nb * tm
    first = i == 0
    cw = cw_ref[...]

    conv_parts = []
    for s in range(nb):
        cv = ca_ref[s].astype(F32) * va_ref[s].astype(F32)
        init = st_ref[s] if has_state else jnp.zeros((CONV_W - 1, D_MODEL), F32)
        halo = jnp.where(first, init, carry_ref[s, 0:2, :])
        conv_parts.append(_conv3(cv, halo[0:1], halo[1:2], cw))
        tail = cv[tm - 2:tm, :]
        carry_ref[s, 0:2, :] = tail
        newc_ref[s] = tail
    conv = conv_parts[0] if nb == 1 else jnp.concatenate(conv_parts, axis=0)

    b_a = ba_ref[...].reshape(rows, D_MODEL).astype(F32)
    y_a = _dot((b_a * conv).astype(BF16), wa_ref[...])
    y_b = _dot(og_ref[...].reshape(rows, D_MODEL), wb_ref[...])
    m_a = ma_ref[...].reshape(rows, D_MODEL).astype(F32)
    m_b = mb_ref[...].reshape(rows, D_MODEL).astype(F32)
    z = _sigmoid(m_a) * y_a + _sigmoid(m_b) * y_b
    x1 = x_ref[...].reshape(rows, D_MODEL) + _dot(z.astype(BF16), wo_ref[...])
    x1_ref[...] = x1.reshape(nb, tm, D_MODEL)


def _mix(proj, og, x, conv_w, state, w_a, w_b, w_o, *, nb, tm):
    bsz, t, _ = x.shape
    has_state = state is not None
    grid = (bsz // nb, t // tm)

    def col(off):
        return pl.BlockSpec((nb, tm, D_MODEL), lambda b, i: (b, i, off // D_MODEL))

    row_spec = pl.BlockSpec((nb, tm, D_MODEL), lambda b, i: (b, i, 0))
    w_spec = pl.BlockSpec((D_MODEL, D_MODEL), lambda b, i: (0, 0), pipeline_mode=pl.Buffered(1))
    in_specs = [col(OFF_BA), col(OFF_CA), col(OFF_VA), col(OFF_MA), col(OFF_MB), row_spec, row_spec,
                pl.BlockSpec((CONV_W, D_MODEL), lambda b, i: (0, 0))]
    args = [proj, proj, proj, proj, proj, og, x, conv_w]
    if has_state:
        in_specs.append(pl.BlockSpec((nb, CONV_W - 1, D_MODEL), lambda b, i: (b, 0, 0)))
        args.append(state)
    in_specs += [w_spec, w_spec, w_spec]
    args += [w_a, w_b, w_o]
    return pl.pallas_call(
        functools.partial(_mix_kernel, nb=nb, tm=tm, has_state=has_state),
        grid=grid,
        in_specs=in_specs,
        out_specs=[
            row_spec,
            pl.BlockSpec((nb, CONV_W - 1, D_MODEL), lambda b, i: (b, 0, 0)),
        ],
        out_shape=[
            jax.ShapeDtypeStruct((bsz, t, D_MODEL), F32),
            jax.ShapeDtypeStruct((bsz, CONV_W - 1, D_MODEL), F32),
        ],
        scratch_shapes=[pltpu.VMEM((nb, 8, D_MODEL), F32)],
        compiler_params=pltpu.CompilerParams(
            dimension_semantics=("parallel", "arbitrary"),
            vmem_limit_bytes=VMEM_LIMIT),
        name="mix",
    )(*args)


def _ffn_up_kernel(*refs, nb, tm, tn, has_state):
    if has_state:
        (x1_ref, g_ref, wv_ref, wg_ref, cwv_ref, cwg_ref, bv_ref, bg_ref, stv_ref, stg_ref,
         h_ref, nfv_ref, nfg_ref, hn_ref, cv_ref, cg_ref) = refs
    else:
        (x1_ref, g_ref, wv_ref, wg_ref, cwv_ref, cwg_ref, bv_ref, bg_ref,
         h_ref, nfv_ref, nfg_ref, hn_ref, cv_ref, cg_ref) = refs
    i = pl.program_id(1)
    j = pl.program_id(2)
    rows = nb * tm
    first = i == 0

    @pl.when(j == 0)
    def _():
        x1 = x1_ref[...].reshape(rows, D_MODEL)
        hn_ref[...] = (x1 * _rms_scale(x1) * g_ref[...]).astype(BF16)

    n_parts = ROW_PARTS if nb == 1 and tm % (ROW_PARTS * SUBLANES) == 0 else 1
    part = rows // n_parts
    ug = [_dot(hn_ref[r * part:(r + 1) * part, :], wg_ref[...]) for r in range(n_parts)]
    uv = [_dot(hn_ref[r * part:(r + 1) * part, :], wv_ref[...]) for r in range(n_parts)]
    cwv = cwv_ref[...]
    cwg = cwg_ref[...]
    zero = jnp.zeros((CONV_W - 1, tn), F32)
    segs = [(0, r) for r in range(n_parts)] if n_parts > 1 else [(s, None) for s in range(nb)]
    act, halo_g, halo_v = {}, None, None
    for s, r in segs:
        ugs = ug[r] if r is not None else ug[0][s * tm:(s + 1) * tm]
        if not r:
            halo_g = jnp.where(first, stg_ref[s] if has_state else zero, cg_ref[j, s, 0:2, :])
        gate = _conv3(ugs, halo_g[0:1], halo_g[1:2], cwg) + bg_ref[...]
        act[(s, r)] = gate * _sigmoid(gate)
        halo_g = ugs[ugs.shape[0] - 2:, :]
        if r is None or r == n_parts - 1:
            cg_ref[j, s, 0:2, :] = halo_g
            nfg_ref[s, 0] = halo_g
    for s, r in segs:
        uvs = uv[r] if r is not None else uv[0][s * tm:(s + 1) * tm]
        if not r:
            halo_v = jnp.where(first, stv_ref[s] if has_state else zero, cv_ref[j, s, 0:2, :])
        val = _conv3(uvs, halo_v[0:1], halo_v[1:2], cwv) + bv_ref[...]
        halo_v = uvs[uvs.shape[0] - 2:, :]
        if r is None or r == n_parts - 1:
            cv_ref[j, s, 0:2, :] = halo_v
            nfv_ref[s, 0] = halo_v
        out = (act[(s, r)] * val).astype(BF16)
        if r is None:
            h_ref[s] = out
        else:
            h_ref[s, r * part:(r + 1) * part, :] = out


def _ffn_up(x1, g, w_up, conv_w, conv_b, state, *, nb, tm, tn):
    bsz, t, _ = x1.shape
    has_state = state is not None
    n_ct = D_FF // tn
    grid = (bsz // nb, t // tm, n_ct)
    in_specs = [
        pl.BlockSpec((nb, tm, D_MODEL), lambda b, i, j: (b, i, 0)),
        pl.BlockSpec((1, D_MODEL), lambda b, i, j: (0, 0)),
        pl.BlockSpec((D_MODEL, tn), lambda b, i, j: (0, j)),
        pl.BlockSpec((D_MODEL, tn), lambda b, i, j: (0, n_ct + j)),
        pl.BlockSpec((CONV_W, tn), lambda b, i, j: (0, j)),
        pl.BlockSpec((CONV_W, tn), lambda b, i, j: (0, n_ct + j)),
        pl.BlockSpec((1, tn), lambda b, i, j: (0, j)),
        pl.BlockSpec((1, tn), lambda b, i, j: (0, n_ct + j)),
    ]
    args = [x1, g, w_up, w_up, conv_w, conv_w, conv_b, conv_b]
    if has_state:
        in_specs += [
            pl.BlockSpec((nb, CONV_W - 1, tn), lambda b, i, j: (b, 0, j)),
            pl.BlockSpec((nb, CONV_W - 1, tn), lambda b, i, j: (b, 0, n_ct + j)),
        ]
        args += [state, state]
    tail_spec = pl.BlockSpec((nb, 1, CONV_W - 1, tn), lambda b, i, j: (b, i, 0, j))
    return pl.pallas_call(
        functools.partial(_ffn_up_kernel, nb=nb, tm=tm, tn=tn, has_state=has_state),
        grid=grid,
        in_specs=in_specs,
        out_specs=[
            pl.BlockSpec((nb, tm, tn), lambda b, i, j: (b, i, j)),
            tail_spec,
            tail_spec,
        ],
        out_shape=[
            jax.ShapeDtypeStruct((bsz, t, D_FF), BF16),
            jax.ShapeDtypeStruct((bsz, t // tm, CONV_W - 1, D_FF), F32),
            jax.ShapeDtypeStruct((bsz, t // tm, CONV_W - 1, D_FF), F32),
        ],
        scratch_shapes=[
            pltpu.VMEM((nb * tm, D_MODEL), BF16),
            pltpu.VMEM((n_ct, nb, 8, tn), F32),
            pltpu.VMEM((n_ct, nb, 8, tn), F32),
        ],
        compiler_params=pltpu.CompilerParams(
            dimension_semantics=("parallel", "arbitrary", "arbitrary"),
            vmem_limit_bytes=VMEM_LIMIT),
        name="ffn_up",
    )(*args)


def _ffn_down_kernel(h_ref, x1_ref, p_ref, wd_ref, gp_ref, wpg_ref, wple_ref, gf_ref, y_ref):
    x2 = x1_ref[...] + _dot(h_ref[...], wd_ref[...])
    pn = (x2 * _rms_scale(x2) * gp_ref[...]).astype(BF16)
    gate = _sigmoid(_dot(pn, wpg_ref[...]))
    pe = _dot(p_ref[...].astype(BF16), wple_ref[...])
    x3 = x2 + gate * pe
    y_ref[...] = x3 * _rms_scale(x3) * gf_ref[...]


def _ffn_down(h, x1, p, w_down, g_ple, w_pg, w_ple, g_final, *, tm):
    m = x1.shape[0]
    const = dict(pipeline_mode=pl.Buffered(1))
    return pl.pallas_call(
        _ffn_down_kernel,
        grid=(m // tm,),
        in_specs=[
            pl.BlockSpec((tm, D_FF), lambda i: (i, 0)),
            pl.BlockSpec((tm, D_MODEL), lambda i: (i, 0)),
            pl.BlockSpec((tm, D_PLE), lambda i: (i, 0)),
            pl.BlockSpec((D_FF, D_MODEL), lambda i: (0, 0), **const),
            pl.BlockSpec((1, D_MODEL), lambda i: (0, 0)),
            pl.BlockSpec((D_MODEL, D_MODEL), lambda i: (0, 0), **const),
            pl.BlockSpec((D_PLE, D_MODEL), lambda i: (0, 0), **const),
            pl.BlockSpec((1, D_MODEL), lambda i: (0, 0)),
        ],
        out_specs=pl.BlockSpec((tm, D_MODEL), lambda i: (i, 0)),
        out_shape=jax.ShapeDtypeStruct((m, D_MODEL), F32),
        compiler_params=pltpu.CompilerParams(
            dimension_semantics=("parallel",),
            vmem_limit_bytes=VMEM_LIMIT),
        name="ffn_down",
    )(h, x1, p, w_down, g_ple, w_pg, w_ple, g_final)


def _tiles(bsz, t):
    if t >= 1024:
        return dict(nb=1, tm_in=1024, tm_mix=256, tm_up=1024, tq=256, diag=128)
    return dict(nb=bsz, tm_in=t, tm_mix=t, tm_up=t, tq=t, diag=t)


def _layer(x, p, conv_state, gla_state, ffn_state, w, g_final):
    bsz, t, _ = x.shape
    cfg = _tiles(bsz, t)
    nb = cfg["nb"]
    proj, loga = _inproj(x, w["norm_mix"], w["w_all"], w["w_tail"], w["w_alr"], w["w_g2"], w["b_gate"],
                         nb=nb, tm=cfg["tm_in"], tn=1024)
    og, new_s = _gla(proj, loga, w["gla_norm"], gla_state, tq=cfg["tq"], diag=cfg["diag"])
    x1, new_conv = _mix(proj, og, x, w["conv_a_w"], conv_state, w["w_a_out"], w["w_b_out"], w["w_o"],
                        nb=nb, tm=cfg["tm_mix"])
    h, nf_v, nf_g = _ffn_up(x1, w["norm_ffn"], w["w_up"], w["ffn_conv_w"], w["ffn_conv_b"], ffn_state,
                            nb=nb, tm=cfg["tm_up"], tn=512)
    m = bsz * t
    y = _ffn_down(h.reshape(m, D_FF), x1.reshape(m, D_MODEL), p.reshape(m, D_PLE),
                  w["w_down"], w["norm_ple"], w["w_ple_gate"], w["w_ple"], g_final, tm=256)
    new_ffn = jnp.concatenate([nf_v[:, -1], nf_g[:, -1]], axis=-1)
    return y.reshape(bsz, t, D_MODEL), new_conv, new_s, new_ffn


def kernel(x_prompt, x_sample, p_prompt, p_sample, state_conv_a, state_gla, state_ffn_conv, norm_mix, w_in, conv_a_w, w_a_out, w_gate2, b_gate, gla_norm, w_b_out, w_o, norm_ffn, w_up, ffn_conv_w, ffn_conv_b, w_down, norm_ple, w_ple_gate, w_ple, norm_final):
    depth = w_in.shape[0]
    assert depth == 1, "the final norm is fused into the last layer; one layer supported"
    i = 0
    w_in_i = w_in[i]
    w_alr = jnp.pad(w_in_i[:, ALR_OFF:ALR_OFF + GATE_RANK], ((0, 0), (0, LANE - GATE_RANK)))
    w = dict(
        norm_mix=norm_mix[i][None],
        w_all=w_in_i.astype(BF16),
        w_tail=w_in_i[:, ALR_OFF + GATE_RANK:].astype(BF16),
        w_alr=w_alr.astype(BF16),
        w_g2=jnp.pad(w_gate2[i], ((0, LANE - GATE_RANK), (0, 0))).astype(BF16),
        b_gate=b_gate[i][None],
        gla_norm=gla_norm[i][None],
        conv_a_w=conv_a_w[i],
        w_a_out=w_a_out[i].astype(BF16),
        w_b_out=w_b_out[i].astype(BF16),
        w_o=w_o[i].astype(BF16),
        norm_ffn=norm_ffn[i][None],
        w_up=w_up[i].astype(BF16),
        ffn_conv_w=ffn_conv_w[i],
        ffn_conv_b=ffn_conv_b[i][None],
        w_down=w_down[i].astype(BF16),
        norm_ple=norm_ple[i][None],
        w_ple_gate=w_ple_gate[i].astype(BF16),
        w_ple=w_ple[i].astype(BF16),
    )
    g_final = norm_final[None]
    yp, c1, s1, f1 = _layer(x_prompt, p_prompt[i], None, None, None, w, g_final)
    ys, c2, s2, f2 = _layer(x_sample, p_sample[i], state_conv_a[i], state_gla[i], state_ffn_conv[i],
                            w, g_final)
    return (yp, ys, c1[None], s1[None], f1[None], c2[None], s2[None], f2[None])
```

```python
import functools
import math

import jax
import jax.numpy as jnp
from jax import lax
from jax.experimental import pallas as pl
from jax.experimental.pallas import tpu as pltpu

F32 = jnp.float32
BF16 = jnp.bfloat16

D_MODEL = 2048
CONV_W = 3
GLA_HEADS = 4
GLA_DK = D_MODEL // 2
GLA_DV = D_MODEL
GLA_HK = GLA_DK // GLA_HEADS
GLA_HV = GLA_DV // GLA_HEADS
GATE_RANK = 16
GATE_NORMALIZER = 16.0
D_FF = 5632
D_PLE = 256
EPS = 1e-6

D_MAIN = 3 * D_MODEL + 2 * GLA_DK + 2 * GLA_DV + 2 * D_MODEL
OFF_BA, OFF_CA, OFF_VA = 0, D_MODEL, 2 * D_MODEL
OFF_Q = 3 * D_MODEL
OFF_K = OFF_Q + GLA_DK
OFF_V = OFF_K + GLA_DK
OFF_GO = OFF_V + GLA_DV
OFF_MA = OFF_GO + GLA_DV
OFF_MB = OFF_MA + D_MODEL
ALR_OFF = OFF_MA
LANE = 128
SUBLANES = 8
ROW_PARTS = 4
BF16_ROWS = 16
VMEM_LIMIT = 56 * 1024 * 1024

LOG2_E = math.log2(math.e)
Q_LOG2_SCALE = -0.5 * math.log2(GLA_HK)

NT_DIMS = (((1,), (1,)), ((), ()))
TN_DIMS = (((0,), (0,)), ((), ()))


def _rms_scale(x):
    return lax.rsqrt(jnp.mean(x * x, axis=-1, keepdims=True) + EPS)


def _sigmoid(x):
    return 1.0 / (1.0 + jnp.exp(-x))


def _log_sigmoid(z):
    return jnp.minimum(z, 0.0) - jnp.log(1.0 + jnp.exp(-jnp.abs(z)))


def _dot(a, b):
    return jnp.dot(a, b, preferred_element_type=F32)


def _split_bf16(x):
    hi = x.astype(BF16)
    return hi, (x - hi.astype(F32)).astype(BF16)


def _conv3(u, h0, h1, w):
    r1 = pltpu.roll(u, 1, axis=0)
    r2 = pltpu.roll(u, 2, axis=0)
    rows = lax.broadcasted_iota(jnp.int32, (SUBLANES, 1), 0)
    top1 = jnp.where(rows == 0, h1, r1[:SUBLANES])
    top2 = jnp.where(rows == 0, h0, jnp.where(rows == 1, h1, r2[:SUBLANES]))
    sh1 = jnp.concatenate([top1, r1[SUBLANES:]], axis=0)
    sh2 = jnp.concatenate([top2, r2[SUBLANES:]], axis=0)
    return w[0:1] * sh2 + w[1:2] * sh1 + w[2:3] * u


def _inproj_kernel(x_ref, g_ref, whead_ref, wtail_ref, walr_ref, wg2_ref, bg_ref, proj_ref, loga_ref,
                   xn_ref, *, nb, tm, n_head):
    j = pl.program_id(2)
    rows = nb * tm

    @pl.when(j == 0)
    def _():
        x = x_ref[...].reshape(rows, D_MODEL)
        xn = (x * _rms_scale(x) * g_ref[...]).astype(BF16)
        xn_ref[...] = xn
        a_lr = _dot(xn, walr_ref[...])
        z = _dot(a_lr.astype(BF16), wg2_ref[...]) + bg_ref[...]
        loga_ref[...] = (_log_sigmoid(z) * (1.0 / GATE_NORMALIZER)).reshape(nb, tm, GLA_DK)

    tn = proj_ref.shape[-1]

    @pl.when(j < n_head)
    def _():
        proj_ref[...] = _dot(xn_ref[...], whead_ref[...]).astype(BF16).reshape(nb, tm, tn)

    @pl.when(j >= n_head)
    def _():
        proj_ref[...] = _dot(xn_ref[...], wtail_ref[...]).astype(BF16).reshape(nb, tm, tn)


def _inproj(x, g, w_all, w_tail, w_alr, w_g2, b_gate, *, nb, tm, tn):
    bsz, t, _ = x.shape
    n_head = ALR_OFF // tn
    grid = (bsz // nb, t // tm, D_MAIN // tn)
    return pl.pallas_call(
        functools.partial(_inproj_kernel, nb=nb, tm=tm, n_head=n_head),
        grid=grid,
        in_specs=[
            pl.BlockSpec((nb, tm, D_MODEL), lambda b, i, j: (b, i, 0)),
            pl.BlockSpec((1, D_MODEL), lambda b, i, j: (0, 0)),
            pl.BlockSpec((D_MODEL, tn), lambda b, i, j: (0, jnp.minimum(j, n_head - 1))),
            pl.BlockSpec((D_MODEL, tn), lambda b, i, j: (0, jnp.maximum(j - n_head, 0))),
            pl.BlockSpec((D_MODEL, LANE), lambda b, i, j: (0, 0)),
            pl.BlockSpec((LANE, GLA_DK), lambda b, i, j: (0, 0)),
            pl.BlockSpec((1, GLA_DK), lambda b, i, j: (0, 0)),
        ],
        out_specs=[
            pl.BlockSpec((nb, tm, tn), lambda b, i, j: (b, i, j)),
            pl.BlockSpec((nb, tm, GLA_DK), lambda b, i, j: (b, i, 0)),
        ],
        out_shape=[
            jax.ShapeDtypeStruct((bsz, t, D_MAIN), BF16),
            jax.ShapeDtypeStruct((bsz, t, GLA_DK), F32),
        ],
        scratch_shapes=[pltpu.VMEM((nb * tm, D_MODEL), BF16)],
        compiler_params=pltpu.CompilerParams(
            dimension_semantics=("parallel", "parallel", "arbitrary"),
            vmem_limit_bytes=VMEM_LIMIT),
        name="inproj",
    )(x, g, w_all, w_tail, w_alr, w_g2, b_gate)


def _gla_head(q, k, v, la, s_old, *, diag):
    rows = q.shape[0]
    n_d = rows // diag
    r_i = lax.broadcasted_iota(jnp.int32, (rows, rows), 0)
    c_i = lax.broadcasted_iota(jnp.int32, (rows, rows), 1)
    tril = (r_i >= c_i).astype(BF16)
    la_hi, la_lo = _split_bf16(la * LOG2_E)
    b = _dot(tril, la_hi) + _dot(tril, la_lo)
    b_end = b[rows - 1:rows, :]

    causal_d = (lax.broadcasted_iota(jnp.int32, (diag, diag), 0)
                >= lax.broadcasted_iota(jnp.int32, (diag, diag), 1))
    a_c, k_c, cen = [], [], []
    for d in range(n_d):
        lo = d * diag
        mid = lo + diag // 2
        c = b[mid - 1:mid, :]
        bd = b[lo:lo + diag, :]
        cen.append(c)
        a_c.append(q[lo:lo + diag, :] * jnp.exp2(bd - (c - Q_LOG2_SCALE)))
        k_c.append(k[lo:lo + diag, :] * jnp.exp2(c - bd))

    def block(lo_d, hi_d):
        if hi_d - lo_d == 1:
            s = lax.dot_general(a_c[lo_d].astype(BF16), k_c[lo_d].astype(BF16), NT_DIMS,
                                preferred_element_type=F32)
            return jnp.where(causal_d, s, 0.0)
        mid_d = (lo_d + hi_d) // 2
        ref = b[mid_d * diag - 1:mid_d * diag, :]
        lhs = [a_c[d] * jnp.exp2(cen[d] - ref) for d in range(mid_d, hi_d)]
        rhs = [k_c[d] * jnp.exp2(ref - cen[d]) for d in range(lo_d, mid_d)]
        lhs = (lhs[0] if len(lhs) == 1 else jnp.concatenate(lhs, axis=0)).astype(BF16)
        rhs = (rhs[0] if len(rhs) == 1 else jnp.concatenate(rhs, axis=0)).astype(BF16)
        off = lax.dot_general(lhs, rhs, NT_DIMS, preferred_element_type=F32)
        top = jnp.concatenate([block(lo_d, mid_d), jnp.zeros_like(off)], axis=1)
        bot = jnp.concatenate([off, block(mid_d, hi_d)], axis=1)
        return jnp.concatenate([top, bot], axis=0)

    p = block(0, n_d).astype(BF16)
    qt = [a_c[d] * jnp.exp2(cen[d]) for d in range(n_d)]
    kt = [k_c[d] * jnp.exp2(b_end - cen[d]) for d in range(n_d)]
    qt = (qt[0] if n_d == 1 else jnp.concatenate(qt, axis=0)).astype(BF16)
    kt = (kt[0] if n_d == 1 else jnp.concatenate(kt, axis=0)).astype(BF16)
    o = _dot(qt, s_old.astype(BF16)) + _dot(p, v)
    e_rep = jnp.broadcast_to(b_end * (1.0 / BF16_ROWS), (BF16_ROWS, b_end.shape[1]))
    e_hi, e_lo = _split_bf16(e_rep)
    ones_col = jnp.ones((BF16_ROWS, LANE), BF16)
    dcol = (lax.dot_general(e_hi, ones_col, TN_DIMS, preferred_element_type=F32)
            + lax.dot_general(e_lo, ones_col, TN_DIMS, preferred_element_type=F32))
    decay = jnp.tile(jnp.exp2(dcol), (1, v.shape[1] // LANE))
    s_new = s_old * decay + lax.dot_general(kt, v, TN_DIMS, preferred_element_type=F32)
    return o, s_new


def _gla_kernel(*refs, block, diag, has_state):
    if has_state:
        q_ref, k_ref, v_ref, go_ref, la_ref, gn_ref, s0_ref, o_ref, sout_ref, s_ref = refs
    else:
        q_ref, k_ref, v_ref, go_ref, la_ref, gn_ref, o_ref, sout_ref, s_ref = refs
    i = pl.program_id(1)

    @pl.when(i == 0)
    def _():
        if has_state:
            s_ref[...] = s0_ref[0]
        else:
            s_ref[...] = jnp.zeros_like(s_ref)

    for r0 in range(0, q_ref.shape[1], block):
        rs = pl.ds(r0, block)
        for h in range(GLA_HEADS):
            ck = pl.ds(h * GLA_HK, GLA_HK)
            cv = pl.ds(h * GLA_HV, GLA_HV)
            q = q_ref[0, rs, ck].astype(F32)
            k = k_ref[0, rs, ck].astype(F32)
            o, s_new = _gla_head(q, k, v_ref[0, rs, cv], la_ref[0, rs, ck], s_ref[h], diag=diag)
            s_ref[h] = s_new
            g = go_ref[0, rs, cv].astype(F32)
            o_ref[0, rs, cv] = (o * _rms_scale(o) * gn_ref[:, cv] * (g * _sigmoid(g))).astype(BF16)

    @pl.when(i == pl.num_programs(1) - 1)
    def _():
        sout_ref[0] = s_ref[...]


def _gla(proj, loga, gla_norm, state, *, tq, block, diag):
    bsz, t, _ = proj.shape
    has_state = state is not None
    grid = (bsz, t // tq)
    state_spec = pl.BlockSpec((1, GLA_HEADS, GLA_HK, GLA_HV), lambda b, i: (b, 0, 0, 0))
    in_specs = [
        pl.BlockSpec((1, tq, GLA_DK), lambda b, i: (b, i, OFF_Q // GLA_DK)),
        pl.BlockSpec((1, tq, GLA_DK), lambda b, i: (b, i, OFF_K // GLA_DK)),
        pl.BlockSpec((1, tq, GLA_DV), lambda b, i: (b, i, OFF_V // GLA_DV)),
        pl.BlockSpec((1, tq, GLA_DV), lambda b, i: (b, i, OFF_GO // GLA_DV)),
        pl.BlockSpec((1, tq, GLA_DK), lambda b, i: (b, i, 0)),
        pl.BlockSpec((1, GLA_DV), lambda b, i: (0, 0)),
    ]
    args = [proj, proj, proj, proj, loga, gla_norm]
    if has_state:
        in_specs.append(state_spec)
        args.append(state)
    return pl.pallas_call(
        functools.partial(_gla_kernel, block=block, diag=diag, has_state=has_state),
        grid=grid,
        in_specs=in_specs,
        out_specs=[
            pl.BlockSpec((1, tq, GLA_DV), lambda b, i: (b, i, 0)),
            state_spec,
        ],
        out_shape=[
            jax.ShapeDtypeStruct((bsz, t, GLA_DV), BF16),
            jax.ShapeDtypeStruct((bsz, GLA_HEADS, GLA_HK, GLA_HV), F32),
        ],
        scratch_shapes=[pltpu.VMEM((GLA_HEADS, GLA_HK, GLA_HV), F32)],
        compiler_params=pltpu.CompilerParams(
            dimension_semantics=("parallel", "arbitrary"),
            vmem_limit_bytes=VMEM_LIMIT),
        name="gla",
    )(*args)


def _mix_kernel(*refs, nb, tm, has_state):
    if has_state:
        (ba_ref, ca_ref, va_ref, ma_ref, mb_ref, og_ref, x_ref, cw_ref, st_ref,
         wa_ref, wb_ref, wo_ref, x1_ref, newc_ref, carry_ref) = refs
    else:
        (ba_ref, ca_ref, va_ref, ma_ref, mb_ref, og_ref, x_ref, cw_ref,
         wa_ref, wb_ref, wo_ref, x1_ref, newc_ref, carry_ref) = refs
    i = pl.program_id(1)
    rows = nb * tm
    first = i == 0
    cw = cw_ref[...]

    @pl.when(first)
    def _():
        carry_ref[...] = jnp.zeros_like(carry_ref)

    conv_parts = []
    for s in range(nb):
        cv = ca_ref[s].astype(F32) * va_ref[s].astype(F32)
        init = st_ref[s] if has_state else jnp.zeros((CONV_W - 1, D_MODEL), F32)
        halo = jnp.where(first, init, carry_ref[s, 0:2, :])
        conv_parts.append(_conv3(cv, halo[0:1], halo[1:2], cw))
        tail = cv[tm - 2:tm, :]
        carry_ref[s, 0:2, :] = tail
        newc_ref[s] = tail
    conv = conv_parts[0] if nb == 1 else jnp.concatenate(conv_parts, axis=0)

    b_a = ba_ref[...].reshape(rows, D_MODEL).astype(F32)
    y_a = _dot((b_a * conv).astype(BF16), wa_ref[...])
    y_b = _dot(og_ref[...].reshape(rows, D_MODEL), wb_ref[...])
    m_a = ma_ref[...].reshape(rows, D_MODEL).astype(F32)
    m_b = mb_ref[...].reshape(rows, D_MODEL).astype(F32)
    z = _sigmoid(m_a) * y_a + _sigmoid(m_b) * y_b
    x1 = x_ref[...].reshape(rows, D_MODEL) + _dot(z.astype(BF16), wo_ref[...])
    x1_ref[...] = x1.reshape(nb, tm, D_MODEL)


def _mix(proj, og, x, conv_w, state, w_a, w_b, w_o, *, nb, tm):
    bsz, t, _ = x.shape
    has_state = state is not None
    grid = (bsz // nb, t // tm)

    def col(off):
        return pl.BlockSpec((nb, tm, D_MODEL), lambda b, i: (b, i, off // D_MODEL))

    row_spec = pl.BlockSpec((nb, tm, D_MODEL), lambda b, i: (b, i, 0))
    w_spec = pl.BlockSpec((D_MODEL, D_MODEL), lambda b, i: (0, 0), pipeline_mode=pl.Buffered(1))
    in_specs = [col(OFF_BA), col(OFF_CA), col(OFF_VA), col(OFF_MA), col(OFF_MB), row_spec, row_spec,
                pl.BlockSpec((CONV_W, D_MODEL), lambda b, i: (0, 0))]
    args = [proj, proj, proj, proj, proj, og, x, conv_w]
    if has_state:
        in_specs.append(pl.BlockSpec((nb, CONV_W - 1, D_MODEL), lambda b, i: (b, 0, 0)))
        args.append(state)
    in_specs += [w_spec, w_spec, w_spec]
    args += [w_a, w_b, w_o]
    return pl.pallas_call(
        functools.partial(_mix_kernel, nb=nb, tm=tm, has_state=has_state),
        grid=grid,
        in_specs=in_specs,
        out_specs=[
            row_spec,
            pl.BlockSpec((nb, CONV_W - 1, D_MODEL), lambda b, i: (b, 0, 0)),
        ],
        out_shape=[
            jax.ShapeDtypeStruct((bsz, t, D_MODEL), F32),
            jax.ShapeDtypeStruct((bsz, CONV_W - 1, D_MODEL), F32),
        ],
        scratch_shapes=[pltpu.VMEM((nb, 8, D_MODEL), F32)],
        compiler_params=pltpu.CompilerParams(
            dimension_semantics=("parallel", "arbitrary"),
            vmem_limit_bytes=VMEM_LIMIT),
        name="mix",
    )(*args)


def _ffn_up_kernel(*refs, nb, tm, tn, has_state):
    if has_state:
        (x1_ref, g_ref, wv_ref, wg_ref, cwv_ref, cwg_ref, bv_ref, bg_ref, stv_ref, stg_ref,
         h_ref, nfv_ref, nfg_ref, hn_ref, cv_ref, cg_ref) = refs
    else:
        (x1_ref, g_ref, wv_ref, wg_ref, cwv_ref, cwg_ref, bv_ref, bg_ref,
         h_ref, nfv_ref, nfg_ref, hn_ref, cv_ref, cg_ref) = refs
    i = pl.program_id(1)
    j = pl.program_id(2)
    rows = nb * tm
    first = i == 0

    @pl.when(j == 0)
    def _():
        x1 = x1_ref[...].reshape(rows, D_MODEL)
        hn_ref[...] = (x1 * _rms_scale(x1) * g_ref[...]).astype(BF16)

    @pl.when(first)
    def _():
        cv_ref[j] = jnp.zeros(cv_ref.shape[1:], F32)
        cg_ref[j] = jnp.zeros(cg_ref.shape[1:], F32)

    n_parts = ROW_PARTS if nb == 1 and tm % (ROW_PARTS * SUBLANES) == 0 else 1
    part = rows // n_parts
    ug = [_dot(hn_ref[r * part:(r + 1) * part, :], wg_ref[...]) for r in range(n_parts)]
    uv = [_dot(hn_ref[r * part:(r + 1) * part, :], wv_ref[...]) for r in range(n_parts)]
    cwv = cwv_ref[...]
    cwg = cwg_ref[...]
    zero = jnp.zeros((CONV_W - 1, tn), F32)
    segs = [(0, r) for r in range(n_parts)] if n_parts > 1 else [(s, None) for s in range(nb)]
    act, halo_g, halo_v = {}, None, None
    for s, r in segs:
        ugs = ug[r] if r is not None else ug[0][s * tm:(s + 1) * tm]
        if not r:
            halo_g = jnp.where(first, stg_ref[s] if has_state else zero, cg_ref[j, s, 0:2, :])
        gate = _conv3(ugs, halo_g[0:1], halo_g[1:2], cwg) + bg_ref[...]
        act[(s, r)] = gate * _sigmoid(gate)
        halo_g = ugs[ugs.shape[0] - 2:, :]
        if r is None or r == n_parts - 1:
            cg_ref[j, s, 0:2, :] = halo_g
            nfg_ref[s, 0] = halo_g
    for s, r in segs:
        uvs = uv[r] if r is not None else uv[0][s * tm:(s + 1) * tm]
        if not r:
            halo_v = jnp.where(first, stv_ref[s] if has_state else zero, cv_ref[j, s, 0:2, :])
        val = _conv3(uvs, halo_v[0:1], halo_v[1:2], cwv) + bv_ref[...]
        halo_v = uvs[uvs.shape[0] - 2:, :]
        if r is None or r == n_parts - 1:
            cv_ref[j, s, 0:2, :] = halo_v
            nfv_ref[s, 0] = halo_v
        out = (act[(s, r)] * val).astype(BF16)
        if r is None:
            h_ref[s] = out
        else:
            h_ref[s, r * part:(r + 1) * part, :] = out


def _ffn_up(x1, g, w_up, conv_w, conv_b, state, *, nb, tm, tn):
    bsz, t, _ = x1.shape
    has_state = state is not None
    n_ct = D_FF // tn
    grid = (bsz // nb, t // tm, n_ct)
    in_specs = [
        pl.BlockSpec((nb, tm, D_MODEL), lambda b, i, j: (b, i, 0)),
        pl.BlockSpec((1, D_MODEL), lambda b, i, j: (0, 0)),
        pl.BlockSpec((D_MODEL, tn), lambda b, i, j: (0, j)),
        pl.BlockSpec((D_MODEL, tn), lambda b, i, j: (0, n_ct + j)),
        pl.BlockSpec((CONV_W, tn), lambda b, i, j: (0, j)),
        pl.BlockSpec((CONV_W, tn), lambda b, i, j: (0, n_ct + j)),
        pl.BlockSpec((1, tn), lambda b, i, j: (0, j)),
        pl.BlockSpec((1, tn), lambda b, i, j: (0, n_ct + j)),
    ]
    args = [x1, g, w_up, w_up, conv_w, conv_w, conv_b, conv_b]
    if has_state:
        in_specs += [
            pl.BlockSpec((nb, CONV_W - 1, tn), lambda b, i, j: (b, 0, j)),
            pl.BlockSpec((nb, CONV_W - 1, tn), lambda b, i, j: (b, 0, n_ct + j)),
        ]
        args += [state, state]
    tail_spec = pl.BlockSpec((nb, 1, CONV_W - 1, tn), lambda b, i, j: (b, i, 0, j))
    return pl.pallas_call(
        functools.partial(_ffn_up_kernel, nb=nb, tm=tm, tn=tn, has_state=has_state),
        grid=grid,
        in_specs=in_specs,
        out_specs=[
            pl.BlockSpec((nb, tm, tn), lambda b, i, j: (b, i, j)),
            tail_spec,
            tail_spec,
        ],
        out_shape=[
            jax.ShapeDtypeStruct((bsz, t, D_FF), BF16),
            jax.ShapeDtypeStruct((bsz, t // tm, CONV_W - 1, D_FF), F32),
            jax.ShapeDtypeStruct((bsz, t // tm, CONV_W - 1, D_FF), F32),
        ],
        scratch_shapes=[
            pltpu.VMEM((nb * tm, D_MODEL), BF16),
            pltpu.VMEM((n_ct, nb, 8, tn), F32),
            pltpu.VMEM((n_ct, nb, 8, tn), F32),
        ],
        compiler_params=pltpu.CompilerParams(
            dimension_semantics=("parallel", "arbitrary", "arbitrary"),
            vmem_limit_bytes=VMEM_LIMIT),
        name="ffn_up",
    )(*args)


def _ffn_down_kernel(h_ref, x1_ref, p_ref, wd_ref, gp_ref, wpg_ref, wple_ref, gf_ref, y_ref):
    x2 = x1_ref[...] + _dot(h_ref[...], wd_ref[...])
    pn = (x2 * _rms_scale(x2) * gp_ref[...]).astype(BF16)
    gate = _sigmoid(_dot(pn, wpg_ref[...]))
    pe = _dot(p_ref[...].astype(BF16), wple_ref[...])
    x3 = x2 + gate * pe
    y_ref[...] = x3 * _rms_scale(x3) * gf_ref[...]


def _ffn_down(h, x1, p, w_down, g_ple, w_pg, w_ple, g_final, *, tm):
    m = x1.shape[0]
    const = dict(pipeline_mode=pl.Buffered(1))
    return pl.pallas_call(
        _ffn_down_kernel,
        grid=(m // tm,),
        in_specs=[
            pl.BlockSpec((tm, D_FF), lambda i: (i, 0)),
            pl.BlockSpec((tm, D_MODEL), lambda i: (i, 0)),
            pl.BlockSpec((tm, D_PLE), lambda i: (i, 0)),
            pl.BlockSpec((D_FF, D_MODEL), lambda i: (0, 0), **const),
            pl.BlockSpec((1, D_MODEL), lambda i: (0, 0)),
            pl.BlockSpec((D_MODEL, D_MODEL), lambda i: (0, 0), **const),
            pl.BlockSpec((D_PLE, D_MODEL), lambda i: (0, 0), **const),
            pl.BlockSpec((1, D_MODEL), lambda i: (0, 0)),
        ],
        out_specs=pl.BlockSpec((tm, D_MODEL), lambda i: (i, 0)),
        out_shape=jax.ShapeDtypeStruct((m, D_MODEL), F32),
        compiler_params=pltpu.CompilerParams(
            dimension_semantics=("parallel",),
            vmem_limit_bytes=VMEM_LIMIT),
        name="ffn_down",
    )(h, x1, p, w_down, g_ple, w_pg, w_ple, g_final)


def _cast_w_in_kernel(w_ref, all_ref, tail_ref):
    w = w_ref[...].astype(BF16)
    all_ref[...] = w
    tail_ref[...] = w[:, ALR_OFF + GATE_RANK:]


def _cast_w_in(w_in, *, tr):
    k, n = w_in.shape
    n_tail = n - ALR_OFF - GATE_RANK
    return pl.pallas_call(
        _cast_w_in_kernel,
        grid=(k // tr,),
        in_specs=[pl.BlockSpec((tr, n), lambda i: (i, 0))],
        out_specs=[pl.BlockSpec((tr, n), lambda i: (i, 0)),
                   pl.BlockSpec((tr, n_tail), lambda i: (i, 0))],
        out_shape=[jax.ShapeDtypeStruct((k, n), BF16), jax.ShapeDtypeStruct((k, n_tail), BF16)],
        compiler_params=pltpu.CompilerParams(
            dimension_semantics=("parallel",), vmem_limit_bytes=VMEM_LIMIT),
        name="cast_w_in",
    )(w_in)


def _tiles(bsz, t):
    if t >= 1024:
        return dict(nb=1, tm_in=1024, tm_mix=256, tm_up=1024, tq=256, block=256, diag=128)
    return dict(nb=bsz, tm_in=t, tm_mix=t, tm_up=t, tq=t, block=t, diag=t)


def _layer(x, p, conv_state, gla_state, ffn_state, w, g_final):
    bsz, t, _ = x.shape
    cfg = _tiles(bsz, t)
    nb = cfg["nb"]
    proj, loga = _inproj(x, w["norm_mix"], w["w_all"], w["w_tail"], w["w_alr"], w["w_g2"], w["b_gate"],
                         nb=nb, tm=cfg["tm_in"], tn=1024)
    og, new_s = _gla(proj, loga, w["gla_norm"], gla_state, tq=cfg["tq"], block=cfg["block"],
                      diag=cfg["diag"])
    x1, new_conv = _mix(proj, og, x, w["conv_a_w"], conv_state, w["w_a_out"], w["w_b_out"], w["w_o"],
                        nb=nb, tm=cfg["tm_mix"])
    h, nf_v, nf_g = _ffn_up(x1, w["norm_ffn"], w["w_up"], w["ffn_conv_w"], w["ffn_conv_b"], ffn_state,
                            nb=nb, tm=cfg["tm_up"], tn=512)
    m = bsz * t
    y = _ffn_down(h.reshape(m, D_FF), x1.reshape(m, D_MODEL), p.reshape(m, D_PLE),
                  w["w_down"], w["norm_ple"], w["w_ple_gate"], w["w_ple"], g_final, tm=256)
    new_ffn = jnp.concatenate([nf_v[:, -1], nf_g[:, -1]], axis=-1)
    return y.reshape(bsz, t, D_MODEL), new_conv, new_s, new_ffn


def kernel(x_prompt, x_sample, p_prompt, p_sample, state_conv_a, state_gla, state_ffn_conv, norm_mix, w_in, conv_a_w, w_a_out, w_gate2, b_gate, gla_norm, w_b_out, w_o, norm_ffn, w_up, ffn_conv_w, ffn_conv_b, w_down, norm_ple, w_ple_gate, w_ple, norm_final):
    depth = w_in.shape[0]
    assert depth == 1, "the final norm is fused into the last layer; one layer supported"
    i = 0
    w_in_i = w_in[i]
    w_alr = jnp.pad(w_in_i[:, ALR_OFF:ALR_OFF + GATE_RANK], ((0, 0), (0, LANE - GATE_RANK)))
    w_all, w_tail = _cast_w_in(w_in_i, tr=128)
    w = dict(
        norm_mix=norm_mix[i][None],
        w_all=w_all,
        w_tail=w_tail,
        w_alr=w_alr.astype(BF16),
        w_g2=jnp.pad(w_gate2[i], ((0, LANE - GATE_RANK), (0, 0))).astype(BF16),
        b_gate=b_gate[i][None],
        gla_norm=gla_norm[i][None],
        conv_a_w=conv_a_w[i],
        w_a_out=w_a_out[i].astype(BF16),
        w_b_out=w_b_out[i].astype(BF16),
        w_o=w_o[i].astype(BF16),
        norm_ffn=norm_ffn[i][None],
        w_up=w_up[i].astype(BF16),
        ffn_conv_w=ffn_conv_w[i],
        ffn_conv_b=ffn_conv_b[i][None],
        w_down=w_down[i].astype(BF16),
        norm_ple=norm_ple[i][None],
        w_ple_gate=w_ple_gate[i].astype(BF16),
        w_ple=w_ple[i].astype(BF16),
    )
    g_final = norm_final[None]
    yp, c1, s1, f1 = _layer(x_prompt, p_prompt[i], None, None, None, w, g_final)
    ys, c2, s2, f2 = _layer(x_sample, p_sample[i], state_conv_a[i], state_gla[i], state_ffn_conv[i],
                            w, g_final)
    return (yp, ys, c1[None], s1[None], f1[None], c2[None], s2[None], f2[None])
```

```python
import functools
import math

import jax
import jax.numpy as jnp
from jax import lax
from jax.experimental import pallas as pl
from jax.experimental.pallas import tpu as pltpu

F32 = jnp.float32
BF16 = jnp.bfloat16

D_MODEL = 2048
CONV_W = 3
GLA_HEADS = 4
GLA_DK = D_MODEL // 2
GLA_DV = D_MODEL
GLA_HK = GLA_DK // GLA_HEADS
GLA_HV = GLA_DV // GLA_HEADS
GATE_RANK = 16
GATE_NORMALIZER = 16.0
D_FF = 5632
D_PLE = 256
EPS = 1e-6

D_MAIN = 3 * D_MODEL + 2 * GLA_DK + 2 * GLA_DV + 2 * D_MODEL
OFF_BA, OFF_CA, OFF_VA = 0, D_MODEL, 2 * D_MODEL
OFF_Q = 3 * D_MODEL
OFF_K = OFF_Q + GLA_DK
OFF_V = OFF_K + GLA_DK
OFF_GO = OFF_V + GLA_DV
OFF_MA = OFF_GO + GLA_DV
OFF_MB = OFF_MA + D_MODEL
ALR_OFF = OFF_MA
LANE = 128
SUBLANES = 8
ROW_PARTS = 4
BF16_ROWS = 16
VMEM_LIMIT = 56 * 1024 * 1024

LOG2_E = math.log2(math.e)
Q_LOG2_SCALE = -0.5 * math.log2(GLA_HK)

NT_DIMS = (((1,), (1,)), ((), ()))
TN_DIMS = (((0,), (0,)), ((), ()))


def _rms_scale(x):
    return lax.rsqrt(jnp.mean(x * x, axis=-1, keepdims=True) + EPS)


def _sigmoid(x):
    return 1.0 / (1.0 + jnp.exp(-x))


def _log_sigmoid(z):
    return jnp.minimum(z, 0.0) - jnp.log(1.0 + jnp.exp(-jnp.abs(z)))


def _dot(a, b):
    return jnp.dot(a, b, preferred_element_type=F32)


def _split_bf16(x):
    hi = x.astype(BF16)
    return hi, (x - hi.astype(F32)).astype(BF16)


def _conv3(u, h0, h1, w):
    r1 = pltpu.roll(u, 1, axis=0)
    r2 = pltpu.roll(u, 2, axis=0)
    rows = lax.broadcasted_iota(jnp.int32, (SUBLANES, 1), 0)
    top1 = jnp.where(rows == 0, h1, r1[:SUBLANES])
    top2 = jnp.where(rows == 0, h0, jnp.where(rows == 1, h1, r2[:SUBLANES]))
    sh1 = jnp.concatenate([top1, r1[SUBLANES:]], axis=0)
    sh2 = jnp.concatenate([top2, r2[SUBLANES:]], axis=0)
    return w[0:1] * sh2 + w[1:2] * sh1 + w[2:3] * u


def _inproj_kernel(x_ref, g_ref, whead_ref, wtail_ref, walr_ref, wg2_ref, bg_ref, proj_ref, loga_ref,
                   xn_ref, *, nb, tm, n_head):
    j = pl.program_id(2)
    rows = nb * tm

    @pl.when(j == 0)
    def _():
        x = x_ref[...].reshape(rows, D_MODEL)
        xn = (x * _rms_scale(x) * g_ref[...]).astype(BF16)
        xn_ref[...] = xn
        a_lr = _dot(xn, walr_ref[...])
        z = _dot(a_lr.astype(BF16), wg2_ref[...]) + bg_ref[...]
        loga_ref[...] = (_log_sigmoid(z) * (1.0 / GATE_NORMALIZER)).reshape(nb, tm, GLA_DK)

    tn = proj_ref.shape[-1]

    @pl.when(j < n_head)
    def _():
        proj_ref[...] = _dot(xn_ref[...], whead_ref[...]).astype(BF16).reshape(nb, tm, tn)

    @pl.when(j >= n_head)
    def _():
        proj_ref[...] = _dot(xn_ref[...], wtail_ref[...]).astype(BF16).reshape(nb, tm, tn)


def _inproj(x, g, w_all, w_tail, w_alr, w_g2, b_gate, *, nb, tm, tn):
    bsz, t, _ = x.shape
    n_head = ALR_OFF // tn
    grid = (bsz // nb, t // tm, D_MAIN // tn)
    return pl.pallas_call(
        functools.partial(_inproj_kernel, nb=nb, tm=tm, n_head=n_head),
        grid=grid,
        in_specs=[
            pl.BlockSpec((nb, tm, D_MODEL), lambda b, i, j: (b, i, 0)),
            pl.BlockSpec((1, D_MODEL), lambda b, i, j: (0, 0)),
            pl.BlockSpec((D_MODEL, tn), lambda b, i, j: (0, jnp.minimum(j, n_head - 1))),
            pl.BlockSpec((D_MODEL, tn), lambda b, i, j: (0, jnp.maximum(j - n_head, 0))),
            pl.BlockSpec((D_MODEL, LANE), lambda b, i, j: (0, 0)),
            pl.BlockSpec((LANE, GLA_DK), lambda b, i, j: (0, 0)),
            pl.BlockSpec((1, GLA_DK), lambda b, i, j: (0, 0)),
        ],
        out_specs=[
            pl.BlockSpec((nb, tm, tn), lambda b, i, j: (b, i, j)),
            pl.BlockSpec((nb, tm, GLA_DK), lambda b, i, j: (b, i, 0)),
        ],
        out_shape=[
            jax.ShapeDtypeStruct((bsz, t, D_MAIN), BF16),
            jax.ShapeDtypeStruct((bsz, t, GLA_DK), F32),
        ],
        scratch_shapes=[pltpu.VMEM((nb * tm, D_MODEL), BF16)],
        compiler_params=pltpu.CompilerParams(
            dimension_semantics=("parallel", "parallel", "arbitrary"),
            vmem_limit_bytes=VMEM_LIMIT),
        name="inproj",
    )(x, g, w_all, w_tail, w_alr, w_g2, b_gate)


def _gla_head(q, k, v, la, s_old, *, diag):
    rows = q.shape[0]
    n_d = rows // diag
    r_i = lax.broadcasted_iota(jnp.int32, (rows, rows), 0)
    c_i = lax.broadcasted_iota(jnp.int32, (rows, rows), 1)
    tril = (r_i >= c_i).astype(BF16)
    la_hi, la_lo = _split_bf16(la * LOG2_E)
    b = _dot(tril, la_hi) + _dot(tril, la_lo)
    b_end = b[rows - 1:rows, :]

    causal_d = (lax.broadcasted_iota(jnp.int32, (diag, diag), 0)
                >= lax.broadcasted_iota(jnp.int32, (diag, diag), 1))
    a_c, k_c, cen = [], [], []
    for d in range(n_d):
        lo = d * diag
        mid = lo + diag // 2
        c = b[mid - 1:mid, :]
        bd = b[lo:lo + diag, :]
        cen.append(c)
        a_c.append(q[lo:lo + diag, :] * jnp.exp2(bd - (c - Q_LOG2_SCALE)))
        k_c.append(k[lo:lo + diag, :] * jnp.exp2(c - bd))

    def block(lo_d, hi_d):
        if hi_d - lo_d == 1:
            s = lax.dot_general(a_c[lo_d].astype(BF16), k_c[lo_d].astype(BF16), NT_DIMS,
                                preferred_element_type=F32)
            return jnp.where(causal_d, s, 0.0)
        mid_d = (lo_d + hi_d) // 2
        ref = b[mid_d * diag - 1:mid_d * diag, :]
        lhs = [a_c[d] * jnp.exp2(cen[d] - ref) for d in range(mid_d, hi_d)]
        rhs = [k_c[d] * jnp.exp2(ref - cen[d]) for d in range(lo_d, mid_d)]
        lhs = (lhs[0] if len(lhs) == 1 else jnp.concatenate(lhs, axis=0)).astype(BF16)
        rhs = (rhs[0] if len(rhs) == 1 else jnp.concatenate(rhs, axis=0)).astype(BF16)
        off = lax.dot_general(lhs, rhs, NT_DIMS, preferred_element_type=F32)
        top = jnp.concatenate([block(lo_d, mid_d), jnp.zeros_like(off)], axis=1)
        bot = jnp.concatenate([off, block(mid_d, hi_d)], axis=1)
        return jnp.concatenate([top, bot], axis=0)

    p = block(0, n_d).astype(BF16)
    qt = [a_c[d] * jnp.exp2(cen[d]) for d in range(n_d)]
    kt = [k_c[d] * jnp.exp2(b_end - cen[d]) for d in range(n_d)]
    qt = (qt[0] if n_d == 1 else jnp.concatenate(qt, axis=0)).astype(BF16)
    kt = (kt[0] if n_d == 1 else jnp.concatenate(kt, axis=0)).astype(BF16)
    o = _dot(qt, s_old.astype(BF16)) + _dot(p, v)
    e_rep = jnp.broadcast_to(b_end * (1.0 / BF16_ROWS), (BF16_ROWS, b_end.shape[1]))
    e_hi, e_lo = _split_bf16(e_rep)
    ones_col = jnp.ones((BF16_ROWS, LANE), BF16)
    dcol = (lax.dot_general(e_hi, ones_col, TN_DIMS, preferred_element_type=F32)
            + lax.dot_general(e_lo, ones_col, TN_DIMS, preferred_element_type=F32))
    decay = jnp.tile(jnp.exp2(dcol), (1, v.shape[1] // LANE))
    s_new = s_old * decay + lax.dot_general(kt, v, TN_DIMS, preferred_element_type=F32)
    return o, s_new


def _gla_kernel(*refs, diag, has_state):
    if has_state:
        q_ref, k_ref, v_ref, go_ref, la_ref, gn_ref, s0_ref, o_ref, sout_ref, s_ref = refs
    else:
        q_ref, k_ref, v_ref, go_ref, la_ref, gn_ref, o_ref, sout_ref, s_ref = refs
    i = pl.program_id(1)

    @pl.when(i == 0)
    def _():
        if has_state:
            s_ref[...] = s0_ref[0]
        else:
            s_ref[...] = jnp.zeros_like(s_ref)

    for h in range(GLA_HEADS):
        ck = pl.ds(h * GLA_HK, GLA_HK)
        cv = pl.ds(h * GLA_HV, GLA_HV)
        q = q_ref[0, :, ck].astype(F32)
        k = k_ref[0, :, ck].astype(F32)
        o, s_new = _gla_head(q, k, v_ref[0, :, cv], la_ref[0, :, ck], s_ref[h], diag=diag)
        s_ref[h] = s_new
        g = go_ref[0, :, cv].astype(F32)
        o_ref[0, :, cv] = (o * _rms_scale(o) * gn_ref[:, cv] * (g * _sigmoid(g))).astype(BF16)

    @pl.when(i == pl.num_programs(1) - 1)
    def _():
        sout_ref[0] = s_ref[...]


def _gla(proj, loga, gla_norm, state, *, tq, diag):
    bsz, t, _ = proj.shape
    has_state = state is not None
    grid = (bsz, t // tq)
    state_spec = pl.BlockSpec((1, GLA_HEADS, GLA_HK, GLA_HV), lambda b, i: (b, 0, 0, 0))
    in_specs = [
        pl.BlockSpec((1, tq, GLA_DK), lambda b, i: (b, i, OFF_Q // GLA_DK)),
        pl.BlockSpec((1, tq, GLA_DK), lambda b, i: (b, i, OFF_K // GLA_DK)),
        pl.BlockSpec((1, tq, GLA_DV), lambda b, i: (b, i, OFF_V // GLA_DV)),
        pl.BlockSpec((1, tq, GLA_DV), lambda b, i: (b, i, OFF_GO // GLA_DV)),
        pl.BlockSpec((1, tq, GLA_DK), lambda b, i: (b, i, 0)),
        pl.BlockSpec((1, GLA_DV), lambda b, i: (0, 0)),
    ]
    args = [proj, proj, proj, proj, loga, gla_norm]
    if has_state:
        in_specs.append(state_spec)
        args.append(state)
    return pl.pallas_call(
        functools.partial(_gla_kernel, diag=diag, has_state=has_state),
        grid=grid,
        in_specs=in_specs,
        out_specs=[
            pl.BlockSpec((1, tq, GLA_DV), lambda b, i: (b, i, 0)),
            state_spec,
        ],
        out_shape=[
            jax.ShapeDtypeStruct((bsz, t, GLA_DV), BF16),
            jax.ShapeDtypeStruct((bsz, GLA_HEADS, GLA_HK, GLA_HV), F32),
        ],
        scratch_shapes=[pltpu.VMEM((GLA_HEADS, GLA_HK, GLA_HV), F32)],
        compiler_params=pltpu.CompilerParams(
            dimension_semantics=("parallel", "arbitrary"),
            vmem_limit_bytes=VMEM_LIMIT),
        name="gla",
    )(*args)


def _mix_kernel(*refs, nb, tm, has_state):
    if has_state:
        (ba_ref, ca_ref, va_ref, ma_ref, mb_ref, og_ref, x_ref, cw_ref, st_ref,
         wa_ref, wb_ref, wo_ref, x1_ref, newc_ref, carry_ref) = refs
    else:
        (ba_ref, ca_ref, va_ref, ma_ref, mb_ref, og_ref, x_ref, cw_ref,
         wa_ref, wb_ref, wo_ref, x1_ref, newc_ref, carry_ref) = refs
    i = pl.program_id(1)
    rows = nb * tm
    first = i == 0
    cw = cw_ref[...]

    @pl.when(first)
    def _():
        carry_ref[...] = jnp.zeros_like(carry_ref)

    conv_parts = []
    for s in range(nb):
        cv = ca_ref[s].astype(F32) * va_ref[s].astype(F32)
        init = st_ref[s] if has_state else jnp.zeros((CONV_W - 1, D_MODEL), F32)
        halo = jnp.where(first, init, carry_ref[s, 0:2, :])
        conv_parts.append(_conv3(cv, halo[0:1], halo[1:2], cw))
        tail = cv[tm - 2:tm, :]
        carry_ref[s, 0:2, :] = tail
        newc_ref[s] = tail
    conv = conv_parts[0] if nb == 1 else jnp.concatenate(conv_parts, axis=0)

    b_a = ba_ref[...].reshape(rows, D_MODEL).astype(F32)
    y_a = _dot((b_a * conv).astype(BF16), wa_ref[...])
    y_b = _dot(og_ref[...].reshape(rows, D_MODEL), wb_ref[...])
    m_a = ma_ref[...].reshape(rows, D_MODEL).astype(F32)
    m_b = mb_ref[...].reshape(rows, D_MODEL).astype(F32)
    z = _sigmoid(m_a) * y_a + _sigmoid(m_b) * y_b
    x1 = x_ref[...].reshape(rows, D_MODEL) + _dot(z.astype(BF16), wo_ref[...])
    x1_ref[...] = x1.reshape(nb, tm, D_MODEL)


def _mix(proj, og, x, conv_w, state, w_a, w_b, w_o, *, nb, tm):
    bsz, t, _ = x.shape
    has_state = state is not None
    grid = (bsz // nb, t // tm)

    def col(off):
        return pl.BlockSpec((nb, tm, D_MODEL), lambda b, i: (b, i, off // D_MODEL))

    row_spec = pl.BlockSpec((nb, tm, D_MODEL), lambda b, i: (b, i, 0))
    w_spec = pl.BlockSpec((D_MODEL, D_MODEL), lambda b, i: (0, 0), pipeline_mode=pl.Buffered(1))
    in_specs = [col(OFF_BA), col(OFF_CA), col(OFF_VA), col(OFF_MA), col(OFF_MB), row_spec, row_spec,
                pl.BlockSpec((CONV_W, D_MODEL), lambda b, i: (0, 0))]
    args = [proj, proj, proj, proj, proj, og, x, conv_w]
    if has_state:
        in_specs.append(pl.BlockSpec((nb, CONV_W - 1, D_MODEL), lambda b, i: (b, 0, 0)))
        args.append(state)
    in_specs += [w_spec, w_spec, w_spec]
    args += [w_a, w_b, w_o]
    return pl.pallas_call(
        functools.partial(_mix_kernel, nb=nb, tm=tm, has_state=has_state),
        grid=grid,
        in_specs=in_specs,
        out_specs=[
            row_spec,
            pl.BlockSpec((nb, CONV_W - 1, D_MODEL), lambda b, i: (b, 0, 0)),
        ],
        out_shape=[
            jax.ShapeDtypeStruct((bsz, t, D_MODEL), F32),
            jax.ShapeDtypeStruct((bsz, CONV_W - 1, D_MODEL), F32),
        ],
        scratch_shapes=[pltpu.VMEM((nb, 8, D_MODEL), F32)],
        compiler_params=pltpu.CompilerParams(
            dimension_semantics=("parallel", "arbitrary"),
            vmem_limit_bytes=VMEM_LIMIT),
        name="mix",
    )(*args)


def _ffn_up_kernel(*refs, nb, tm, tn, has_state):
    if has_state:
        (x1_ref, g_ref, wv_ref, wg_ref, cwv_ref, cwg_ref, bv_ref, bg_ref, stv_ref, stg_ref,
         h_ref, nfv_ref, nfg_ref, hn_ref, cv_ref, cg_ref) = refs
    else:
        (x1_ref, g_ref, wv_ref, wg_ref, cwv_ref, cwg_ref, bv_ref, bg_ref,
         h_ref, nfv_ref, nfg_ref, hn_ref, cv_ref, cg_ref) = refs
    i = pl.program_id(1)
    j = pl.program_id(2)
    rows = nb * tm
    first = i == 0

    @pl.when(j == 0)
    def _():
        x1 = x1_ref[...].reshape(rows, D_MODEL)
        hn_ref[...] = (x1 * _rms_scale(x1) * g_ref[...]).astype(BF16)

    @pl.when(first)
    def _():
        cv_ref[j] = jnp.zeros(cv_ref.shape[1:], F32)
        cg_ref[j] = jnp.zeros(cg_ref.shape[1:], F32)

    n_parts = ROW_PARTS if nb == 1 and tm % (ROW_PARTS * SUBLANES) == 0 else 1
    part = rows // n_parts
    ug = [_dot(hn_ref[r * part:(r + 1) * part, :], wg_ref[...]) for r in range(n_parts)]
    uv = [_dot(hn_ref[r * part:(r + 1) * part, :], wv_ref[...]) for r in range(n_parts)]
    cwv = cwv_ref[...]
    cwg = cwg_ref[...]
    zero = jnp.zeros((CONV_W - 1, tn), F32)
    segs = [(0, r) for r in range(n_parts)] if n_parts > 1 else [(s, None) for s in range(nb)]
    act, halo_g, halo_v = {}, None, None
    for s, r in segs:
        ugs = ug[r] if r is not None else ug[0][s * tm:(s + 1) * tm]
        if not r:
            halo_g = jnp.where(first, stg_ref[s] if has_state else zero, cg_ref[j, s, 0:2, :])
        gate = _conv3(ugs, halo_g[0:1], halo_g[1:2], cwg) + bg_ref[...]
        act[(s, r)] = gate * _sigmoid(gate)
        halo_g = ugs[ugs.shape[0] - 2:, :]
        if r is None or r == n_parts - 1:
            cg_ref[j, s, 0:2, :] = halo_g
            nfg_ref[s, 0] = halo_g
    for s, r in segs:
        uvs = uv[r] if r is not None else uv[0][s * tm:(s + 1) * tm]
        if not r:
            halo_v = jnp.where(first, stv_ref[s] if has_state else zero, cv_ref[j, s, 0:2, :])
        val = _conv3(uvs, halo_v[0:1], halo_v[1:2], cwv) + bv_ref[...]
        halo_v = uvs[uvs.shape[0] - 2:, :]
        if r is None or r == n_parts - 1:
            cv_ref[j, s, 0:2, :] = halo_v
            nfv_ref[s, 0] = halo_v
        out = (act[(s, r)] * val).astype(BF16)
        if r is None:
            h_ref[s] = out
        else:
            h_ref[s, r * part:(r + 1) * part, :] = out


def _ffn_up(x1, g, w_up, conv_w, conv_b, state, *, nb, tm, tn):
    bsz, t, _ = x1.shape
    has_state = state is not None
    n_ct = D_FF // tn
    grid = (bsz // nb, t // tm, n_ct)
    in_specs = [
        pl.BlockSpec((nb, tm, D_MODEL), lambda b, i, j: (b, i, 0)),
        pl.BlockSpec((1, D_MODEL), lambda b, i, j: (0, 0)),
        pl.BlockSpec((D_MODEL, tn), lambda b, i, j: (0, j)),
        pl.BlockSpec((D_MODEL, tn), lambda b, i, j: (0, n_ct + j)),
        pl.BlockSpec((CONV_W, tn), lambda b, i, j: (0, j)),
        pl.BlockSpec((CONV_W, tn), lambda b, i, j: (0, n_ct + j)),
        pl.BlockSpec((1, tn), lambda b, i, j: (0, j)),
        pl.BlockSpec((1, tn), lambda b, i, j: (0, n_ct + j)),
    ]
    args = [x1, g, w_up, w_up, conv_w, conv_w, conv_b, conv_b]
    if has_state:
        in_specs += [
            pl.BlockSpec((nb, CONV_W - 1, tn), lambda b, i, j: (b, 0, j)),
            pl.BlockSpec((nb, CONV_W - 1, tn), lambda b, i, j: (b, 0, n_ct + j)),
        ]
        args += [state, state]
    tail_spec = pl.BlockSpec((nb, 1, CONV_W - 1, tn), lambda b, i, j: (b, i, 0, j))
    return pl.pallas_call(
        functools.partial(_ffn_up_kernel, nb=nb, tm=tm, tn=tn, has_state=has_state),
        grid=grid,
        in_specs=in_specs,
        out_specs=[
            pl.BlockSpec((nb, tm, tn), lambda b, i, j: (b, i, j)),
            tail_spec,
            tail_spec,
        ],
        out_shape=[
            jax.ShapeDtypeStruct((bsz, t, D_FF), BF16),
            jax.ShapeDtypeStruct((bsz, t // tm, CONV_W - 1, D_FF), F32),
            jax.ShapeDtypeStruct((bsz, t // tm, CONV_W - 1, D_FF), F32),
        ],
        scratch_shapes=[
            pltpu.VMEM((nb * tm, D_MODEL), BF16),
            pltpu.VMEM((n_ct, nb, 8, tn), F32),
            pltpu.VMEM((n_ct, nb, 8, tn), F32),
        ],
        compiler_params=pltpu.CompilerParams(
            dimension_semantics=("parallel", "arbitrary", "arbitrary"),
            vmem_limit_bytes=VMEM_LIMIT),
        name="ffn_up",
    )(*args)


def _ffn_down_kernel(h_ref, x1_ref, p_ref, wd_ref, gp_ref, wpg_ref, wple_ref, gf_ref, y_ref):
    x2 = x1_ref[...] + _dot(h_ref[...], wd_ref[...])
    pn = (x2 * _rms_scale(x2) * gp_ref[...]).astype(BF16)
    gate = _sigmoid(_dot(pn, wpg_ref[...]))
    pe = _dot(p_ref[...].astype(BF16), wple_ref[...])
    x3 = x2 + gate * pe
    y_ref[...] = x3 * _rms_scale(x3) * gf_ref[...]


def _ffn_down(h, x1, p, w_down, g_ple, w_pg, w_ple, g_final, *, tm):
    m = x1.shape[0]
    const = dict(pipeline_mode=pl.Buffered(1))
    return pl.pallas_call(
        _ffn_down_kernel,
        grid=(m // tm,),
        in_specs=[
            pl.BlockSpec((tm, D_FF), lambda i: (i, 0)),
            pl.BlockSpec((tm, D_MODEL), lambda i: (i, 0)),
            pl.BlockSpec((tm, D_PLE), lambda i: (i, 0)),
            pl.BlockSpec((D_FF, D_MODEL), lambda i: (0, 0), **const),
            pl.BlockSpec((1, D_MODEL), lambda i: (0, 0)),
            pl.BlockSpec((D_MODEL, D_MODEL), lambda i: (0, 0), **const),
            pl.BlockSpec((D_PLE, D_MODEL), lambda i: (0, 0), **const),
            pl.BlockSpec((1, D_MODEL), lambda i: (0, 0)),
        ],
        out_specs=pl.BlockSpec((tm, D_MODEL), lambda i: (i, 0)),
        out_shape=jax.ShapeDtypeStruct((m, D_MODEL), F32),
        compiler_params=pltpu.CompilerParams(
            dimension_semantics=("parallel",),
            vmem_limit_bytes=VMEM_LIMIT),
        name="ffn_down",
    )(h, x1, p, w_down, g_ple, w_pg, w_ple, g_final)


def _tiles(bsz, t):
    if t >= 1024:
        return dict(nb=1, tm_in=1024, tm_mix=256, tm_up=1024, tq=256, diag=128)
    return dict(nb=bsz, tm_in=t, tm_mix=t, tm_up=t, tq=t, diag=t)


def _layer(x, p, conv_state, gla_state, ffn_state, w, g_final):
    bsz, t, _ = x.shape
    cfg = _tiles(bsz, t)
    nb = cfg["nb"]
    proj, loga = _inproj(x, w["norm_mix"], w["w_all"], w["w_tail"], w["w_alr"], w["w_g2"], w["b_gate"],
                         nb=nb, tm=cfg["tm_in"], tn=1024)
    og, new_s = _gla(proj, loga, w["gla_norm"], gla_state, tq=cfg["tq"], diag=cfg["diag"])
    x1, new_conv = _mix(proj, og, x, w["conv_a_w"], conv_state, w["w_a_out"], w["w_b_out"], w["w_o"],
                        nb=nb, tm=cfg["tm_mix"])
    h, nf_v, nf_g = _ffn_up(x1, w["norm_ffn"], w["w_up"], w["ffn_conv_w"], w["ffn_conv_b"], ffn_state,
                            nb=nb, tm=cfg["tm_up"], tn=512)
    m = bsz * t
    y = _ffn_down(h.reshape(m, D_FF), x1.reshape(m, D_MODEL), p.reshape(m, D_PLE),
                  w["w_down"], w["norm_ple"], w["w_ple_gate"], w["w_ple"], g_final, tm=256)
    new_ffn = jnp.concatenate([nf_v[:, -1], nf_g[:, -1]], axis=-1)
    return y.reshape(bsz, t, D_MODEL), new_conv, new_s, new_ffn


def kernel(x_prompt, x_sample, p_prompt, p_sample, state_conv_a, state_gla, state_ffn_conv, norm_mix, w_in, conv_a_w, w_a_out, w_gate2, b_gate, gla_norm, w_b_out, w_o, norm_ffn, w_up, ffn_conv_w, ffn_conv_b, w_down, norm_ple, w_ple_gate, w_ple, norm_final):
    depth = w_in.shape[0]
    assert depth == 1, "the final norm is fused into the last layer; one layer supported"
    i = 0
    w_in_i = w_in[i]
    w_alr = jnp.pad(w_in_i[:, ALR_OFF:ALR_OFF + GATE_RANK], ((0, 0), (0, LANE - GATE_RANK)))
    w = dict(
        norm_mix=norm_mix[i][None],
        w_all=w_in_i.astype(BF16),
        w_tail=w_in_i[:, ALR_OFF + GATE_RANK:].astype(BF16),
        w_alr=w_alr.astype(BF16),
        w_g2=jnp.pad(w_gate2[i], ((0, LANE - GATE_RANK), (0, 0))).astype(BF16),
        b_gate=b_gate[i][None],
        gla_norm=gla_norm[i][None],
        conv_a_w=conv_a_w[i],
        w_a_out=w_a_out[i].astype(BF16),
        w_b_out=w_b_out[i].astype(BF16),
        w_o=w_o[i].astype(BF16),
        norm_ffn=norm_ffn[i][None],
        w_up=w_up[i].astype(BF16),
        ffn_conv_w=ffn_conv_w[i],
        ffn_conv_b=ffn_conv_b[i][None],
        w_down=w_down[i].astype(BF16),
        norm_ple=norm_ple[i][None],
        w_ple_gate=w_ple_gate[i].astype(BF16),
        w_ple=w_ple[i].astype(BF16),
    )
    g_final = norm_final[None]
    yp, c1, s1, f1 = _layer(x_prompt, p_prompt[i], None, None, None, w, g_final)
    ys, c2, s2, f2 = _layer(x_sample, p_sample[i], state_conv_a[i], state_gla[i], state_ffn_conv[i],
                            w, g_final)
    return (yp, ys, c1[None], s1[None], f1[None], c2[None], s2[None], f2[None])
```

```python
import functools
import math

import jax
import jax.numpy as jnp
from jax import lax
from jax.experimental import pallas as pl
from jax.experimental.pallas import tpu as pltpu

F32 = jnp.float32
BF16 = jnp.bfloat16

D_MODEL = 2048
CONV_W = 3
GLA_HEADS = 4
GLA_DK = D_MODEL // 2
GLA_DV = D_MODEL
GLA_HK = GLA_DK // GLA_HEADS
GLA_HV = GLA_DV // GLA_HEADS
GATE_RANK = 16
GATE_NORMALIZER = 16.0
D_FF = 5632
D_PLE = 256
EPS = 1e-6

D_MAIN = 3 * D_MODEL + 2 * GLA_DK + 2 * GLA_DV + 2 * D_MODEL
OFF_BA, OFF_CA, OFF_VA = 0, D_MODEL, 2 * D_MODEL
OFF_Q = 3 * D_MODEL
OFF_K = OFF_Q + GLA_DK
OFF_V = OFF_K + GLA_DK
OFF_GO = OFF_V + GLA_DV
OFF_MA = OFF_GO + GLA_DV
OFF_MB = OFF_MA + D_MODEL
ALR_OFF = OFF_MA
LANE = 128
SUBLANES = 8
ROW_PARTS = 4
BF16_ROWS = 16
VMEM_LIMIT = 56 * 1024 * 1024

LOG2_E = math.log2(math.e)
Q_LOG2_SCALE = -0.5 * math.log2(GLA_HK)

NT_DIMS = (((1,), (1,)), ((), ()))
TN_DIMS = (((0,), (0,)), ((), ()))


def _rms_scale(x):
    return lax.rsqrt(jnp.mean(x * x, axis=-1, keepdims=True) + EPS)


def _sigmoid(x):
    return 1.0 / (1.0 + jnp.exp(-x))


def _log_sigmoid(z):
    return jnp.minimum(z, 0.0) - jnp.log(1.0 + jnp.exp(-jnp.abs(z)))


def _dot(a, b):
    return jnp.dot(a, b, preferred_element_type=F32)


def _split_bf16(x):
    hi = x.astype(BF16)
    return hi, (x - hi.astype(F32)).astype(BF16)


def _conv3(u, h0, h1, w):
    r1 = pltpu.roll(u, 1, axis=0)
    r2 = pltpu.roll(u, 2, axis=0)
    rows = lax.broadcasted_iota(jnp.int32, (SUBLANES, 1), 0)
    top1 = jnp.where(rows == 0, h1, r1[:SUBLANES])
    top2 = jnp.where(rows == 0, h0, jnp.where(rows == 1, h1, r2[:SUBLANES]))
    sh1 = jnp.concatenate([top1, r1[SUBLANES:]], axis=0)
    sh2 = jnp.concatenate([top2, r2[SUBLANES:]], axis=0)
    return w[0:1] * sh2 + w[1:2] * sh1 + w[2:3] * u


def _inproj_kernel(x_ref, g_ref, whead_ref, wtail_ref, walr_ref, wg2_ref, bg_ref, proj_ref, loga_ref,
                   xn_ref, *, nb, tm, n_head):
    j = pl.program_id(2)
    rows = nb * tm

    @pl.when(j == 0)
    def _():
        x = x_ref[...].reshape(rows, D_MODEL)
        xn = (x * _rms_scale(x) * g_ref[...]).astype(BF16)
        xn_ref[...] = xn
        a_lr = _dot(xn, walr_ref[...])
        z = _dot(a_lr.astype(BF16), wg2_ref[...]) + bg_ref[...]
        loga_ref[...] = (_log_sigmoid(z) * (1.0 / GATE_NORMALIZER)).reshape(nb, tm, GLA_DK)

    tn = proj_ref.shape[-1]

    @pl.when(j < n_head)
    def _():
        proj_ref[...] = _dot(xn_ref[...], whead_ref[...]).astype(BF16).reshape(nb, tm, tn)

    @pl.when(j >= n_head)
    def _():
        proj_ref[...] = _dot(xn_ref[...], wtail_ref[...]).astype(BF16).reshape(nb, tm, tn)


def _inproj(x, g, w_all, w_tail, w_alr, w_g2, b_gate, *, nb, tm, tn):
    bsz, t, _ = x.shape
    n_head = ALR_OFF // tn
    grid = (bsz // nb, t // tm, D_MAIN // tn)
    return pl.pallas_call(
        functools.partial(_inproj_kernel, nb=nb, tm=tm, n_head=n_head),
        grid=grid,
        in_specs=[
            pl.BlockSpec((nb, tm, D_MODEL), lambda b, i, j: (b, i, 0)),
            pl.BlockSpec((1, D_MODEL), lambda b, i, j: (0, 0)),
            pl.BlockSpec((D_MODEL, tn), lambda b, i, j: (0, jnp.minimum(j, n_head - 1))),
            pl.BlockSpec((D_MODEL, tn), lambda b, i, j: (0, jnp.maximum(j - n_head, 0))),
            pl.BlockSpec((D_MODEL, LANE), lambda b, i, j: (0, 0)),
            pl.BlockSpec((LANE, GLA_DK), lambda b, i, j: (0, 0)),
            pl.BlockSpec((1, GLA_DK), lambda b, i, j: (0, 0)),
        ],
        out_specs=[
            pl.BlockSpec((nb, tm, tn), lambda b, i, j: (b, i, j)),
            pl.BlockSpec((nb, tm, GLA_DK), lambda b, i, j: (b, i, 0)),
        ],
        out_shape=[
            jax.ShapeDtypeStruct((bsz, t, D_MAIN), BF16),
            jax.ShapeDtypeStruct((bsz, t, GLA_DK), F32),
        ],
        scratch_shapes=[pltpu.VMEM((nb * tm, D_MODEL), BF16)],
        compiler_params=pltpu.CompilerParams(
            dimension_semantics=("parallel", "parallel", "arbitrary"),
            vmem_limit_bytes=VMEM_LIMIT),
        name="inproj",
    )(x, g, w_all, w_tail, w_alr, w_g2, b_gate)


def _gla_head(q, k, v, la, s_old, *, diag):
    rows = q.shape[0]
    n_d = rows // diag
    r_i = lax.broadcasted_iota(jnp.int32, (rows, rows), 0)
    c_i = lax.broadcasted_iota(jnp.int32, (rows, rows), 1)
    tril = (r_i >= c_i).astype(BF16)
    la_hi, la_lo = _split_bf16(la * LOG2_E)
    b = _dot(tril, la_hi) + _dot(tril, la_lo)
    b_end = b[rows - 1:rows, :]

    causal_d = (lax.broadcasted_iota(jnp.int32, (diag, diag), 0)
                >= lax.broadcasted_iota(jnp.int32, (diag, diag), 1))
    a_c, k_c, cen = [], [], []
    for d in range(n_d):
        lo = d * diag
        mid = lo + diag // 2
        c = b[mid - 1:mid, :]
        bd = b[lo:lo + diag, :]
        cen.append(c)
        a_c.append(q[lo:lo + diag, :] * jnp.exp2(bd - (c - Q_LOG2_SCALE)))
        k_c.append(k[lo:lo + diag, :] * jnp.exp2(c - bd))

    def block(lo_d, hi_d):
        if hi_d - lo_d == 1:
            s = lax.dot_general(a_c[lo_d].astype(BF16), k_c[lo_d].astype(BF16), NT_DIMS,
                                preferred_element_type=F32)
            return jnp.where(causal_d, s, 0.0)
        mid_d = (lo_d + hi_d) // 2
        ref = b[mid_d * diag - 1:mid_d * diag, :]
        lhs = [a_c[d] * jnp.exp2(cen[d] - ref) for d in range(mid_d, hi_d)]
        rhs = [k_c[d] * jnp.exp2(ref - cen[d]) for d in range(lo_d, mid_d)]
        lhs = (lhs[0] if len(lhs) == 1 else jnp.concatenate(lhs, axis=0)).astype(BF16)
        rhs = (rhs[0] if len(rhs) == 1 else jnp.concatenate(rhs, axis=0)).astype(BF16)
        off = lax.dot_general(lhs, rhs, NT_DIMS, preferred_element_type=F32)
        top = jnp.concatenate([block(lo_d, mid_d), jnp.zeros_like(off)], axis=1)
        bot = jnp.concatenate([off, block(mid_d, hi_d)], axis=1)
        return jnp.concatenate([top, bot], axis=0)

    p = block(0, n_d).astype(BF16)
    qt = [a_c[d] * jnp.exp2(cen[d]) for d in range(n_d)]
    kt = [k_c[d] * jnp.exp2(b_end - cen[d]) for d in range(n_d)]
    qt = (qt[0] if n_d == 1 else jnp.concatenate(qt, axis=0)).astype(BF16)
    kt = (kt[0] if n_d == 1 else jnp.concatenate(kt, axis=0)).astype(BF16)
    o = _dot(qt, s_old.astype(BF16)) + _dot(p, v)
    e_rep = jnp.broadcast_to(b_end * (1.0 / BF16_ROWS), (BF16_ROWS, b_end.shape[1]))
    e_hi, e_lo = _split_bf16(e_rep)
    ones_col = jnp.ones((BF16_ROWS, LANE), BF16)
    dcol = (lax.dot_general(e_hi, ones_col, TN_DIMS, preferred_element_type=F32)
            + lax.dot_general(e_lo, ones_col, TN_DIMS, preferred_element_type=F32))
    decay = jnp.tile(jnp.exp2(dcol), (1, v.shape[1] // LANE))
    s_new = s_old * decay + lax.dot_general(kt, v, TN_DIMS, preferred_element_type=F32)
    return o, s_new


def _gla_kernel(*refs, block, diag, has_state):
    if has_state:
        q_ref, k_ref, v_ref, go_ref, la_ref, gn_ref, s0_ref, o_ref, sout_ref, s_ref = refs
    else:
        q_ref, k_ref, v_ref, go_ref, la_ref, gn_ref, o_ref, sout_ref, s_ref = refs
    i = pl.program_id(1)

    @pl.when(i == 0)
    def _():
        if has_state:
            s_ref[...] = s0_ref[0]
        else:
            s_ref[...] = jnp.zeros_like(s_ref)

    for r0 in range(0, q_ref.shape[1], block):
        rs = pl.ds(r0, block)
        for h in range(GLA_HEADS):
            ck = pl.ds(h * GLA_HK, GLA_HK)
            cv = pl.ds(h * GLA_HV, GLA_HV)
            q = q_ref[0, rs, ck].astype(F32)
            k = k_ref[0, rs, ck].astype(F32)
            o, s_new = _gla_head(q, k, v_ref[0, rs, cv], la_ref[0, rs, ck], s_ref[h], diag=diag)
            s_ref[h] = s_new
            g = go_ref[0, rs, cv].astype(F32)
            o_ref[0, rs, cv] = (o * _rms_scale(o) * gn_ref[:, cv] * (g * _sigmoid(g))).astype(BF16)

    @pl.when(i == pl.num_programs(1) - 1)
    def _():
        sout_ref[0] = s_ref[...]


def _gla(proj, loga, gla_norm, state, *, tq, block, diag):
    bsz, t, _ = proj.shape
    has_state = state is not None
    grid = (bsz, t // tq)
    state_spec = pl.BlockSpec((1, GLA_HEADS, GLA_HK, GLA_HV), lambda b, i: (b, 0, 0, 0))
    in_specs = [
        pl.BlockSpec((1, tq, GLA_DK), lambda b, i: (b, i, OFF_Q // GLA_DK)),
        pl.BlockSpec((1, tq, GLA_DK), lambda b, i: (b, i, OFF_K // GLA_DK)),
        pl.BlockSpec((1, tq, GLA_DV), lambda b, i: (b, i, OFF_V // GLA_DV)),
        pl.BlockSpec((1, tq, GLA_DV), lambda b, i: (b, i, OFF_GO // GLA_DV)),
        pl.BlockSpec((1, tq, GLA_DK), lambda b, i: (b, i, 0)),
        pl.BlockSpec((1, GLA_DV), lambda b, i: (0, 0)),
    ]
    args = [proj, proj, proj, proj, loga, gla_norm]
    if has_state:
        in_specs.append(state_spec)
        args.append(state)
    return pl.pallas_call(
        functools.partial(_gla_kernel, block=block, diag=diag, has_state=has_state),
        grid=grid,
        in_specs=in_specs,
        out_specs=[
            pl.BlockSpec((1, tq, GLA_DV), lambda b, i: (b, i, 0)),
            state_spec,
        ],
        out_shape=[
            jax.ShapeDtypeStruct((bsz, t, GLA_DV), BF16),
            jax.ShapeDtypeStruct((bsz, GLA_HEADS, GLA_HK, GLA_HV), F32),
        ],
        scratch_shapes=[pltpu.VMEM((GLA_HEADS, GLA_HK, GLA_HV), F32)],
        compiler_params=pltpu.CompilerParams(
            dimension_semantics=("parallel", "arbitrary"),
            vmem_limit_bytes=VMEM_LIMIT),
        name="gla",
    )(*args)


def _mix_kernel(*refs, nb, tm, has_state):
    if has_state:
        (ba_ref, ca_ref, va_ref, ma_ref, mb_ref, og_ref, x_ref, cw_ref, st_ref,
         wa_ref, wb_ref, wo_ref, x1_ref, newc_ref, carry_ref) = refs
    else:
        (ba_ref, ca_ref, va_ref, ma_ref, mb_ref, og_ref, x_ref, cw_ref,
         wa_ref, wb_ref, wo_ref, x1_ref, newc_ref, carry_ref) = refs
    i = pl.program_id(1)
    rows = nb * tm
    first = i == 0
    cw = cw_ref[...]

    @pl.when(first)
    def _():
        carry_ref[...] = jnp.zeros_like(carry_ref)

    conv_parts = []
    for s in range(nb):
        cv = ca_ref[s].astype(F32) * va_ref[s].astype(F32)
        init = st_ref[s] if has_state else jnp.zeros((CONV_W - 1, D_MODEL), F32)
        halo = jnp.where(first, init, carry_ref[s, 0:2, :])
        conv_parts.append(_conv3(cv, halo[0:1], halo[1:2], cw))
        tail = cv[tm - 2:tm, :]
        carry_ref[s, 0:2, :] = tail
        newc_ref[s] = tail
    conv = conv_parts[0] if nb == 1 else jnp.concatenate(conv_parts, axis=0)

    b_a = ba_ref[...].reshape(rows, D_MODEL).astype(F32)
    y_a = _dot((b_a * conv).astype(BF16), wa_ref[...])
    y_b = _dot(og_ref[...].reshape(rows, D_MODEL), wb_ref[...])
    m_a = ma_ref[...].reshape(rows, D_MODEL).astype(F32)
    m_b = mb_ref[...].reshape(rows, D_MODEL).astype(F32)
    z = _sigmoid(m_a) * y_a + _sigmoid(m_b) * y_b
    x1 = x_ref[...].reshape(rows, D_MODEL) + _dot(z.astype(BF16), wo_ref[...])
    x1_ref[...] = x1.reshape(nb, tm, D_MODEL)


def _mix(proj, og, x, conv_w, state, w_a, w_b, w_o, *, nb, tm):
    bsz, t, _ = x.shape
    has_state = state is not None
    grid = (bsz // nb, t // tm)

    def col(off):
        return pl.BlockSpec((nb, tm, D_MODEL), lambda b, i: (b, i, off // D_MODEL))

    row_spec = pl.BlockSpec((nb, tm, D_MODEL), lambda b, i: (b, i, 0))
    w_spec = pl.BlockSpec((D_MODEL, D_MODEL), lambda b, i: (0, 0), pipeline_mode=pl.Buffered(1))
    in_specs = [col(OFF_BA), col(OFF_CA), col(OFF_VA), col(OFF_MA), col(OFF_MB), row_spec, row_spec,
                pl.BlockSpec((CONV_W, D_MODEL), lambda b, i: (0, 0))]
    args = [proj, proj, proj, proj, proj, og, x, conv_w]
    if has_state:
        in_specs.append(pl.BlockSpec((nb, CONV_W - 1, D_MODEL), lambda b, i: (b, 0, 0)))
        args.append(state)
    in_specs += [w_spec, w_spec, w_spec]
    args += [w_a, w_b, w_o]
    return pl.pallas_call(
        functools.partial(_mix_kernel, nb=nb, tm=tm, has_state=has_state),
        grid=grid,
        in_specs=in_specs,
        out_specs=[
            row_spec,
            pl.BlockSpec((nb, CONV_W - 1, D_MODEL), lambda b, i: (b, 0, 0)),
        ],
        out_shape=[
            jax.ShapeDtypeStruct((bsz, t, D_MODEL), F32),
            jax.ShapeDtypeStruct((bsz, CONV_W - 1, D_MODEL), F32),
        ],
        scratch_shapes=[pltpu.VMEM((nb, 8, D_MODEL), F32)],
        compiler_params=pltpu.CompilerParams(
            dimension_semantics=("parallel", "arbitrary"),
            vmem_limit_bytes=VMEM_LIMIT),
        name="mix",
    )(*args)


def _ffn_up_kernel(*refs, nb, tm, tn, has_state):
    if has_state:
        (x1_ref, g_ref, wv_ref, wg_ref, cwv_ref, cwg_ref, bv_ref, bg_ref, stv_ref, stg_ref,
         h_ref, nfv_ref, nfg_ref, hn_ref, cv_ref, cg_ref) = refs
    else:
        (x1_ref, g_ref, wv_ref, wg_ref, cwv_ref, cwg_ref, bv_ref, bg_ref,
         h_ref, nfv_ref, nfg_ref, hn_ref, cv_ref, cg_ref) = refs
    i = pl.program_id(1)
    j = pl.program_id(2)
    rows = nb * tm
    first = i == 0

    @pl.when(j == 0)
    def _():
        x1 = x1_ref[...].reshape(rows, D_MODEL)
        hn_ref[...] = (x1 * _rms_scale(x1) * g_ref[...]).astype(BF16)

    @pl.when(first)
    def _():
        cv_ref[j] = jnp.zeros(cv_ref.shape[1:], F32)
        cg_ref[j] = jnp.zeros(cg_ref.shape[1:], F32)

    n_parts = ROW_PARTS if nb == 1 and tm % (ROW_PARTS * SUBLANES) == 0 else 1
    part = rows // n_parts
    ug = [_dot(hn_ref[r * part:(r + 1) * part, :], wg_ref[...]) for r in range(n_parts)]
    uv = [_dot(hn_ref[r * part:(r + 1) * part, :], wv_ref[...]) for r in range(n_parts)]
    cwv = cwv_ref[...]
    cwg = cwg_ref[...]
    zero = jnp.zeros((CONV_W - 1, tn), F32)
    segs = [(0, r) for r in range(n_parts)] if n_parts > 1 else [(s, None) for s in range(nb)]
    act, halo_g, halo_v = {}, None, None
    for s, r in segs:
        ugs = ug[r] if r is not None else ug[0][s * tm:(s + 1) * tm]
        if not r:
            halo_g = jnp.where(first, stg_ref[s] if has_state else zero, cg_ref[j, s, 0:2, :])
        gate = _conv3(ugs, halo_g[0:1], halo_g[1:2], cwg) + bg_ref[...]
        act[(s, r)] = gate * _sigmoid(gate)
        halo_g = ugs[ugs.shape[0] - 2:, :]
        if r is None or r == n_parts - 1:
            cg_ref[j, s, 0:2, :] = halo_g
            nfg_ref[s, 0] = halo_g
    for s, r in segs:
        uvs = uv[r] if r is not None else uv[0][s * tm:(s + 1) * tm]
        if not r:
            halo_v = jnp.where(first, stv_ref[s] if has_state else zero, cv_ref[j, s, 0:2, :])
        val = _conv3(uvs, halo_v[0:1], halo_v[1:2], cwv) + bv_ref[...]
        halo_v = uvs[uvs.shape[0] - 2:, :]
        if r is None or r == n_parts - 1:
            cv_ref[j, s, 0:2, :] = halo_v
            nfv_ref[s, 0] = halo_v
        out = (act[(s, r)] * val).astype(BF16)
        if r is None:
            h_ref[s] = out
        else:
            h_ref[s, r * part:(r + 1) * part, :] = out


def _ffn_up(x1, g, w_up, conv_w, conv_b, state, *, nb, tm, tn):
    bsz, t, _ = x1.shape
    has_state = state is not None
    n_ct = D_FF // tn
    grid = (bsz // nb, t // tm, n_ct)
    in_specs = [
        pl.BlockSpec((nb, tm, D_MODEL), lambda b, i, j: (b, i, 0)),
        pl.BlockSpec((1, D_MODEL), lambda b, i, j: (0, 0)),
        pl.BlockSpec((D_MODEL, tn), lambda b, i, j: (0, j)),
        pl.BlockSpec((D_MODEL, tn), lambda b, i, j: (0, n_ct + j)),
        pl.BlockSpec((CONV_W, tn), lambda b, i, j: (0, j)),
        pl.BlockSpec((CONV_W, tn), lambda b, i, j: (0, n_ct + j)),
        pl.BlockSpec((1, tn), lambda b, i, j: (0, j)),
        pl.BlockSpec((1, tn), lambda b, i, j: (0, n_ct + j)),
    ]
    args = [x1, g, w_up, w_up, conv_w, conv_w, conv_b, conv_b]
    if has_state:
        in_specs += [
            pl.BlockSpec((nb, CONV_W - 1, tn), lambda b, i, j: (b, 0, j)),
            pl.BlockSpec((nb, CONV_W - 1, tn), lambda b, i, j: (b, 0, n_ct + j)),
        ]
        args += [state, state]
    tail_spec = pl.BlockSpec((nb, 1, CONV_W - 1, tn), lambda b, i, j: (b, i, 0, j))
    return pl.pallas_call(
        functools.partial(_ffn_up_kernel, nb=nb, tm=tm, tn=tn, has_state=has_state),
        grid=grid,
        in_specs=in_specs,
        out_specs=[
            pl.BlockSpec((nb, tm, tn), lambda b, i, j: (b, i, j)),
            tail_spec,
            tail_spec,
        ],
        out_shape=[
            jax.ShapeDtypeStruct((bsz, t, D_FF), BF16),
            jax.ShapeDtypeStruct((bsz, t // tm, CONV_W - 1, D_FF), F32),
            jax.ShapeDtypeStruct((bsz, t // tm, CONV_W - 1, D_FF), F32),
        ],
        scratch_shapes=[
            pltpu.VMEM((nb * tm, D_MODEL), BF16),
            pltpu.VMEM((n_ct, nb, 8, tn), F32),
            pltpu.VMEM((n_ct, nb, 8, tn), F32),
        ],
        compiler_params=pltpu.CompilerParams(
            dimension_semantics=("parallel", "arbitrary", "arbitrary"),
            vmem_limit_bytes=VMEM_LIMIT),
        name="ffn_up",
    )(*args)


def _ffn_down_kernel(h_ref, x1_ref, p_ref, wd_ref, gp_ref, wpg_ref, wple_ref, gf_ref, y_ref):
    x2 = x1_ref[...] + _dot(h_ref[...], wd_ref[...])
    pn = (x2 * _rms_scale(x2) * gp_ref[...]).astype(BF16)
    gate = _sigmoid(_dot(pn, wpg_ref[...]))
    pe = _dot(p_ref[...].astype(BF16), wple_ref[...])
    x3 = x2 + gate * pe
    y_ref[...] = x3 * _rms_scale(x3) * gf_ref[...]


def _ffn_down(h, x1, p, w_down, g_ple, w_pg, w_ple, g_final, *, tm):
    m = x1.shape[0]
    const = dict(pipeline_mode=pl.Buffered(1))
    return pl.pallas_call(
        _ffn_down_kernel,
        grid=(m // tm,),
        in_specs=[
            pl.BlockSpec((tm, D_FF), lambda i: (i, 0)),
            pl.BlockSpec((tm, D_MODEL), lambda i: (i, 0)),
            pl.BlockSpec((tm, D_PLE), lambda i: (i, 0)),
            pl.BlockSpec((D_FF, D_MODEL), lambda i: (0, 0), **const),
            pl.BlockSpec((1, D_MODEL), lambda i: (0, 0)),
            pl.BlockSpec((D_MODEL, D_MODEL), lambda i: (0, 0), **const),
            pl.BlockSpec((D_PLE, D_MODEL), lambda i: (0, 0), **const),
            pl.BlockSpec((1, D_MODEL), lambda i: (0, 0)),
        ],
        out_specs=pl.BlockSpec((tm, D_MODEL), lambda i: (i, 0)),
        out_shape=jax.ShapeDtypeStruct((m, D_MODEL), F32),
        compiler_params=pltpu.CompilerParams(
            dimension_semantics=("parallel",),
            vmem_limit_bytes=VMEM_LIMIT),
        name="ffn_down",
    )(h, x1, p, w_down, g_ple, w_pg, w_ple, g_final)


def _tiles(bsz, t):
    if t >= 1024:
        return dict(nb=1, tm_in=1024, tm_mix=256, tm_up=1024, tq=512, block=256, diag=128)
    return dict(nb=bsz, tm_in=t, tm_mix=t, tm_up=t, tq=t, block=t, diag=t)


def _layer(x, p, conv_state, gla_state, ffn_state, w, g_final):
    bsz, t, _ = x.shape
    cfg = _tiles(bsz, t)
    nb = cfg["nb"]
    proj, loga = _inproj(x, w["norm_mix"], w["w_all"], w["w_tail"], w["w_alr"], w["w_g2"], w["b_gate"],
                         nb=nb, tm=cfg["tm_in"], tn=1024)
    og, new_s = _gla(proj, loga, w["gla_norm"], gla_state, tq=cfg["tq"], block=cfg["block"],
                      diag=cfg["diag"])
    x1, new_conv = _mix(proj, og, x, w["conv_a_w"], conv_state, w["w_a_out"], w["w_b_out"], w["w_o"],
                        nb=nb, tm=cfg["tm_mix"])
    h, nf_v, nf_g = _ffn_up(x1, w["norm_ffn"], w["w_up"], w["ffn_conv_w"], w["ffn_conv_b"], ffn_state,
                            nb=nb, tm=cfg["tm_up"], tn=512)
    m = bsz * t
    y = _ffn_down(h.reshape(m, D_FF), x1.reshape(m, D_MODEL), p.reshape(m, D_PLE),
                  w["w_down"], w["norm_ple"], w["w_ple_gate"], w["w_ple"], g_final, tm=256)
    new_ffn = jnp.concatenate([nf_v[:, -1], nf_g[:, -1]], axis=-1)
    return y.reshape(bsz, t, D_MODEL), new_conv, new_s, new_ffn


def kernel(x_prompt, x_sample, p_prompt, p_sample, state_conv_a, state_gla, state_ffn_conv, norm_mix, w_in, conv_a_w, w_a_out, w_gate2, b_gate, gla_norm, w_b_out, w_o, norm_ffn, w_up, ffn_conv_w, ffn_conv_b, w_down, norm_ple, w_ple_gate, w_ple, norm_final):
    depth = w_in.shape[0]
    assert depth == 1, "the final norm is fused into the last layer; one layer supported"
    i = 0
    w_in_i = w_in[i]
    w_alr = jnp.pad(w_in_i[:, ALR_OFF:ALR_OFF + GATE_RANK], ((0, 0), (0, LANE - GATE_RANK)))
    w = dict(
        norm_mix=norm_mix[i][None],
        w_all=w_in_i.astype(BF16),
        w_tail=w_in_i[:, ALR_OFF + GATE_RANK:].astype(BF16),
        w_alr=w_alr.astype(BF16),
        w_g2=jnp.pad(w_gate2[i], ((0, LANE - GATE_RANK), (0, 0))).astype(BF16),
        b_gate=b_gate[i][None],
        gla_norm=gla_norm[i][None],
        conv_a_w=conv_a_w[i],
        w_a_out=w_a_out[i].astype(BF16),
        w_b_out=w_b_out[i].astype(BF16),
        w_o=w_o[i].astype(BF16),
        norm_ffn=norm_ffn[i][None],
        w_up=w_up[i].astype(BF16),
        ffn_conv_w=ffn_conv_w[i],
        ffn_conv_b=ffn_conv_b[i][None],
        w_down=w_down[i].astype(BF16),
        norm_ple=norm_ple[i][None],
        w_ple_gate=w_ple_gate[i].astype(BF16),
        w_ple=w_ple[i].astype(BF16),
    )
    g_final = norm_final[None]
    yp, c1, s1, f1 = _layer(x_prompt, p_prompt[i], None, None, None, w, g_final)
    ys, c2, s2, f2 = _layer(x_sample, p_sample[i], state_conv_a[i], state_gla[i], state_ffn_conv[i],
                            w, g_final)
    return (yp, ys, c1[None], s1[None], f1[None], c2[None], s2[None], f2[None])
```

```python
import functools
import math

import jax
import jax.numpy as jnp
from jax import lax
from jax.experimental import pallas as pl
from jax.experimental.pallas import tpu as pltpu

F32 = jnp.float32
BF16 = jnp.bfloat16

D_MODEL = 2048
CONV_W = 3
GLA_HEADS = 4
GLA_DK = D_MODEL // 2
GLA_DV = D_MODEL
GLA_HK = GLA_DK // GLA_HEADS
GLA_HV = GLA_DV // GLA_HEADS
GATE_RANK = 16
GATE_NORMALIZER = 16.0
D_FF = 5632
D_PLE = 256
EPS = 1e-6

D_MAIN = 3 * D_MODEL + 2 * GLA_DK + 2 * GLA_DV + 2 * D_MODEL
OFF_BA, OFF_CA, OFF_VA = 0, D_MODEL, 2 * D_MODEL
OFF_Q = 3 * D_MODEL
OFF_K = OFF_Q + GLA_DK
OFF_V = OFF_K + GLA_DK
OFF_GO = OFF_V + GLA_DV
OFF_MA = OFF_GO + GLA_DV
OFF_MB = OFF_MA + D_MODEL
ALR_OFF = OFF_MA
LANE = 128
SUBLANES = 8
ROW_PARTS = 2
BF16_ROWS = 16
VMEM_LIMIT = 56 * 1024 * 1024

LOG2_E = math.log2(math.e)
Q_LOG2_SCALE = -0.5 * math.log2(GLA_HK)

NT_DIMS = (((1,), (1,)), ((), ()))
TN_DIMS = (((0,), (0,)), ((), ()))


def _rms_scale(x):
    return lax.rsqrt(jnp.mean(x * x, axis=-1, keepdims=True) + EPS)


def _sigmoid(x):
    return 1.0 / (1.0 + jnp.exp(-x))


def _log_sigmoid(z):
    return jnp.minimum(z, 0.0) - jnp.log(1.0 + jnp.exp(-jnp.abs(z)))


def _dot(a, b):
    return jnp.dot(a, b, preferred_element_type=F32)


def _split_bf16(x):
    hi = x.astype(BF16)
    return hi, (x - hi.astype(F32)).astype(BF16)


def _conv3(u, h0, h1, w):
    r1 = pltpu.roll(u, 1, axis=0)
    r2 = pltpu.roll(u, 2, axis=0)
    rows = lax.broadcasted_iota(jnp.int32, (SUBLANES, 1), 0)
    top1 = jnp.where(rows == 0, h1, r1[:SUBLANES])
    top2 = jnp.where(rows == 0, h0, jnp.where(rows == 1, h1, r2[:SUBLANES]))
    sh1 = jnp.concatenate([top1, r1[SUBLANES:]], axis=0)
    sh2 = jnp.concatenate([top2, r2[SUBLANES:]], axis=0)
    return w[0:1] * sh2 + w[1:2] * sh1 + w[2:3] * u


def _inproj_kernel(x_ref, g_ref, whead_ref, wtail_ref, walr_ref, wg2_ref, bg_ref, proj_ref, loga_ref,
                   xn_ref, *, nb, tm, n_head):
    j = pl.program_id(2)
    rows = nb * tm

    @pl.when(j == 0)
    def _():
        x = x_ref[...].reshape(rows, D_MODEL)
        xn = (x * _rms_scale(x) * g_ref[...]).astype(BF16)
        xn_ref[...] = xn
        a_lr = _dot(xn, walr_ref[...])
        z = _dot(a_lr.astype(BF16), wg2_ref[...]) + bg_ref[...]
        loga_ref[...] = (_log_sigmoid(z) * (1.0 / GATE_NORMALIZER)).reshape(nb, tm, GLA_DK)

    tn = proj_ref.shape[-1]

    @pl.when(j < n_head)
    def _():
        proj_ref[...] = _dot(xn_ref[...], whead_ref[...]).astype(BF16).reshape(nb, tm, tn)

    @pl.when(j >= n_head)
    def _():
        proj_ref[...] = _dot(xn_ref[...], wtail_ref[...]).astype(BF16).reshape(nb, tm, tn)


def _inproj(x, g, w_all, w_tail, w_alr, w_g2, b_gate, *, nb, tm, tn):
    bsz, t, _ = x.shape
    n_head = ALR_OFF // tn
    grid = (bsz // nb, t // tm, D_MAIN // tn)
    return pl.pallas_call(
        functools.partial(_inproj_kernel, nb=nb, tm=tm, n_head=n_head),
        grid=grid,
        in_specs=[
            pl.BlockSpec((nb, tm, D_MODEL), lambda b, i, j: (b, i, 0)),
            pl.BlockSpec((1, D_MODEL), lambda b, i, j: (0, 0)),
            pl.BlockSpec((D_MODEL, tn), lambda b, i, j: (0, jnp.minimum(j, n_head - 1))),
            pl.BlockSpec((D_MODEL, tn), lambda b, i, j: (0, jnp.maximum(j - n_head, 0))),
            pl.BlockSpec((D_MODEL, LANE), lambda b, i, j: (0, 0)),
            pl.BlockSpec((LANE, GLA_DK), lambda b, i, j: (0, 0)),
            pl.BlockSpec((1, GLA_DK), lambda b, i, j: (0, 0)),
        ],
        out_specs=[
            pl.BlockSpec((nb, tm, tn), lambda b, i, j: (b, i, j)),
            pl.BlockSpec((nb, tm, GLA_DK), lambda b, i, j: (b, i, 0)),
        ],
        out_shape=[
            jax.ShapeDtypeStruct((bsz, t, D_MAIN), BF16),
            jax.ShapeDtypeStruct((bsz, t, GLA_DK), F32),
        ],
        scratch_shapes=[pltpu.VMEM((nb * tm, D_MODEL), BF16)],
        compiler_params=pltpu.CompilerParams(
            dimension_semantics=("parallel", "parallel", "arbitrary"),
            vmem_limit_bytes=VMEM_LIMIT),
        name="inproj",
    )(x, g, w_all, w_tail, w_alr, w_g2, b_gate)


def _gla_head(q, k, v, la, s_old, *, diag):
    rows = q.shape[0]
    n_d = rows // diag
    r_i = lax.broadcasted_iota(jnp.int32, (rows, rows), 0)
    c_i = lax.broadcasted_iota(jnp.int32, (rows, rows), 1)
    tril = (r_i >= c_i).astype(BF16)
    la_hi, la_lo = _split_bf16(la * LOG2_E)
    b = _dot(tril, la_hi) + _dot(tril, la_lo)
    b_end = b[rows - 1:rows, :]

    causal_d = (lax.broadcasted_iota(jnp.int32, (diag, diag), 0)
                >= lax.broadcasted_iota(jnp.int32, (diag, diag), 1))
    a_c, k_c, cen = [], [], []
    for d in range(n_d):
        lo = d * diag
        mid = lo + diag // 2
        c = b[mid - 1:mid, :]
        bd = b[lo:lo + diag, :]
        cen.append(c)
        a_c.append(q[lo:lo + diag, :] * jnp.exp2(bd - (c - Q_LOG2_SCALE)))
        k_c.append(k[lo:lo + diag, :] * jnp.exp2(c - bd))

    def block(lo_d, hi_d):
        if hi_d - lo_d == 1:
            s = lax.dot_general(a_c[lo_d].astype(BF16), k_c[lo_d].astype(BF16), NT_DIMS,
                                preferred_element_type=F32)
            return jnp.where(causal_d, s, 0.0)
        mid_d = (lo_d + hi_d) // 2
        ref = b[mid_d * diag - 1:mid_d * diag, :]
        lhs = [a_c[d] * jnp.exp2(cen[d] - ref) for d in range(mid_d, hi_d)]
        rhs = [k_c[d] * jnp.exp2(ref - cen[d]) for d in range(lo_d, mid_d)]
        lhs = (lhs[0] if len(lhs) == 1 else jnp.concatenate(lhs, axis=0)).astype(BF16)
        rhs = (rhs[0] if len(rhs) == 1 else jnp.concatenate(rhs, axis=0)).astype(BF16)
        off = lax.dot_general(lhs, rhs, NT_DIMS, preferred_element_type=F32)
        top = jnp.concatenate([block(lo_d, mid_d), jnp.zeros_like(off)], axis=1)
        bot = jnp.concatenate([off, block(mid_d, hi_d)], axis=1)
        return jnp.concatenate([top, bot], axis=0)

    p = block(0, n_d).astype(BF16)
    qt = [a_c[d] * jnp.exp2(cen[d]) for d in range(n_d)]
    kt = [k_c[d] * jnp.exp2(b_end - cen[d]) for d in range(n_d)]
    qt = (qt[0] if n_d == 1 else jnp.concatenate(qt, axis=0)).astype(BF16)
    kt = (kt[0] if n_d == 1 else jnp.concatenate(kt, axis=0)).astype(BF16)
    o = _dot(qt, s_old.astype(BF16)) + _dot(p, v)
    e_rep = jnp.broadcast_to(b_end * (1.0 / BF16_ROWS), (BF16_ROWS, b_end.shape[1]))
    e_hi, e_lo = _split_bf16(e_rep)
    ones_col = jnp.ones((BF16_ROWS, LANE), BF16)
    dcol = (lax.dot_general(e_hi, ones_col, TN_DIMS, preferred_element_type=F32)
            + lax.dot_general(e_lo, ones_col, TN_DIMS, preferred_element_type=F32))
    decay = jnp.tile(jnp.exp2(dcol), (1, v.shape[1] // LANE))
    s_new = s_old * decay + lax.dot_general(kt, v, TN_DIMS, preferred_element_type=F32)
    return o, s_new


def _gla_kernel(*refs, diag, has_state):
    if has_state:
        q_ref, k_ref, v_ref, go_ref, la_ref, gn_ref, s0_ref, o_ref, sout_ref, s_ref = refs
    else:
        q_ref, k_ref, v_ref, go_ref, la_ref, gn_ref, o_ref, sout_ref, s_ref = refs
    i = pl.program_id(1)

    @pl.when(i == 0)
    def _():
        if has_state:
            s_ref[...] = s0_ref[0]
        else:
            s_ref[...] = jnp.zeros_like(s_ref)

    for h in range(GLA_HEADS):
        ck = pl.ds(h * GLA_HK, GLA_HK)
        cv = pl.ds(h * GLA_HV, GLA_HV)
        q = q_ref[0, :, ck].astype(F32)
        k = k_ref[0, :, ck].astype(F32)
        o, s_new = _gla_head(q, k, v_ref[0, :, cv], la_ref[0, :, ck], s_ref[h], diag=diag)
        s_ref[h] = s_new
        g = go_ref[0, :, cv].astype(F32)
        o_ref[0, :, cv] = (o * _rms_scale(o) * gn_ref[:, cv] * (g * _sigmoid(g))).astype(BF16)

    @pl.when(i == pl.num_programs(1) - 1)
    def _():
        sout_ref[0] = s_ref[...]


def _gla(proj, loga, gla_norm, state, *, tq, diag):
    bsz, t, _ = proj.shape
    has_state = state is not None
    grid = (bsz, t // tq)
    state_spec = pl.BlockSpec((1, GLA_HEADS, GLA_HK, GLA_HV), lambda b, i: (b, 0, 0, 0))
    in_specs = [
        pl.BlockSpec((1, tq, GLA_DK), lambda b, i: (b, i, OFF_Q // GLA_DK)),
        pl.BlockSpec((1, tq, GLA_DK), lambda b, i: (b, i, OFF_K // GLA_DK)),
        pl.BlockSpec((1, tq, GLA_DV), lambda b, i: (b, i, OFF_V // GLA_DV)),
        pl.BlockSpec((1, tq, GLA_DV), lambda b, i: (b, i, OFF_GO // GLA_DV)),
        pl.BlockSpec((1, tq, GLA_DK), lambda b, i: (b, i, 0)),
        pl.BlockSpec((1, GLA_DV), lambda b, i: (0, 0)),
    ]
    args = [proj, proj, proj, proj, loga, gla_norm]
    if has_state:
        in_specs.append(state_spec)
        args.append(state)
    return pl.pallas_call(
        functools.partial(_gla_kernel, diag=diag, has_state=has_state),
        grid=grid,
        in_specs=in_specs,
        out_specs=[
            pl.BlockSpec((1, tq, GLA_DV), lambda b, i: (b, i, 0)),
            state_spec,
        ],
        out_shape=[
            jax.ShapeDtypeStruct((bsz, t, GLA_DV), BF16),
            jax.ShapeDtypeStruct((bsz, GLA_HEADS, GLA_HK, GLA_HV), F32),
        ],
        scratch_shapes=[pltpu.VMEM((GLA_HEADS, GLA_HK, GLA_HV), F32)],
        compiler_params=pltpu.CompilerParams(
            dimension_semantics=("parallel", "arbitrary"),
            vmem_limit_bytes=VMEM_LIMIT),
        name="gla",
    )(*args)


def _mix_kernel(*refs, nb, tm, has_state):
    if has_state:
        (ba_ref, ca_ref, va_ref, ma_ref, mb_ref, og_ref, x_ref, cw_ref, st_ref,
         wa_ref, wb_ref, wo_ref, x1_ref, newc_ref, carry_ref) = refs
    else:
        (ba_ref, ca_ref, va_ref, ma_ref, mb_ref, og_ref, x_ref, cw_ref,
         wa_ref, wb_ref, wo_ref, x1_ref, newc_ref, carry_ref) = refs
    i = pl.program_id(1)
    rows = nb * tm
    first = i == 0
    cw = cw_ref[...]

    @pl.when(first)
    def _():
        carry_ref[...] = jnp.zeros_like(carry_ref)

    conv_parts = []
    for s in range(nb):
        cv = ca_ref[s].astype(F32) * va_ref[s].astype(F32)
        init = st_ref[s] if has_state else jnp.zeros((CONV_W - 1, D_MODEL), F32)
        halo = jnp.where(first, init, carry_ref[s, 0:2, :])
        conv_parts.append(_conv3(cv, halo[0:1], halo[1:2], cw))
        tail = cv[tm - 2:tm, :]
        carry_ref[s, 0:2, :] = tail
        newc_ref[s] = tail
    conv = conv_parts[0] if nb == 1 else jnp.concatenate(conv_parts, axis=0)

    b_a = ba_ref[...].reshape(rows, D_MODEL).astype(F32)
    y_a = _dot((b_a * conv).astype(BF16), wa_ref[...])
    y_b = _dot(og_ref[...].reshape(rows, D_MODEL), wb_ref[...])
    m_a = ma_ref[...].reshape(rows, D_MODEL).astype(F32)
    m_b = mb_ref[...].reshape(rows, D_MODEL).astype(F32)
    z = _sigmoid(m_a) * y_a + _sigmoid(m_b) * y_b
    x1 = x_ref[...].reshape(rows, D_MODEL) + _dot(z.astype(BF16), wo_ref[...])
    x1_ref[...] = x1.reshape(nb, tm, D_MODEL)


def _mix(proj, og, x, conv_w, state, w_a, w_b, w_o, *, nb, tm):
    bsz, t, _ = x.shape
    has_state = state is not None
    grid = (bsz // nb, t // tm)

    def col(off):
        return pl.BlockSpec((nb, tm, D_MODEL), lambda b, i: (b, i, off // D_MODEL))

    row_spec = pl.BlockSpec((nb, tm, D_MODEL), lambda b, i: (b, i, 0))
    w_spec = pl.BlockSpec((D_MODEL, D_MODEL), lambda b, i: (0, 0), pipeline_mode=pl.Buffered(1))
    in_specs = [col(OFF_BA), col(OFF_CA), col(OFF_VA), col(OFF_MA), col(OFF_MB), row_spec, row_spec,
                pl.BlockSpec((CONV_W, D_MODEL), lambda b, i: (0, 0))]
    args = [proj, proj, proj, proj, proj, og, x, conv_w]
    if has_state:
        in_specs.append(pl.BlockSpec((nb, CONV_W - 1, D_MODEL), lambda b, i: (b, 0, 0)))
        args.append(state)
    in_specs += [w_spec, w_spec, w_spec]
    args += [w_a, w_b, w_o]
    return pl.pallas_call(
        functools.partial(_mix_kernel, nb=nb, tm=tm, has_state=has_state),
        grid=grid,
        in_specs=in_specs,
        out_specs=[
            row_spec,
            pl.BlockSpec((nb, CONV_W - 1, D_MODEL), lambda b, i: (b, 0, 0)),
        ],
        out_shape=[
            jax.ShapeDtypeStruct((bsz, t, D_MODEL), F32),
            jax.ShapeDtypeStruct((bsz, CONV_W - 1, D_MODEL), F32),
        ],
        scratch_shapes=[pltpu.VMEM((nb, 8, D_MODEL), F32)],
        compiler_params=pltpu.CompilerParams(
            dimension_semantics=("parallel", "arbitrary"),
            vmem_limit_bytes=VMEM_LIMIT),
        name="mix",
    )(*args)


def _ffn_up_kernel(*refs, nb, tm, tn, has_state):
    if has_state:
        (x1_ref, g_ref, wv_ref, wg_ref, cwv_ref, cwg_ref, bv_ref, bg_ref, stv_ref, stg_ref,
         h_ref, nfv_ref, nfg_ref, hn_ref, cv_ref, cg_ref) = refs
    else:
        (x1_ref, g_ref, wv_ref, wg_ref, cwv_ref, cwg_ref, bv_ref, bg_ref,
         h_ref, nfv_ref, nfg_ref, hn_ref, cv_ref, cg_ref) = refs
    i = pl.program_id(1)
    j = pl.program_id(2)
    rows = nb * tm
    first = i == 0

    @pl.when(j == 0)
    def _():
        x1 = x1_ref[...].reshape(rows, D_MODEL)
        hn_ref[...] = (x1 * _rms_scale(x1) * g_ref[...]).astype(BF16)

    @pl.when(first)
    def _():
        cv_ref[j] = jnp.zeros(cv_ref.shape[1:], F32)
        cg_ref[j] = jnp.zeros(cg_ref.shape[1:], F32)

    n_parts = ROW_PARTS if nb == 1 and tm % (ROW_PARTS * SUBLANES) == 0 else 1
    part = rows // n_parts
    ug = [_dot(hn_ref[r * part:(r + 1) * part, :], wg_ref[...]) for r in range(n_parts)]
    uv = [_dot(hn_ref[r * part:(r + 1) * part, :], wv_ref[...]) for r in range(n_parts)]
    cwv = cwv_ref[...]
    cwg = cwg_ref[...]
    zero = jnp.zeros((CONV_W - 1, tn), F32)
    segs = [(0, r) for r in range(n_parts)] if n_parts > 1 else [(s, None) for s in range(nb)]
    act, halo_g, halo_v = {}, None, None
    for s, r in segs:
        ugs = ug[r] if r is not None else ug[0][s * tm:(s + 1) * tm]
        if not r:
            halo_g = jnp.where(first, stg_ref[s] if has_state else zero, cg_ref[j, s, 0:2, :])
        gate = _conv3(ugs, halo_g[0:1], halo_g[1:2], cwg) + bg_ref[...]
        act[(s, r)] = gate * _sigmoid(gate)
        halo_g = ugs[ugs.shape[0] - 2:, :]
        if r is None or r == n_parts - 1:
            cg_ref[j, s, 0:2, :] = halo_g
            nfg_ref[s, 0] = halo_g
    for s, r in segs:
        uvs = uv[r] if r is not None else uv[0][s * tm:(s + 1) * tm]
        if not r:
            halo_v = jnp.where(first, stv_ref[s] if has_state else zero, cv_ref[j, s, 0:2, :])
        val = _conv3(uvs, halo_v[0:1], halo_v[1:2], cwv) + bv_ref[...]
        halo_v = uvs[uvs.shape[0] - 2:, :]
        if r is None or r == n_parts - 1:
            cv_ref[j, s, 0:2, :] = halo_v
            nfv_ref[s, 0] = halo_v
        out = (act[(s, r)] * val).astype(BF16)
        if r is None:
            h_ref[s] = out
        else:
            h_ref[s, r * part:(r + 1) * part, :] = out


def _ffn_up(x1, g, w_up, conv_w, conv_b, state, *, nb, tm, tn):
    bsz, t, _ = x1.shape
    has_state = state is not None
    n_ct = D_FF // tn
    grid = (bsz // nb, t // tm, n_ct)
    in_specs = [
        pl.BlockSpec((nb, tm, D_MODEL), lambda b, i, j: (b, i, 0)),
        pl.BlockSpec((1, D_MODEL), lambda b, i, j: (0, 0)),
        pl.BlockSpec((D_MODEL, tn), lambda b, i, j: (0, j)),
        pl.BlockSpec((D_MODEL, tn), lambda b, i, j: (0, n_ct + j)),
        pl.BlockSpec((CONV_W, tn), lambda b, i, j: (0, j)),
        pl.BlockSpec((CONV_W, tn), lambda b, i, j: (0, n_ct + j)),
        pl.BlockSpec((1, tn), lambda b, i, j: (0, j)),
        pl.BlockSpec((1, tn), lambda b, i, j: (0, n_ct + j)),
    ]
    args = [x1, g, w_up, w_up, conv_w, conv_w, conv_b, conv_b]
    if has_state:
        in_specs += [
            pl.BlockSpec((nb, CONV_W - 1, tn), lambda b, i, j: (b, 0, j)),
            pl.BlockSpec((nb, CONV_W - 1, tn), lambda b, i, j: (b, 0, n_ct + j)),
        ]
        args += [state, state]
    tail_spec = pl.BlockSpec((nb, 1, CONV_W - 1, tn), lambda b, i, j: (b, i, 0, j))
    return pl.pallas_call(
        functools.partial(_ffn_up_kernel, nb=nb, tm=tm, tn=tn, has_state=has_state),
        grid=grid,
        in_specs=in_specs,
        out_specs=[
            pl.BlockSpec((nb, tm, tn), lambda b, i, j: (b, i, j)),
            tail_spec,
            tail_spec,
        ],
        out_shape=[
            jax.ShapeDtypeStruct((bsz, t, D_FF), BF16),
            jax.ShapeDtypeStruct((bsz, t // tm, CONV_W - 1, D_FF), F32),
            jax.ShapeDtypeStruct((bsz, t // tm, CONV_W - 1, D_FF), F32),
        ],
        scratch_shapes=[
            pltpu.VMEM((nb * tm, D_MODEL), BF16),
            pltpu.VMEM((n_ct, nb, 8, tn), F32),
            pltpu.VMEM((n_ct, nb, 8, tn), F32),
        ],
        compiler_params=pltpu.CompilerParams(
            dimension_semantics=("parallel", "arbitrary", "arbitrary"),
            vmem_limit_bytes=VMEM_LIMIT),
        name="ffn_up",
    )(*args)


def _ffn_down_kernel(h_ref, x1_ref, p_ref, wd_ref, gp_ref, wpg_ref, wple_ref, gf_ref, y_ref):
    x2 = x1_ref[...] + _dot(h_ref[...], wd_ref[...])
    pn = (x2 * _rms_scale(x2) * gp_ref[...]).astype(BF16)
    gate = _sigmoid(_dot(pn, wpg_ref[...]))
    pe = _dot(p_ref[...].astype(BF16), wple_ref[...])
    x3 = x2 + gate * pe
    y_ref[...] = x3 * _rms_scale(x3) * gf_ref[...]


def _ffn_down(h, x1, p, w_down, g_ple, w_pg, w_ple, g_final, *, tm):
    m = x1.shape[0]
    const = dict(pipeline_mode=pl.Buffered(1))
    return pl.pallas_call(
        _ffn_down_kernel,
        grid=(m // tm,),
        in_specs=[
            pl.BlockSpec((tm, D_FF), lambda i: (i, 0)),
            pl.BlockSpec((tm, D_MODEL), lambda i: (i, 0)),
            pl.BlockSpec((tm, D_PLE), lambda i: (i, 0)),
            pl.BlockSpec((D_FF, D_MODEL), lambda i: (0, 0), **const),
            pl.BlockSpec((1, D_MODEL), lambda i: (0, 0)),
            pl.BlockSpec((D_MODEL, D_MODEL), lambda i: (0, 0), **const),
            pl.BlockSpec((D_PLE, D_MODEL), lambda i: (0, 0), **const),
            pl.BlockSpec((1, D_MODEL), lambda i: (0, 0)),
        ],
        out_specs=pl.BlockSpec((tm, D_MODEL), lambda i: (i, 0)),
        out_shape=jax.ShapeDtypeStruct((m, D_MODEL), F32),
        compiler_params=pltpu.CompilerParams(
            dimension_semantics=("parallel",),
            vmem_limit_bytes=VMEM_LIMIT),
        name="ffn_down",
    )(h, x1, p, w_down, g_ple, w_pg, w_ple, g_final)


def _tiles(bsz, t):
    if t >= 1024:
        return dict(nb=1, tm_in=1024, tm_mix=256, tm_up=1024, tq=256, diag=128)
    return dict(nb=bsz, tm_in=t, tm_mix=t, tm_up=t, tq=t, diag=t)


def _layer(x, p, conv_state, gla_state, ffn_state, w, g_final):
    bsz, t, _ = x.shape
    cfg = _tiles(bsz, t)
    nb = cfg["nb"]
    proj, loga = _inproj(x, w["norm_mix"], w["w_all"], w["w_tail"], w["w_alr"], w["w_g2"], w["b_gate"],
                         nb=nb, tm=cfg["tm_in"], tn=1024)
    og, new_s = _gla(proj, loga, w["gla_norm"], gla_state, tq=cfg["tq"], diag=cfg["diag"])
    x1, new_conv = _mix(proj, og, x, w["conv_a_w"], conv_state, w["w_a_out"], w["w_b_out"], w["w_o"],
                        nb=nb, tm=cfg["tm_mix"])
    h, nf_v, nf_g = _ffn_up(x1, w["norm_ffn"], w["w_up"], w["ffn_conv_w"], w["ffn_conv_b"], ffn_state,
                            nb=nb, tm=cfg["tm_up"], tn=512)
    m = bsz * t
    y = _ffn_down(h.reshape(m, D_FF), x1.reshape(m, D_MODEL), p.reshape(m, D_PLE),
                  w["w_down"], w["norm_ple"], w["w_ple_gate"], w["w_ple"], g_final, tm=256)
    new_ffn = jnp.concatenate([nf_v[:, -1], nf_g[:, -1]], axis=-1)
    return y.reshape(bsz, t, D_MODEL), new_conv, new_s, new_ffn


def kernel(x_prompt, x_sample, p_prompt, p_sample, state_conv_a, state_gla, state_ffn_conv, norm_mix, w_in, conv_a_w, w_a_out, w_gate2, b_gate, gla_norm, w_b_out, w_o, norm_ffn, w_up, ffn_conv_w, ffn_conv_b, w_down, norm_ple, w_ple_gate, w_ple, norm_final):
    depth = w_in.shape[0]
    assert depth == 1, "the final norm is fused into the last layer; one layer supported"
    i = 0
    w_in_i = w_in[i]
    w_alr = jnp.pad(w_in_i[:, ALR_OFF:ALR_OFF + GATE_RANK], ((0, 0), (0, LANE - GATE_RANK)))
    w = dict(
        norm_mix=norm_mix[i][None],
        w_all=w_in_i.astype(BF16),
        w_tail=w_in_i[:, ALR_OFF + GATE_RANK:].astype(BF16),
        w_alr=w_alr.astype(BF16),
        w_g2=jnp.pad(w_gate2[i], ((0, LANE - GATE_RANK), (0, 0))).astype(BF16),
        b_gate=b_gate[i][None],
        gla_norm=gla_norm[i][None],
        conv_a_w=conv_a_w[i],
        w_a_out=w_a_out[i].astype(BF16),
        w_b_out=w_b_out[i].astype(BF16),
        w_o=w_o[i].astype(BF16),
        norm_ffn=norm_ffn[i][None],
        w_up=w_up[i].astype(BF16),
        ffn_conv_w=ffn_conv_w[i],
        ffn_conv_b=ffn_conv_b[i][None],
        w_down=w_down[i].astype(BF16),
        norm_ple=norm_ple[i][None],
        w_ple_gate=w_ple_gate[i].astype(BF16),
        w_ple=w_ple[i].astype(BF16),
    )
    g_final = norm_final[None]
    yp, c1, s1, f1 = _layer(x_prompt, p_prompt[i], None, None, None, w, g_final)
    ys, c2, s2, f2 = _layer(x_sample, p_sample[i], state_conv_a[i], state_gla[i], state_ffn_conv[i],
                            w, g_final)
    return (yp, ys, c1[None], s1[None], f1[None], c2[None], s2[None], f2[None])
```

```python
import functools
import math

import jax
import jax.numpy as jnp
from jax import lax
from jax.experimental import pallas as pl
from jax.experimental.pallas import tpu as pltpu

F32 = jnp.float32
BF16 = jnp.bfloat16

D_MODEL = 2048
CONV_W = 3
GLA_HEADS = 4
GLA_DK = D_MODEL // 2
GLA_DV = D_MODEL
GLA_HK = GLA_DK // GLA_HEADS
GLA_HV = GLA_DV // GLA_HEADS
GATE_RANK = 16
GATE_NORMALIZER = 16.0
D_FF = 5632
D_PLE = 256
EPS = 1e-6

D_MAIN = 3 * D_MODEL + 2 * GLA_DK + 2 * GLA_DV + 2 * D_MODEL
OFF_BA, OFF_CA, OFF_VA = 0, D_MODEL, 2 * D_MODEL
OFF_Q = 3 * D_MODEL
OFF_K = OFF_Q + GLA_DK
OFF_V = OFF_K + GLA_DK
OFF_GO = OFF_V + GLA_DV
OFF_MA = OFF_GO + GLA_DV
OFF_MB = OFF_MA + D_MODEL
ALR_OFF = OFF_MA
LANE = 128
SUBLANES = 8
ROW_PARTS = 4
BF16_ROWS = 16
VMEM_LIMIT = 56 * 1024 * 1024

LOG2_E = math.log2(math.e)
Q_LOG2_SCALE = -0.5 * math.log2(GLA_HK)

NT_DIMS = (((1,), (1,)), ((), ()))
TN_DIMS = (((0,), (0,)), ((), ()))


def _rms_scale(x):
    return lax.rsqrt(jnp.mean(x * x, axis=-1, keepdims=True) + EPS)


def _sigmoid(x):
    return 1.0 / (1.0 + jnp.exp(-x))


def _log_sigmoid(z):
    return jnp.minimum(z, 0.0) - jnp.log(1.0 + jnp.exp(-jnp.abs(z)))


def _dot(a, b):
    return jnp.dot(a, b, preferred_element_type=F32)


def _split_bf16(x):
    hi = x.astype(BF16)
    return hi, (x - hi.astype(F32)).astype(BF16)


def _conv3(u, h0, h1, w):
    r1 = pltpu.roll(u, 1, axis=0)
    r2 = pltpu.roll(u, 2, axis=0)
    rows = lax.broadcasted_iota(jnp.int32, (SUBLANES, 1), 0)
    top1 = jnp.where(rows == 0, h1, r1[:SUBLANES])
    top2 = jnp.where(rows == 0, h0, jnp.where(rows == 1, h1, r2[:SUBLANES]))
    sh1 = jnp.concatenate([top1, r1[SUBLANES:]], axis=0)
    sh2 = jnp.concatenate([top2, r2[SUBLANES:]], axis=0)
    return w[0:1] * sh2 + w[1:2] * sh1 + w[2:3] * u


def _inproj_kernel(x_ref, g_ref, whead_ref, wtail_ref, walr_ref, wg2_ref, bg_ref, proj_ref, loga_ref,
                   xn_ref, *, nb, tm, n_head):
    j = pl.program_id(2)
    rows = nb * tm

    @pl.when(j == 0)
    def _():
        x = x_ref[...].reshape(rows, D_MODEL)
        xn = (x * _rms_scale(x) * g_ref[...]).astype(BF16)
        xn_ref[...] = xn
        a_lr = _dot(xn, walr_ref[...])
        z = _dot(a_lr.astype(BF16), wg2_ref[...]) + bg_ref[...]
        loga_ref[...] = (_log_sigmoid(z) * (1.0 / GATE_NORMALIZER)).reshape(nb, tm, GLA_DK)

    tn = proj_ref.shape[-1]

    @pl.when(j < n_head)
    def _():
        proj_ref[...] = _dot(xn_ref[...], whead_ref[...]).astype(BF16).reshape(nb, tm, tn)

    @pl.when(j >= n_head)
    def _():
        proj_ref[...] = _dot(xn_ref[...], wtail_ref[...]).astype(BF16).reshape(nb, tm, tn)


def _inproj(x, g, w_all, w_tail, w_alr, w_g2, b_gate, *, nb, tm, tn):
    bsz, t, _ = x.shape
    n_head = ALR_OFF // tn
    grid = (bsz // nb, t // tm, D_MAIN // tn)
    return pl.pallas_call(
        functools.partial(_inproj_kernel, nb=nb, tm=tm, n_head=n_head),
        grid=grid,
        in_specs=[
            pl.BlockSpec((nb, tm, D_MODEL), lambda b, i, j: (b, i, 0)),
            pl.BlockSpec((1, D_MODEL), lambda b, i, j: (0, 0)),
            pl.BlockSpec((D_MODEL, tn), lambda b, i, j: (0, jnp.minimum(j, n_head - 1))),
            pl.BlockSpec((D_MODEL, tn), lambda b, i, j: (0, jnp.maximum(j - n_head, 0))),
            pl.BlockSpec((D_MODEL, LANE), lambda b, i, j: (0, 0)),
            pl.BlockSpec((LANE, GLA_DK), lambda b, i, j: (0, 0)),
            pl.BlockSpec((1, GLA_DK), lambda b, i, j: (0, 0)),
        ],
        out_specs=[
            pl.BlockSpec((nb, tm, tn), lambda b, i, j: (b, i, j)),
            pl.BlockSpec((nb, tm, GLA_DK), lambda b, i, j: (b, i, 0)),
        ],
        out_shape=[
            jax.ShapeDtypeStruct((bsz, t, D_MAIN), BF16),
            jax.ShapeDtypeStruct((bsz, t, GLA_DK), F32),
        ],
        scratch_shapes=[pltpu.VMEM((nb * tm, D_MODEL), BF16)],
        compiler_params=pltpu.CompilerParams(
            dimension_semantics=("parallel", "parallel", "arbitrary"),
            vmem_limit_bytes=VMEM_LIMIT),
        name="inproj",
    )(x, g, w_all, w_tail, w_alr, w_g2, b_gate)


def _gla_head(q, k, v, la, s_old, *, diag):
    rows = q.shape[0]
    n_d = rows // diag
    r_i = lax.broadcasted_iota(jnp.int32, (rows, rows), 0)
    c_i = lax.broadcasted_iota(jnp.int32, (rows, rows), 1)
    tril = (r_i >= c_i).astype(BF16)
    la_hi, la_lo = _split_bf16(la * LOG2_E)
    b = _dot(tril, la_hi) + _dot(tril, la_lo)
    b_end = b[rows - 1:rows, :]

    causal_d = (lax.broadcasted_iota(jnp.int32, (diag, diag), 0)
                >= lax.broadcasted_iota(jnp.int32, (diag, diag), 1))
    a_c, k_c, cen = [], [], []
    for d in range(n_d):
        lo = d * diag
        mid = lo + diag // 2
        c = b[mid - 1:mid, :]
        bd = b[lo:lo + diag, :]
        cen.append(c)
        a_c.append(q[lo:lo + diag, :] * jnp.exp2(bd - (c - Q_LOG2_SCALE)))
        k_c.append(k[lo:lo + diag, :] * jnp.exp2(c - bd))

    def block(lo_d, hi_d):
        if hi_d - lo_d == 1:
            s = lax.dot_general(a_c[lo_d].astype(BF16), k_c[lo_d].astype(BF16), NT_DIMS,
                                preferred_element_type=F32)
            return jnp.where(causal_d, s, 0.0)
        mid_d = (lo_d + hi_d) // 2
        ref = b[mid_d * diag - 1:mid_d * diag, :]
        lhs = [a_c[d] * jnp.exp2(cen[d] - ref) for d in range(mid_d, hi_d)]
        rhs = [k_c[d] * jnp.exp2(ref - cen[d]) for d in range(lo_d, mid_d)]
        lhs = (lhs[0] if len(lhs) == 1 else jnp.concatenate(lhs, axis=0)).astype(BF16)
        rhs = (rhs[0] if len(rhs) == 1 else jnp.concatenate(rhs, axis=0)).astype(BF16)
        off = lax.dot_general(lhs, rhs, NT_DIMS, preferred_element_type=F32)
        top = jnp.concatenate([block(lo_d, mid_d), jnp.zeros_like(off)], axis=1)
        bot = jnp.concatenate([off, block(mid_d, hi_d)], axis=1)
        return jnp.concatenate([top, bot], axis=0)

    p = block(0, n_d).astype(BF16)
    qt = [a_c[d] * jnp.exp2(cen[d]) for d in range(n_d)]
    kt = [k_c[d] * jnp.exp2(b_end - cen[d]) for d in range(n_d)]
    qt = (qt[0] if n_d == 1 else jnp.concatenate(qt, axis=0)).astype(BF16)
    kt = (kt[0] if n_d == 1 else jnp.concatenate(kt, axis=0)).astype(BF16)
    o = _dot(qt, s_old.astype(BF16)) + _dot(p, v)
    e_rep = jnp.broadcast_to(b_end * (1.0 / BF16_ROWS), (BF16_ROWS, b_end.shape[1]))
    e_hi, e_lo = _split_bf16(e_rep)
    ones_col = jnp.ones((BF16_ROWS, LANE), BF16)
    dcol = (lax.dot_general(e_hi, ones_col, TN_DIMS, preferred_element_type=F32)
            + lax.dot_general(e_lo, ones_col, TN_DIMS, preferred_element_type=F32))
    decay = jnp.tile(jnp.exp2(dcol), (1, v.shape[1] // LANE))
    s_new = s_old * decay + lax.dot_general(kt, v, TN_DIMS, preferred_element_type=F32)
    return o, s_new


def _gla_kernel(*refs, diag, has_state):
    if has_state:
        q_ref, k_ref, v_ref, go_ref, la_ref, gn_ref, s0_ref, o_ref, sout_ref, s_ref = refs
    else:
        q_ref, k_ref, v_ref, go_ref, la_ref, gn_ref, o_ref, sout_ref, s_ref = refs
    i = pl.program_id(1)

    @pl.when(i == 0)
    def _():
        if has_state:
            s_ref[...] = s0_ref[0]
        else:
            s_ref[...] = jnp.zeros_like(s_ref)

    for h in range(GLA_HEADS):
        ck = pl.ds(h * GLA_HK, GLA_HK)
        cv = pl.ds(h * GLA_HV, GLA_HV)
        q = q_ref[0, :, ck].astype(F32)
        k = k_ref[0, :, ck].astype(F32)
        o, s_new = _gla_head(q, k, v_ref[0, :, cv], la_ref[0, :, ck], s_ref[h], diag=diag)
        s_ref[h] = s_new
        g = go_ref[0, :, cv].astype(F32)
        o_ref[0, :, cv] = (o * _rms_scale(o) * gn_ref[:, cv] * (g * _sigmoid(g))).astype(BF16)

    @pl.when(i == pl.num_programs(1) - 1)
    def _():
        sout_ref[0] = s_ref[...]


def _gla(proj, loga, gla_norm, state, *, tq, diag):
    bsz, t, _ = proj.shape
    has_state = state is not None
    grid = (bsz, t // tq)
    state_spec = pl.BlockSpec((1, GLA_HEADS, GLA_HK, GLA_HV), lambda b, i: (b, 0, 0, 0))
    in_specs = [
        pl.BlockSpec((1, tq, GLA_DK), lambda b, i: (b, i, OFF_Q // GLA_DK)),
        pl.BlockSpec((1, tq, GLA_DK), lambda b, i: (b, i, OFF_K // GLA_DK)),
        pl.BlockSpec((1, tq, GLA_DV), lambda b, i: (b, i, OFF_V // GLA_DV)),
        pl.BlockSpec((1, tq, GLA_DV), lambda b, i: (b, i, OFF_GO // GLA_DV)),
        pl.BlockSpec((1, tq, GLA_DK), lambda b, i: (b, i, 0)),
        pl.BlockSpec((1, GLA_DV), lambda b, i: (0, 0)),
    ]
    args = [proj, proj, proj, proj, loga, gla_norm]
    if has_state:
        in_specs.append(state_spec)
        args.append(state)
    return pl.pallas_call(
        functools.partial(_gla_kernel, diag=diag, has_state=has_state),
        grid=grid,
        in_specs=in_specs,
        out_specs=[
            pl.BlockSpec((1, tq, GLA_DV), lambda b, i: (b, i, 0)),
            state_spec,
        ],
        out_shape=[
            jax.ShapeDtypeStruct((bsz, t, GLA_DV), BF16),
            jax.ShapeDtypeStruct((bsz, GLA_HEADS, GLA_HK, GLA_HV), F32),
        ],
        scratch_shapes=[pltpu.VMEM((GLA_HEADS, GLA_HK, GLA_HV), F32)],
        compiler_params=pltpu.CompilerParams(
            dimension_semantics=("parallel", "arbitrary"),
            vmem_limit_bytes=VMEM_LIMIT),
        name="gla",
    )(*args)


def _mix_kernel(*refs, nb, tm, has_state):
    if has_state:
        (ba_ref, ca_ref, va_ref, ma_ref, mb_ref, og_ref, x_ref, cw_ref, st_ref,
         wa_ref, wb_ref, wo_ref, x1_ref, newc_ref, carry_ref) = refs
    else:
        (ba_ref, ca_ref, va_ref, ma_ref, mb_ref, og_ref, x_ref, cw_ref,
         wa_ref, wb_ref, wo_ref, x1_ref, newc_ref, carry_ref) = refs
    i = pl.program_id(1)
    rows = nb * tm
    first = i == 0
    cw = cw_ref[...]

    @pl.when(first)
    def _():
        carry_ref[...] = jnp.zeros_like(carry_ref)

    conv_parts = []
    for s in range(nb):
        cv = ca_ref[s].astype(F32) * va_ref[s].astype(F32)
        init = st_ref[s] if has_state else jnp.zeros((CONV_W - 1, D_MODEL), F32)
        halo = jnp.where(first, init, carry_ref[s, 0:2, :])
        conv_parts.append(_conv3(cv, halo[0:1], halo[1:2], cw))
        tail = cv[tm - 2:tm, :]
        carry_ref[s, 0:2, :] = tail
        newc_ref[s] = tail
    conv = conv_parts[0] if nb == 1 else jnp.concatenate(conv_parts, axis=0)

    b_a = ba_ref[...].reshape(rows, D_MODEL).astype(F32)
    y_a = _dot((b_a * conv).astype(BF16), wa_ref[...])
    y_b = _dot(og_ref[...].reshape(rows, D_MODEL), wb_ref[...])
    m_a = ma_ref[...].reshape(rows, D_MODEL).astype(F32)
    m_b = mb_ref[...].reshape(rows, D_MODEL).astype(F32)
    z = _sigmoid(m_a) * y_a + _sigmoid(m_b) * y_b
    x1 = x_ref[...].reshape(rows, D_MODEL) + _dot(z.astype(BF16), wo_ref[...])
    x1_ref[...] = x1.reshape(nb, tm, D_MODEL)


def _mix(proj, og, x, conv_w, state, w_a, w_b, w_o, *, nb, tm):
    bsz, t, _ = x.shape
    has_state = state is not None
    grid = (bsz // nb, t // tm)

    def col(off):
        return pl.BlockSpec((nb, tm, D_MODEL), lambda b, i: (b, i, off // D_MODEL))

    row_spec = pl.BlockSpec((nb, tm, D_MODEL), lambda b, i: (b, i, 0))
    w_spec = pl.BlockSpec((D_MODEL, D_MODEL), lambda b, i: (0, 0), pipeline_mode=pl.Buffered(1))
    in_specs = [col(OFF_BA), col(OFF_CA), col(OFF_VA), col(OFF_MA), col(OFF_MB), row_spec, row_spec,
                pl.BlockSpec((CONV_W, D_MODEL), lambda b, i: (0, 0))]
    args = [proj, proj, proj, proj, proj, og, x, conv_w]
    if has_state:
        in_specs.append(pl.BlockSpec((nb, CONV_W - 1, D_MODEL), lambda b, i: (b, 0, 0)))
        args.append(state)
    in_specs += [w_spec, w_spec, w_spec]
    args += [w_a, w_b, w_o]
    return pl.pallas_call(
        functools.partial(_mix_kernel, nb=nb, tm=tm, has_state=has_state),
        grid=grid,
        in_specs=in_specs,
        out_specs=[
            row_spec,
            pl.BlockSpec((nb, CONV_W - 1, D_MODEL), lambda b, i: (b, 0, 0)),
        ],
        out_shape=[
            jax.ShapeDtypeStruct((bsz, t, D_MODEL), F32),
            jax.ShapeDtypeStruct((bsz, CONV_W - 1, D_MODEL), F32),
        ],
        scratch_shapes=[pltpu.VMEM((nb, 8, D_MODEL), F32)],
        compiler_params=pltpu.CompilerParams(
            dimension_semantics=("parallel", "arbitrary"),
            vmem_limit_bytes=VMEM_LIMIT),
        name="mix",
    )(*args)


def _ffn_up_kernel(*refs, nb, tm, tn, has_state):
    if has_state:
        (x1_ref, g_ref, wv_ref, wg_ref, cw_ref, cb_ref, stv_ref, stg_ref,
         h_ref, nfv_ref, nfg_ref, hn_ref, cv_ref, cg_ref) = refs
    else:
        (x1_ref, g_ref, wv_ref, wg_ref, cw_ref, cb_ref,
         h_ref, nfv_ref, nfg_ref, hn_ref, cv_ref, cg_ref) = refs
    i = pl.program_id(1)
    j = pl.program_id(2)
    rows = nb * tm
    first = i == 0
    cols_v = pl.ds(pl.multiple_of(j * tn, tn), tn)
    cols_g = pl.ds(pl.multiple_of(D_FF + j * tn, tn), tn)

    @pl.when(j == 0)
    def _():
        x1 = x1_ref[...].reshape(rows, D_MODEL)
        hn_ref[...] = (x1 * _rms_scale(x1) * g_ref[...]).astype(BF16)

    @pl.when(first)
    def _():
        cv_ref[j] = jnp.zeros(cv_ref.shape[1:], F32)
        cg_ref[j] = jnp.zeros(cg_ref.shape[1:], F32)

    n_parts = ROW_PARTS if nb == 1 and tm % (ROW_PARTS * SUBLANES) == 0 else 1
    part = rows // n_parts
    ug = [_dot(hn_ref[r * part:(r + 1) * part, :], wg_ref[...]) for r in range(n_parts)]
    uv = [_dot(hn_ref[r * part:(r + 1) * part, :], wv_ref[...]) for r in range(n_parts)]
    cwv = cw_ref[:, cols_v]
    cwg = cw_ref[:, cols_g]
    bias_v = cb_ref[:, cols_v]
    bias_g = cb_ref[:, cols_g]
    zero = jnp.zeros((CONV_W - 1, tn), F32)
    segs = [(0, r) for r in range(n_parts)] if n_parts > 1 else [(s, None) for s in range(nb)]
    act, halo_g, halo_v = {}, None, None
    for s, r in segs:
        ugs = ug[r] if r is not None else ug[0][s * tm:(s + 1) * tm]
        if not r:
            halo_g = jnp.where(first, stg_ref[s] if has_state else zero, cg_ref[j, s, 0:2, :])
        gate = _conv3(ugs, halo_g[0:1], halo_g[1:2], cwg) + bias_g
        act[(s, r)] = gate * _sigmoid(gate)
        halo_g = ugs[ugs.shape[0] - 2:, :]
        if r is None or r == n_parts - 1:
            cg_ref[j, s, 0:2, :] = halo_g
            nfg_ref[s, 0] = halo_g
    for s, r in segs:
        uvs = uv[r] if r is not None else uv[0][s * tm:(s + 1) * tm]
        if not r:
            halo_v = jnp.where(first, stv_ref[s] if has_state else zero, cv_ref[j, s, 0:2, :])
        val = _conv3(uvs, halo_v[0:1], halo_v[1:2], cwv) + bias_v
        halo_v = uvs[uvs.shape[0] - 2:, :]
        if r is None or r == n_parts - 1:
            cv_ref[j, s, 0:2, :] = halo_v
            nfv_ref[s, 0] = halo_v
        out = (act[(s, r)] * val).astype(BF16)
        if r is None:
            h_ref[s] = out
        else:
            h_ref[s, r * part:(r + 1) * part, :] = out


def _ffn_up(x1, g, w_up, conv_w, conv_b, state, *, nb, tm, tn):
    bsz, t, _ = x1.shape
    has_state = state is not None
    n_ct = D_FF // tn
    grid = (bsz // nb, t // tm, n_ct)
    in_specs = [
        pl.BlockSpec((nb, tm, D_MODEL), lambda b, i, j: (b, i, 0)),
        pl.BlockSpec((1, D_MODEL), lambda b, i, j: (0, 0)),
        pl.BlockSpec((D_MODEL, tn), lambda b, i, j: (0, j)),
        pl.BlockSpec((D_MODEL, tn), lambda b, i, j: (0, n_ct + j)),
        pl.BlockSpec((CONV_W, 2 * D_FF), lambda b, i, j: (0, 0)),
        pl.BlockSpec((1, 2 * D_FF), lambda b, i, j: (0, 0)),
    ]
    args = [x1, g, w_up, w_up, conv_w, conv_b]
    if has_state:
        in_specs += [
            pl.BlockSpec((nb, CONV_W - 1, tn), lambda b, i, j: (b, 0, j)),
            pl.BlockSpec((nb, CONV_W - 1, tn), lambda b, i, j: (b, 0, n_ct + j)),
        ]
        args += [state, state]
    tail_spec = pl.BlockSpec((nb, 1, CONV_W - 1, tn), lambda b, i, j: (b, i, 0, j))
    return pl.pallas_call(
        functools.partial(_ffn_up_kernel, nb=nb, tm=tm, tn=tn, has_state=has_state),
        grid=grid,
        in_specs=in_specs,
        out_specs=[
            pl.BlockSpec((nb, tm, tn), lambda b, i, j: (b, i, j)),
            tail_spec,
            tail_spec,
        ],
        out_shape=[
            jax.ShapeDtypeStruct((bsz, t, D_FF), BF16),
            jax.ShapeDtypeStruct((bsz, t // tm, CONV_W - 1, D_FF), F32),
            jax.ShapeDtypeStruct((bsz, t // tm, CONV_W - 1, D_FF), F32),
        ],
        scratch_shapes=[
            pltpu.VMEM((nb * tm, D_MODEL), BF16),
            pltpu.VMEM((n_ct, nb, 8, tn), F32),
            pltpu.VMEM((n_ct, nb, 8, tn), F32),
        ],
        compiler_params=pltpu.CompilerParams(
            dimension_semantics=("parallel", "arbitrary", "arbitrary"),
            vmem_limit_bytes=VMEM_LIMIT),
        name="ffn_up",
    )(*args)


def _ffn_down_kernel(h_ref, x1_ref, p_ref, wd_ref, gp_ref, wpg_ref, wple_ref, gf_ref, y_ref):
    x2 = x1_ref[...] + _dot(h_ref[...], wd_ref[...])
    pn = (x2 * _rms_scale(x2) * gp_ref[...]).astype(BF16)
    gate = _sigmoid(_dot(pn, wpg_ref[...]))
    pe = _dot(p_ref[...].astype(BF16), wple_ref[...])
    x3 = x2 + gate * pe
    y_ref[...] = x3 * _rms_scale(x3) * gf_ref[...]


def _ffn_down(h, x1, p, w_down, g_ple, w_pg, w_ple, g_final, *, tm):
    m = x1.shape[0]
    const = dict(pipeline_mode=pl.Buffered(1))
    return pl.pallas_call(
        _ffn_down_kernel,
        grid=(m // tm,),
        in_specs=[
            pl.BlockSpec((tm, D_FF), lambda i: (i, 0)),
            pl.BlockSpec((tm, D_MODEL), lambda i: (i, 0)),
            pl.BlockSpec((tm, D_PLE), lambda i: (i, 0)),
            pl.BlockSpec((D_FF, D_MODEL), lambda i: (0, 0), **const),
            pl.BlockSpec((1, D_MODEL), lambda i: (0, 0)),
            pl.BlockSpec((D_MODEL, D_MODEL), lambda i: (0, 0), **const),
            pl.BlockSpec((D_PLE, D_MODEL), lambda i: (0, 0), **const),
            pl.BlockSpec((1, D_MODEL), lambda i: (0, 0)),
        ],
        out_specs=pl.BlockSpec((tm, D_MODEL), lambda i: (i, 0)),
        out_shape=jax.ShapeDtypeStruct((m, D_MODEL), F32),
        compiler_params=pltpu.CompilerParams(
            dimension_semantics=("parallel",),
            vmem_limit_bytes=VMEM_LIMIT),
        name="ffn_down",
    )(h, x1, p, w_down, g_ple, w_pg, w_ple, g_final)


def _tiles(bsz, t):
    if t >= 1024:
        return dict(nb=1, tm_in=1024, tm_mix=256, tm_up=1024, tq=256, diag=128)
    return dict(nb=bsz, tm_in=t, tm_mix=t, tm_up=t, tq=t, diag=t)


def _layer(x, p, conv_state, gla_state, ffn_state, w, g_final):
    bsz, t, _ = x.shape
    cfg = _tiles(bsz, t)
    nb = cfg["nb"]
    proj, loga = _inproj(x, w["norm_mix"], w["w_all"], w["w_tail"], w["w_alr"], w["w_g2"], w["b_gate"],
                         nb=nb, tm=cfg["tm_in"], tn=1024)
    og, new_s = _gla(proj, loga, w["gla_norm"], gla_state, tq=cfg["tq"], diag=cfg["diag"])
    x1, new_conv = _mix(proj, og, x, w["conv_a_w"], conv_state, w["w_a_out"], w["w_b_out"], w["w_o"],
                        nb=nb, tm=cfg["tm_mix"])
    h, nf_v, nf_g = _ffn_up(x1, w["norm_ffn"], w["w_up"], w["ffn_conv_w"], w["ffn_conv_b"], ffn_state,
                            nb=nb, tm=cfg["tm_up"], tn=512)
    m = bsz * t
    y = _ffn_down(h.reshape(m, D_FF), x1.reshape(m, D_MODEL), p.reshape(m, D_PLE),
                  w["w_down"], w["norm_ple"], w["w_ple_gate"], w["w_ple"], g_final, tm=256)
    new_ffn = jnp.concatenate([nf_v[:, -1], nf_g[:, -1]], axis=-1)
    return y.reshape(bsz, t, D_MODEL), new_conv, new_s, new_ffn


def kernel(x_prompt, x_sample, p_prompt, p_sample, state_conv_a, state_gla, state_ffn_conv, norm_mix, w_in, conv_a_w, w_a_out, w_gate2, b_gate, gla_norm, w_b_out, w_o, norm_ffn, w_up, ffn_conv_w, ffn_conv_b, w_down, norm_ple, w_ple_gate, w_ple, norm_final):
    depth = w_in.shape[0]
    assert depth == 1, "the final norm is fused into the last layer; one layer supported"
    i = 0
    w_in_i = w_in[i]
    w_alr = jnp.pad(w_in_i[:, ALR_OFF:ALR_OFF + GATE_RANK], ((0, 0), (0, LANE - GATE_RANK)))
    w = dict(
        norm_mix=norm_mix[i][None],
        w_all=w_in_i.astype(BF16),
        w_tail=w_in_i[:, ALR_OFF + GATE_RANK:].astype(BF16),
        w_alr=w_alr.astype(BF16),
        w_g2=jnp.pad(w_gate2[i], ((0, LANE - GATE_RANK), (0, 0))).astype(BF16),
        b_gate=b_gate[i][None],
        gla_norm=gla_norm[i][None],
        conv_a_w=conv_a_w[i],
        w_a_out=w_a_out[i].astype(BF16),
        w_b_out=w_b_out[i].astype(BF16),
        w_o=w_o[i].astype(BF16),
        norm_ffn=norm_ffn[i][None],
        w_up=w_up[i].astype(BF16),
        ffn_conv_w=ffn_conv_w[i],
        ffn_conv_b=ffn_conv_b[i][None],
        w_down=w_down[i].astype(BF16),
        norm_ple=norm_ple[i][None],
        w_ple_gate=w_ple_gate[i].astype(BF16),
        w_ple=w_ple[i].astype(BF16),
    )
    g_final = norm_final[None]
    yp, c1, s1, f1 = _layer(x_prompt, p_prompt[i], None, None, None, w, g_final)
    ys, c2, s2, f2 = _layer(x_sample, p_sample[i], state_conv_a[i], state_gla[i], state_ffn_conv[i],
                            w, g_final)
    return (yp, ys, c1[None], s1[None], f1[None], c2[None], s2[None], f2[None])
```

```python
import functools
import math

import jax
import jax.numpy as jnp
from jax import lax
from jax.experimental import pallas as pl
from jax.experimental.pallas import tpu as pltpu

F32 = jnp.float32
BF16 = jnp.bfloat16

D_MODEL = 2048
CONV_W = 3
GLA_HEADS = 4
GLA_DK = D_MODEL // 2
GLA_DV = D_MODEL
GLA_HK = GLA_DK // GLA_HEADS
GLA_HV = GLA_DV // GLA_HEADS
GATE_RANK = 16
GATE_NORMALIZER = 16.0
D_FF = 5632
D_PLE = 256
EPS = 1e-6

D_MAIN = 3 * D_MODEL + 2 * GLA_DK + 2 * GLA_DV + 2 * D_MODEL
OFF_BA, OFF_CA, OFF_VA = 0, D_MODEL, 2 * D_MODEL
OFF_Q = 3 * D_MODEL
OFF_K = OFF_Q + GLA_DK
OFF_V = OFF_K + GLA_DK
OFF_GO = OFF_V + GLA_DV
OFF_MA = OFF_GO + GLA_DV
OFF_MB = OFF_MA + D_MODEL
ALR_OFF = OFF_MA
LANE = 128
SUBLANES = 8
ROW_PARTS = 4
BF16_ROWS = 16
VMEM_LIMIT = 56 * 1024 * 1024

LOG2_E = math.log2(math.e)
Q_LOG2_SCALE = -0.5 * math.log2(GLA_HK)

NT_DIMS = (((1,), (1,)), ((), ()))
TN_DIMS = (((0,), (0,)), ((), ()))


def _rms_scale(x):
    return lax.rsqrt(jnp.mean(x * x, axis=-1, keepdims=True) + EPS)


def _sigmoid(x):
    return 1.0 / (1.0 + jnp.exp(-x))


def _log_sigmoid(z):
    return jnp.minimum(z, 0.0) - jnp.log(1.0 + jnp.exp(-jnp.abs(z)))


def _dot(a, b):
    return jnp.dot(a, b, preferred_element_type=F32)


def _split_bf16(x):
    hi = x.astype(BF16)
    return hi, (x - hi.astype(F32)).astype(BF16)


def _conv3(u, h0, h1, w):
    r1 = pltpu.roll(u, 1, axis=0)
    r2 = pltpu.roll(u, 2, axis=0)
    rows = lax.broadcasted_iota(jnp.int32, (SUBLANES, 1), 0)
    top1 = jnp.where(rows == 0, h1, r1[:SUBLANES])
    top2 = jnp.where(rows == 0, h0, jnp.where(rows == 1, h1, r2[:SUBLANES]))
    sh1 = jnp.concatenate([top1, r1[SUBLANES:]], axis=0)
    sh2 = jnp.concatenate([top2, r2[SUBLANES:]], axis=0)
    return w[0:1] * sh2 + w[1:2] * sh1 + w[2:3] * u


def _inproj_kernel(x_ref, g_ref, whead_ref, wtail_ref, walr_ref, wg2_ref, bg_ref, proj_ref, loga_ref,
                   xn_ref, *, nb, tm, n_head):
    j = pl.program_id(2)
    rows = nb * tm

    @pl.when(j == 0)
    def _():
        x = x_ref[...].reshape(rows, D_MODEL)
        xn = (x * _rms_scale(x) * g_ref[...]).astype(BF16)
        xn_ref[...] = xn
        a_lr = _dot(xn, walr_ref[...])
        z = _dot(a_lr.astype(BF16), wg2_ref[...]) + bg_ref[...]
        loga_ref[...] = (_log_sigmoid(z) * (1.0 / GATE_NORMALIZER)).reshape(nb, tm, GLA_DK)

    tn = proj_ref.shape[-1]

    @pl.when(j < n_head)
    def _():
        proj_ref[...] = _dot(xn_ref[...], whead_ref[...]).astype(BF16).reshape(nb, tm, tn)

    @pl.when(j >= n_head)
    def _():
        proj_ref[...] = _dot(xn_ref[...], wtail_ref[...]).astype(BF16).reshape(nb, tm, tn)


def _inproj(x, g, w_all, w_tail, w_alr, w_g2, b_gate, *, nb, tm, tn):
    bsz, t, _ = x.shape
    n_head = ALR_OFF // tn
    grid = (bsz // nb, t // tm, D_MAIN // tn)
    return pl.pallas_call(
        functools.partial(_inproj_kernel, nb=nb, tm=tm, n_head=n_head),
        grid=grid,
        in_specs=[
            pl.BlockSpec((nb, tm, D_MODEL), lambda b, i, j: (b, i, 0)),
            pl.BlockSpec((1, D_MODEL), lambda b, i, j: (0, 0)),
            pl.BlockSpec((D_MODEL, tn), lambda b, i, j: (0, jnp.minimum(j, n_head - 1))),
            pl.BlockSpec((D_MODEL, tn), lambda b, i, j: (0, jnp.maximum(j - n_head, 0))),
            pl.BlockSpec((D_MODEL, LANE), lambda b, i, j: (0, 0)),
            pl.BlockSpec((LANE, GLA_DK), lambda b, i, j: (0, 0)),
            pl.BlockSpec((1, GLA_DK), lambda b, i, j: (0, 0)),
        ],
        out_specs=[
            pl.BlockSpec((nb, tm, tn), lambda b, i, j: (b, i, j)),
            pl.BlockSpec((nb, tm, GLA_DK), lambda b, i, j: (b, i, 0)),
        ],
        out_shape=[
            jax.ShapeDtypeStruct((bsz, t, D_MAIN), BF16),
            jax.ShapeDtypeStruct((bsz, t, GLA_DK), F32),
        ],
        scratch_shapes=[pltpu.VMEM((nb * tm, D_MODEL), BF16)],
        compiler_params=pltpu.CompilerParams(
            dimension_semantics=("parallel", "parallel", "arbitrary"),
            vmem_limit_bytes=VMEM_LIMIT),
        name="inproj",
    )(x, g, w_all, w_tail, w_alr, w_g2, b_gate)


def _gla_head(q, k, v, la, s_old, *, diag):
    rows = q.shape[0]
    n_d = rows // diag
    r_i = lax.broadcasted_iota(jnp.int32, (rows, rows), 0)
    c_i = lax.broadcasted_iota(jnp.int32, (rows, rows), 1)
    tril = (r_i >= c_i).astype(BF16)
    la_hi, la_lo = _split_bf16(la * LOG2_E)
    b = _dot(tril, la_hi) + _dot(tril, la_lo)
    b_end = b[rows - 1:rows, :]

    causal_d = (lax.broadcasted_iota(jnp.int32, (diag, diag), 0)
                >= lax.broadcasted_iota(jnp.int32, (diag, diag), 1))
    a_c, k_c, cen = [], [], []
    for d in range(n_d):
        lo = d * diag
        mid = lo + diag // 2
        c = b[mid - 1:mid, :]
        bd = b[lo:lo + diag, :]
        cen.append(c)
        a_c.append(q[lo:lo + diag, :] * jnp.exp2(bd - (c - Q_LOG2_SCALE)))
        k_c.append(k[lo:lo + diag, :] * jnp.exp2(c - bd))

    def block(lo_d, hi_d):
        if hi_d - lo_d == 1:
            s = lax.dot_general(a_c[lo_d].astype(BF16), k_c[lo_d].astype(BF16), NT_DIMS,
                                preferred_element_type=F32)
            return jnp.where(causal_d, s, 0.0)
        mid_d = (lo_d + hi_d) // 2
        ref = b[mid_d * diag - 1:mid_d * diag, :]
        lhs = [a_c[d] * jnp.exp2(cen[d] - ref) for d in range(mid_d, hi_d)]
        rhs = [k_c[d] * jnp.exp2(ref - cen[d]) for d in range(lo_d, mid_d)]
        lhs = (lhs[0] if len(lhs) == 1 else jnp.concatenate(lhs, axis=0)).astype(BF16)
        rhs = (rhs[0] if len(rhs) == 1 else jnp.concatenate(rhs, axis=0)).astype(BF16)
        off = lax.dot_general(lhs, rhs, NT_DIMS, preferred_element_type=F32)
        top = jnp.concatenate([block(lo_d, mid_d), jnp.zeros_like(off)], axis=1)
        bot = jnp.concatenate([off, block(mid_d, hi_d)], axis=1)
        return jnp.concatenate([top, bot], axis=0)

    p = block(0, n_d).astype(BF16)
    qt = [a_c[d] * jnp.exp2(cen[d]) for d in range(n_d)]
    kt = [k_c[d] * jnp.exp2(b_end - cen[d]) for d in range(n_d)]
    qt = (qt[0] if n_d == 1 else jnp.concatenate(qt, axis=0)).astype(BF16)
    kt = (kt[0] if n_d == 1 else jnp.concatenate(kt, axis=0)).astype(BF16)
    o = _dot(qt, s_old.astype(BF16)) + _dot(p, v)
    e_rep = jnp.broadcast_to(b_end * (1.0 / BF16_ROWS), (BF16_ROWS, b_end.shape[1]))
    e_hi, e_lo = _split_bf16(e_rep)
    ones_col = jnp.ones((BF16_ROWS, LANE), BF16)
    dcol = (lax.dot_general(e_hi, ones_col, TN_DIMS, preferred_element_type=F32)
            + lax.dot_general(e_lo, ones_col, TN_DIMS, preferred_element_type=F32))
    decay = jnp.tile(jnp.exp2(dcol), (1, v.shape[1] // LANE))
    s_new = s_old * decay + lax.dot_general(kt, v, TN_DIMS, preferred_element_type=F32)
    return o, s_new


def _gla_kernel(*refs, diag, has_state):
    if has_state:
        q_ref, k_ref, v_ref, go_ref, la_ref, gn_ref, s0_ref, o_ref, sout_ref, s_ref = refs
    else:
        q_ref, k_ref, v_ref, go_ref, la_ref, gn_ref, o_ref, sout_ref, s_ref = refs
    i = pl.program_id(1)

    @pl.when(i == 0)
    def _():
        if has_state:
            s_ref[...] = s0_ref[0]
        else:
            s_ref[...] = jnp.zeros_like(s_ref)

    for h in range(GLA_HEADS):
        ck = pl.ds(h * GLA_HK, GLA_HK)
        cv = pl.ds(h * GLA_HV, GLA_HV)
        q = q_ref[0, :, ck].astype(F32)
        k = k_ref[0, :, ck].astype(F32)
        o, s_new = _gla_head(q, k, v_ref[0, :, cv], la_ref[0, :, ck], s_ref[h], diag=diag)
        s_ref[h] = s_new
        g = go_ref[0, :, cv].astype(F32)
        o_ref[0, :, cv] = (o * _rms_scale(o) * gn_ref[:, cv] * (g * _sigmoid(g))).astype(BF16)

    @pl.when(i == pl.num_programs(1) - 1)
    def _():
        sout_ref[0] = s_ref[...]


def _gla(proj, loga, gla_norm, state, *, tq, diag):
    bsz, t, _ = proj.shape
    has_state = state is not None
    grid = (bsz, t // tq)
    state_spec = pl.BlockSpec((1, GLA_HEADS, GLA_HK, GLA_HV), lambda b, i: (b, 0, 0, 0))
    in_specs = [
        pl.BlockSpec((1, tq, GLA_DK), lambda b, i: (b, i, OFF_Q // GLA_DK)),
        pl.BlockSpec((1, tq, GLA_DK), lambda b, i: (b, i, OFF_K // GLA_DK)),
        pl.BlockSpec((1, tq, GLA_DV), lambda b, i: (b, i, OFF_V // GLA_DV)),
        pl.BlockSpec((1, tq, GLA_DV), lambda b, i: (b, i, OFF_GO // GLA_DV)),
        pl.BlockSpec((1, tq, GLA_DK), lambda b, i: (b, i, 0)),
        pl.BlockSpec((1, GLA_DV), lambda b, i: (0, 0)),
    ]
    args = [proj, proj, proj, proj, loga, gla_norm]
    if has_state:
        in_specs.append(state_spec)
        args.append(state)
    return pl.pallas_call(
        functools.partial(_gla_kernel, diag=diag, has_state=has_state),
        grid=grid,
        in_specs=in_specs,
        out_specs=[
            pl.BlockSpec((1, tq, GLA_DV), lambda b, i: (b, i, 0)),
            state_spec,
        ],
        out_shape=[
            jax.ShapeDtypeStruct((bsz, t, GLA_DV), BF16),
            jax.ShapeDtypeStruct((bsz, GLA_HEADS, GLA_HK, GLA_HV), F32),
        ],
        scratch_shapes=[pltpu.VMEM((GLA_HEADS, GLA_HK, GLA_HV), F32)],
        compiler_params=pltpu.CompilerParams(
            dimension_semantics=("parallel", "arbitrary"),
            vmem_limit_bytes=VMEM_LIMIT),
        name="gla",
    )(*args)


def _mix_kernel(*refs, nb, tm, has_state):
    if has_state:
        (ba_ref, ca_ref, va_ref, ma_ref, mb_ref, og_ref, x_ref, cw_ref, st_ref,
         wa_ref, wb_ref, wo_ref, x1_ref, newc_ref, carry_ref) = refs
    else:
        (ba_ref, ca_ref, va_ref, ma_ref, mb_ref, og_ref, x_ref, cw_ref,
         wa_ref, wb_ref, wo_ref, x1_ref, newc_ref, carry_ref) = refs
    i = pl.program_id(1)
    rows = nb * tm
    first = i == 0
    cw = cw_ref[...]

    @pl.when(first)
    def _():
        carry_ref[...] = jnp.zeros_like(carry_ref)

    conv_parts = []
    for s in range(nb):
        cv = ca_ref[s].astype(F32) * va_ref[s].astype(F32)
        init = st_ref[s] if has_state else jnp.zeros((CONV_W - 1, D_MODEL), F32)
        halo = jnp.where(first, init, carry_ref[s, 0:2, :])
        conv_parts.append(_conv3(cv, halo[0:1], halo[1:2], cw))
        tail = cv[tm - 2:tm, :]
        carry_ref[s, 0:2, :] = tail
        newc_ref[s] = tail
    conv = conv_parts[0] if nb == 1 else jnp.concatenate(conv_parts, axis=0)

    b_a = ba_ref[...].reshape(rows, D_MODEL).astype(F32)
    y_a = _dot((b_a * conv).astype(BF16), wa_ref[...])
    y_b = _dot(og_ref[...].reshape(rows, D_MODEL), wb_ref[...])
    m_a = ma_ref[...].reshape(rows, D_MODEL).astype(F32)
    m_b = mb_ref[...].reshape(rows, D_MODEL).astype(F32)
    z = _sigmoid(m_a) * y_a + _sigmoid(m_b) * y_b
    x1 = x_ref[...].reshape(rows, D_MODEL) + _dot(z.astype(BF16), wo_ref[...])
    x1_ref[...] = x1.reshape(nb, tm, D_MODEL)


def _mix(proj, og, x, conv_w, state, w_a, w_b, w_o, *, nb, tm):
    bsz, t, _ = x.shape
    has_state = state is not None
    grid = (bsz // nb, t // tm)

    def col(off):
        return pl.BlockSpec((nb, tm, D_MODEL), lambda b, i: (b, i, off // D_MODEL))

    row_spec = pl.BlockSpec((nb, tm, D_MODEL), lambda b, i: (b, i, 0))
    w_spec = pl.BlockSpec((D_MODEL, D_MODEL), lambda b, i: (0, 0), pipeline_mode=pl.Buffered(1))
    in_specs = [col(OFF_BA), col(OFF_CA), col(OFF_VA), col(OFF_MA), col(OFF_MB), row_spec, row_spec,
                pl.BlockSpec((CONV_W, D_MODEL), lambda b, i: (0, 0))]
    args = [proj, proj, proj, proj, proj, og, x, conv_w]
    if has_state:
        in_specs.append(pl.BlockSpec((nb, CONV_W - 1, D_MODEL), lambda b, i: (b, 0, 0)))
        args.append(state)
    in_specs += [w_spec, w_spec, w_spec]
    args += [w_a, w_b, w_o]
    return pl.pallas_call(
        functools.partial(_mix_kernel, nb=nb, tm=tm, has_state=has_state),
        grid=grid,
        in_specs=in_specs,
        out_specs=[
            row_spec,
            pl.BlockSpec((nb, CONV_W - 1, D_MODEL), lambda b, i: (b, 0, 0)),
        ],
        out_shape=[
            jax.ShapeDtypeStruct((bsz, t, D_MODEL), F32),
            jax.ShapeDtypeStruct((bsz, CONV_W - 1, D_MODEL), F32),
        ],
        scratch_shapes=[pltpu.VMEM((nb, 8, D_MODEL), F32)],
        compiler_params=pltpu.CompilerParams(
            dimension_semantics=("parallel", "arbitrary"),
            vmem_limit_bytes=VMEM_LIMIT),
        name="mix",
    )(*args)


def _ffn_up_kernel(*refs, nb, tm, tn, has_state):
    if has_state:
        (x1_ref, g_ref, wv_ref, wg_ref, cw_ref, cb_ref, stv_ref, stg_ref,
         h_ref, nfv_ref, nfg_ref, hn_ref, cv_ref, cg_ref) = refs
    else:
        (x1_ref, g_ref, wv_ref, wg_ref, cw_ref, cb_ref,
         h_ref, nfv_ref, nfg_ref, hn_ref, cv_ref, cg_ref) = refs
    i = pl.program_id(1)
    j = pl.program_id(2)
    rows = nb * tm
    first = i == 0
    cols_v = pl.ds(pl.multiple_of(j * tn, tn), tn)
    cols_g = pl.ds(pl.multiple_of(D_FF + j * tn, tn), tn)

    @pl.when(j == 0)
    def _():
        x1 = x1_ref[...].reshape(rows, D_MODEL)
        hn_ref[...] = (x1 * _rms_scale(x1) * g_ref[...]).astype(BF16)

    @pl.when(first)
    def _():
        cv_ref[j] = jnp.zeros(cv_ref.shape[1:], F32)
        cg_ref[j] = jnp.zeros(cg_ref.shape[1:], F32)

    n_parts = ROW_PARTS if nb == 1 and tm % (ROW_PARTS * SUBLANES) == 0 else 1
    part = rows // n_parts
    ug = [_dot(hn_ref[r * part:(r + 1) * part, :], wg_ref[...]) for r in range(n_parts)]
    uv = [_dot(hn_ref[r * part:(r + 1) * part, :], wv_ref[...]) for r in range(n_parts)]
    cwv = cw_ref[:, cols_v]
    cwg = cw_ref[:, cols_g]
    bias_v = cb_ref[:, cols_v]
    bias_g = cb_ref[:, cols_g]
    zero = jnp.zeros((CONV_W - 1, tn), F32)
    segs = [(0, r) for r in range(n_parts)] if n_parts > 1 else [(s, None) for s in range(nb)]
    act, halo_g, halo_v = {}, None, None
    for s, r in segs:
        ugs = ug[r] if r is not None else ug[0][s * tm:(s + 1) * tm]
        if not r:
            halo_g = jnp.where(first, stg_ref[s] if has_state else zero, cg_ref[j, s, 0:2, :])
        gate = _conv3(ugs, halo_g[0:1], halo_g[1:2], cwg) + bias_g
        act[(s, r)] = gate * _sigmoid(gate)
        halo_g = ugs[ugs.shape[0] - 2:, :]
        if r is None or r == n_parts - 1:
            cg_ref[j, s, 0:2, :] = halo_g
            nfg_ref[s, 0, :, cols_v] = halo_g
    for s, r in segs:
        uvs = uv[r] if r is not None else uv[0][s * tm:(s + 1) * tm]
        if not r:
            halo_v = jnp.where(first, stv_ref[s] if has_state else zero, cv_ref[j, s, 0:2, :])
        val = _conv3(uvs, halo_v[0:1], halo_v[1:2], cwv) + bias_v
        halo_v = uvs[uvs.shape[0] - 2:, :]
        if r is None or r == n_parts - 1:
            cv_ref[j, s, 0:2, :] = halo_v
            nfv_ref[s, 0, :, cols_v] = halo_v
        out = (act[(s, r)] * val).astype(BF16)
        if r is None:
            h_ref[s] = out
        else:
            h_ref[s, r * part:(r + 1) * part, :] = out


def _ffn_up(x1, g, w_up, conv_w, conv_b, state, *, nb, tm, tn):
    bsz, t, _ = x1.shape
    has_state = state is not None
    n_ct = D_FF // tn
    grid = (bsz // nb, t // tm, n_ct)
    in_specs = [
        pl.BlockSpec((nb, tm, D_MODEL), lambda b, i, j: (b, i, 0)),
        pl.BlockSpec((1, D_MODEL), lambda b, i, j: (0, 0)),
        pl.BlockSpec((D_MODEL, tn), lambda b, i, j: (0, j)),
        pl.BlockSpec((D_MODEL, tn), lambda b, i, j: (0, n_ct + j)),
        pl.BlockSpec((CONV_W, 2 * D_FF), lambda b, i, j: (0, 0)),
        pl.BlockSpec((1, 2 * D_FF), lambda b, i, j: (0, 0)),
    ]
    args = [x1, g, w_up, w_up, conv_w, conv_b]
    if has_state:
        in_specs += [
            pl.BlockSpec((nb, CONV_W - 1, tn), lambda b, i, j: (b, 0, j)),
            pl.BlockSpec((nb, CONV_W - 1, tn), lambda b, i, j: (b, 0, n_ct + j)),
        ]
        args += [state, state]
    tail_spec = pl.BlockSpec((nb, 1, CONV_W - 1, D_FF), lambda b, i, j: (b, i, 0, 0))
    return pl.pallas_call(
        functools.partial(_ffn_up_kernel, nb=nb, tm=tm, tn=tn, has_state=has_state),
        grid=grid,
        in_specs=in_specs,
        out_specs=[
            pl.BlockSpec((nb, tm, tn), lambda b, i, j: (b, i, j)),
            tail_spec,
            tail_spec,
        ],
        out_shape=[
            jax.ShapeDtypeStruct((bsz, t, D_FF), BF16),
            jax.ShapeDtypeStruct((bsz, t // tm, CONV_W - 1, D_FF), F32),
            jax.ShapeDtypeStruct((bsz, t // tm, CONV_W - 1, D_FF), F32),
        ],
        scratch_shapes=[
            pltpu.VMEM((nb * tm, D_MODEL), BF16),
            pltpu.VMEM((n_ct, nb, 8, tn), F32),
            pltpu.VMEM((n_ct, nb, 8, tn), F32),
        ],
        compiler_params=pltpu.CompilerParams(
            dimension_semantics=("parallel", "arbitrary", "arbitrary"),
            vmem_limit_bytes=VMEM_LIMIT),
        name="ffn_up",
    )(*args)


def _ffn_down_kernel(h_ref, x1_ref, p_ref, wd_ref, gp_ref, wpg_ref, wple_ref, gf_ref, y_ref):
    x2 = x1_ref[...] + _dot(h_ref[...], wd_ref[...])
    pn = (x2 * _rms_scale(x2) * gp_ref[...]).astype(BF16)
    gate = _sigmoid(_dot(pn, wpg_ref[...]))
    pe = _dot(p_ref[...].astype(BF16), wple_ref[...])
    x3 = x2 + gate * pe
    y_ref[...] = x3 * _rms_scale(x3) * gf_ref[...]


def _ffn_down(h, x1, p, w_down, g_ple, w_pg, w_ple, g_final, *, tm):
    m = x1.shape[0]
    const = dict(pipeline_mode=pl.Buffered(1))
    return pl.pallas_call(
        _ffn_down_kernel,
        grid=(m // tm,),
        in_specs=[
            pl.BlockSpec((tm, D_FF), lambda i: (i, 0)),
            pl.BlockSpec((tm, D_MODEL), lambda i: (i, 0)),
            pl.BlockSpec((tm, D_PLE), lambda i: (i, 0)),
            pl.BlockSpec((D_FF, D_MODEL), lambda i: (0, 0), **const),
            pl.BlockSpec((1, D_MODEL), lambda i: (0, 0)),
            pl.BlockSpec((D_MODEL, D_MODEL), lambda i: (0, 0), **const),
            pl.BlockSpec((D_PLE, D_MODEL), lambda i: (0, 0), **const),
            pl.BlockSpec((1, D_MODEL), lambda i: (0, 0)),
        ],
        out_specs=pl.BlockSpec((tm, D_MODEL), lambda i: (i, 0)),
        out_shape=jax.ShapeDtypeStruct((m, D_MODEL), F32),
        compiler_params=pltpu.CompilerParams(
            dimension_semantics=("parallel",),
            vmem_limit_bytes=VMEM_LIMIT),
        name="ffn_down",
    )(h, x1, p, w_down, g_ple, w_pg, w_ple, g_final)


def _tiles(bsz, t):
    if t >= 1024:
        return dict(nb=1, tm_in=1024, tm_mix=256, tm_up=1024, tq=256, diag=128)
    return dict(nb=bsz, tm_in=t, tm_mix=t, tm_up=t, tq=t, diag=t)


def _layer(x, p, conv_state, gla_state, ffn_state, w, g_final):
    bsz, t, _ = x.shape
    cfg = _tiles(bsz, t)
    nb = cfg["nb"]
    proj, loga = _inproj(x, w["norm_mix"], w["w_all"], w["w_tail"], w["w_alr"], w["w_g2"], w["b_gate"],
                         nb=nb, tm=cfg["tm_in"], tn=1024)
    og, new_s = _gla(proj, loga, w["gla_norm"], gla_state, tq=cfg["tq"], diag=cfg["diag"])
    x1, new_conv = _mix(proj, og, x, w["conv_a_w"], conv_state, w["w_a_out"], w["w_b_out"], w["w_o"],
                        nb=nb, tm=cfg["tm_mix"])
    h, nf_v, nf_g = _ffn_up(x1, w["norm_ffn"], w["w_up"], w["ffn_conv_w"], w["ffn_conv_b"], ffn_state,
                            nb=nb, tm=cfg["tm_up"], tn=512)
    m = bsz * t
    y = _ffn_down(h.reshape(m, D_FF), x1.reshape(m, D_MODEL), p.reshape(m, D_PLE),
                  w["w_down"], w["norm_ple"], w["w_ple_gate"], w["w_ple"], g_final, tm=256)
    new_ffn = jnp.concatenate([nf_v[:, -1], nf_g[:, -1]], axis=-1)
    return y.reshape(bsz, t, D_MODEL), new_conv, new_s, new_ffn


def kernel(x_prompt, x_sample, p_prompt, p_sample, state_conv_a, state_gla, state_ffn_conv, norm_mix, w_in, conv_a_w, w_a_out, w_gate2, b_gate, gla_norm, w_b_out, w_o, norm_ffn, w_up, ffn_conv_w, ffn_conv_b, w_down, norm_ple, w_ple_gate, w_ple, norm_final):
    depth = w_in.shape[0]
    assert depth == 1, "the final norm is fused into the last layer; one layer supported"
    i = 0
    w_in_i = w_in[i]
    w_alr = jnp.pad(w_in_i[:, ALR_OFF:ALR_OFF + GATE_RANK], ((0, 0), (0, LANE - GATE_RANK)))
    w = dict(
        norm_mix=norm_mix[i][None],
        w_all=w_in_i.astype(BF16),
        w_tail=w_in_i[:, ALR_OFF + GATE_RANK:].astype(BF16),
        w_alr=w_alr.astype(BF16),
        w_g2=jnp.pad(w_gate2[i], ((0, LANE - GATE_RANK), (0, 0))).astype(BF16),
        b_gate=b_gate[i][None],
        gla_norm=gla_norm[i][None],
        conv_a_w=conv_a_w[i],
        w_a_out=w_a_out[i].astype(BF16),
        w_b_out=w_b_out[i].astype(BF16),
        w_o=w_o[i].astype(BF16),
        norm_ffn=norm_ffn[i][None],
        w_up=w_up[i].astype(BF16),
        ffn_conv_w=ffn_conv_w[i],
        ffn_conv_b=ffn_conv_b[i][None],
        w_down=w_down[i].astype(BF16),
        norm_ple=norm_ple[i][None],
        w_ple_gate=w_ple_gate[i].astype(BF16),
        w_ple=w_ple[i].astype(BF16),
    )
    g_final = norm_final[None]
    yp, c1, s1, f1 = _layer(x_prompt, p_prompt[i], None, None, None, w, g_final)
    ys, c2, s2, f2 = _layer(x_sample, p_sample[i], state_conv_a[i], state_gla[i], state_ffn_conv[i],
                            w, g_final)
    return (yp, ys, c1[None], s1[None], f1[None], c2[None], s2[None], f2[None])
```
